```python
import math
import jax, jax.numpy as jnp
from jax import lax
import numpy as np

D_MODEL = 4096
BATCH = 4
SEQ = 4096
DEPTH = 2
DEC_BATCH = 8
DEC_SEQ = 2048
PAST_LEN = 128

HEAD_DIM = 128
GRID_W = 64
NA_HEADS = 12
NA_WIN_R = 8
NA_WIN_C = 16
DIFF_HEADS = 4
GQA_Q_HEADS = 12
GQA_KV_HEADS = 4
D_FF = 4 * D_MODEL
ROPE_THETA = 10000.0
Q_BLOCK = 128
NORM_EPS = 1e-6

W_A = NA_HEADS * HEAD_DIM
W_B_QK = 2 * DIFF_HEADS * HEAD_DIM
W_B_V = DIFF_HEADS * 2 * HEAD_DIM
W_C_Q = GQA_Q_HEADS * HEAD_DIM
W_C_KV = GQA_KV_HEADS * HEAD_DIM
SPLIT_SIZES = (W_A, W_A, W_A, W_B_QK, W_B_QK, W_B_V, W_C_Q, W_C_KV, W_C_KV, D_MODEL, D_MODEL, D_MODEL)
N_IN = 3 * W_A + 2 * W_B_QK + W_B_V + W_C_Q + 2 * W_C_KV + 3 * D_MODEL

kernel_name = 'hybrid_natten_diff_gqa_encoder'


def rms_norm(x, g):
    xf = x.astype(jnp.float32)
    y = xf * lax.rsqrt(jnp.mean(xf * xf, axis=-1, keepdims=True) + NORM_EPS)
    return (y * g.astype(jnp.float32)).astype(x.dtype)


def rope(x, pos):
    dr = x.shape[-1]
    inv = ROPE_THETA ** (-jnp.arange(0, dr, 2, dtype=jnp.float32) / dr)
    ang = pos.astype(jnp.float32)[:, None] * inv[None, :]
    cos = jnp.cos(ang)[None, :, None, :]
    sin = jnp.sin(ang)[None, :, None, :]
    x1, x2 = jnp.split(x.astype(jnp.float32), 2, axis=-1)
    return jnp.concatenate([x1 * cos - x2 * sin, x2 * cos + x1 * sin], axis=-1).astype(x.dtype)


def axial_rope(x, pos):
    half = x.shape[-1] // 2
    return jnp.concatenate([rope(x[..., :half], pos // GRID_W), rope(x[..., half:], pos % GRID_W)], axis=-1)


def neighbourhood_attention(q, k, v, rpb):
    b, s, h, dh = q.shape
    rows = s // GRID_W
    kr = min(NA_WIN_R, rows)
    qg = q.reshape(b, rows, GRID_W, h, dh)
    kg = k.reshape(b, rows, GRID_W, h, dh)
    vg = v.reshape(b, rows, GRID_W, h, dh)
    col = jnp.arange(GRID_W)
    col_start = jnp.clip(col - NA_WIN_C // 2, 0, GRID_W - NA_WIN_C)
    col_mask = (col[None, :] >= col_start[:, None]) & (col[None, :] < col_start[:, None] + NA_WIN_C)
    dc = jnp.clip(col[None, :] - col[:, None], -(NA_WIN_C - 1), NA_WIN_C - 1) + NA_WIN_C - 1
    rpb_c = rpb[:, :, dc]
    scale = dh ** -0.5

    def row_block(r):
        rs = jnp.clip(r - kr // 2, 0, rows - kr)
        q_r = lax.dynamic_index_in_dim(qg, r, axis=1, keepdims=False)
        k_r = lax.dynamic_slice_in_dim(kg, rs, kr, axis=1)
        v_r = lax.dynamic_slice_in_dim(vg, rs, kr, axis=1)
        dr = rs + jnp.arange(kr) - r + NA_WIN_R - 1
        bias = rpb_c[:, dr].transpose(0, 2, 1, 3)
        sc = jnp.einsum('bqhd,bkwhd->bhqkw', q_r, k_r, preferred_element_type=jnp.float32) * scale
        sc = sc + bias[None].astype(jnp.float32)
        sc = jnp.where(col_mask[None, None, :, None, :], sc, -jnp.inf)
        p = jax.nn.softmax(sc.reshape(b, h, GRID_W, kr * GRID_W), axis=-1).reshape(b, h, GRID_W, kr, GRID_W)
        return jnp.einsum('bhqkw,bkwhd->bqhd', p.astype(v.dtype), v_r)

    out = lax.map(row_block, jnp.arange(rows))
    return out.transpose(1, 0, 2, 3, 4).reshape(b, s, h * dh)


def diff_attention(q, k, v, lam, lam_init, sub_gain):
    b, s, h2, dh = q.shape
    hd = h2 // 2
    nblk = s // Q_BLOCK
    qb = q.reshape(b, nblk, Q_BLOCK, h2, dh).transpose(1, 0, 2, 3, 4)
    scale = dh ** -0.5

    def block(q_blk):
        sc = jnp.einsum('bqhd,bkhd->bhqk', q_blk, k, preferred_element_type=jnp.float32) * scale
        p = jax.nn.softmax(sc, axis=-1).reshape(b, hd, 2, Q_BLOCK, s)
        p = p[:, :, 0] - lam * p[:, :, 1]
        return jnp.einsum('bhqk,bkhe->bqhe', p.astype(v.dtype), v)

    o = lax.map(block, qb).transpose(1, 0, 2, 3, 4).reshape(b, s, hd, 2 * dh)
    o = rms_norm(o, sub_gain) * (1.0 - lam_init)
    return o.reshape(b, s, hd * 2 * dh)


def gqa_attention(q, k, v):
    b, s, hq, dh = q.shape
    hkv = k.shape[2]
    g = hq // hkv
    nblk = s // Q_BLOCK
    qb = q.reshape(b, nblk, Q_BLOCK, hkv, g, dh).transpose(1, 0, 2, 3, 4, 5)
    scale = dh ** -0.5

    def block(q_blk):
        sc = jnp.einsum('bqngd,btnd->bngqt', q_blk, k, preferred_element_type=jnp.float32) * scale
        p = jax.nn.softmax(sc, axis=-1)
        return jnp.einsum('bngqt,btnd->bqngd', p.astype(v.dtype), v)

    o = lax.map(block, qb).transpose(1, 0, 2, 3, 4, 5)
    return o.reshape(b, s, hq * dh)


def encoder_layer(x, lam_init, norm_mix, w_in, qn_a, kn_a, rpb, qn_b, kn_b, lam_q1, lam_k1, lam_q2, lam_k2,
                  subln_b, qn_c, kn_c, w_oa, w_ob, w_oc, w_out, norm_mlp, w_up, w_down):
    b, s, _ = x.shape
    h = rms_norm(x, norm_mix)
    proj = jnp.einsum('bsd,dn->bsn', h, w_in)
    points = []
    acc = 0
    for n in SPLIT_SIZES[:-1]:
        acc += n
        points.append(acc)
    qa, ka, va, qb, kb, vb, qc, kc, vc, ga, gb, gc = jnp.split(proj, points, axis=-1)
    pos = jnp.arange(s, dtype=jnp.int32)

    qa = rms_norm(qa.reshape(b, s, NA_HEADS, HEAD_DIM), qn_a)
    ka = rms_norm(ka.reshape(b, s, NA_HEADS, HEAD_DIM), kn_a)
    o_a = neighbourhood_attention(qa, ka, va.reshape(b, s, NA_HEADS, HEAD_DIM), rpb)

    qb = rope(rms_norm(qb.reshape(b, s, 2 * DIFF_HEADS, HEAD_DIM), qn_b), pos)
    kb = rope(rms_norm(kb.reshape(b, s, 2 * DIFF_HEADS, HEAD_DIM), kn_b), pos)
    f32 = jnp.float32
    lam = (jnp.exp(jnp.sum(lam_q1.astype(f32) * lam_k1.astype(f32)))
           - jnp.exp(jnp.sum(lam_q2.astype(f32) * lam_k2.astype(f32))) + lam_init)
    o_b = diff_attention(qb, kb, vb.reshape(b, s, DIFF_HEADS, 2 * HEAD_DIM), lam, lam_init, subln_b)

    qc = axial_rope(rms_norm(qc.reshape(b, s, GQA_Q_HEADS, HEAD_DIM), qn_c), pos)
    kc = axial_rope(rms_norm(kc.reshape(b, s, GQA_KV_HEADS, HEAD_DIM), kn_c), pos)
    o_c = gqa_attention(qc, kc, vc.reshape(b, s, GQA_KV_HEADS, HEAD_DIM))

    merged = (jax.nn.sigmoid(ga) * (o_a @ w_oa)
              + jax.nn.sigmoid(gb) * (o_b @ w_ob)
              + jax.nn.sigmoid(gc) * (o_c @ w_oc))
    x = x + merged @ w_out

    h = rms_norm(x, norm_mlp)
    u = jnp.square(jax.nn.relu(h @ w_up))
    return x + u @ w_down


def setup_inputs(seed: int = 0) -> dict:
    key = jax.random.key(seed)
    ks = jax.random.split(key, 24)
    f32 = jnp.float32

    def nrm(k, shape, scale):
        return jax.random.normal(k, shape, f32) * scale

    def gain(k, shape):
        return 1.0 + 0.05 * jax.random.normal(k, shape, f32)

    return {
        'x_prompt': nrm(ks[0], (BATCH, SEQ, D_MODEL), 1.0),
        'x_sample': nrm(ks[1], (DEC_BATCH, DEC_SEQ, D_MODEL), 1.0),
        'norm_mix': gain(ks[2], (DEPTH, D_MODEL)),
        'w_in': nrm(ks[3], (DEPTH, D_MODEL, N_IN), D_MODEL ** -0.5),
        'qn_a': gain(ks[4], (DEPTH, HEAD_DIM)),
        'kn_a': gain(ks[5], (DEPTH, HEAD_DIM)),
        'rpb': nrm(ks[6], (DEPTH, NA_HEADS, 2 * NA_WIN_R - 1, 2 * NA_WIN_C - 1), 0.1),
        'qn_b': gain(ks[7], (DEPTH, HEAD_DIM)),
        'kn_b': gain(ks[8], (DEPTH, HEAD_DIM)),
        'lam_q1': nrm(ks[9], (DEPTH, HEAD_DIM), 0.1),
        'lam_k1': nrm(ks[10], (DEPTH, HEAD_DIM), 0.1),
        'lam_q2': nrm(ks[11], (DEPTH, HEAD_DIM), 0.1),
        'lam_k2': nrm(ks[12], (DEPTH, HEAD_DIM), 0.1),
        'subln_b': gain(ks[13], (DEPTH, 2 * HEAD_DIM)),
        'qn_c': gain(ks[14], (DEPTH, HEAD_DIM)),
        'kn_c': gain(ks[15], (DEPTH, HEAD_DIM)),
        'w_oa': nrm(ks[16], (DEPTH, W_A, D_MODEL), W_A ** -0.5),
        'w_ob': nrm(ks[17], (DEPTH, W_B_V, D_MODEL), W_B_V ** -0.5),
        'w_oc': nrm(ks[18], (DEPTH, W_C_Q, D_MODEL), W_C_Q ** -0.5),
        'w_out': nrm(ks[19], (DEPTH, D_MODEL, D_MODEL), D_MODEL ** -0.5),
        'norm_mlp': gain(ks[20], (DEPTH, D_MODEL)),
        'w_up': nrm(ks[21], (DEPTH, D_MODEL, D_FF), D_MODEL ** -0.5),
        'w_down': nrm(ks[22], (DEPTH, D_FF, D_MODEL), D_FF ** -0.5),
    }


def reference(x_prompt, x_sample, norm_mix, w_in, qn_a, kn_a, rpb, qn_b, kn_b, lam_q1, lam_k1, lam_q2, lam_k2,
              subln_b, qn_c, kn_c, w_oa, w_ob, w_oc, w_out, norm_mlp, w_up, w_down):
    def trunk(x):
        for l in range(DEPTH):
            lam_init = 0.8 - 0.6 * math.exp(-0.3 * l)
            x = encoder_layer(x, lam_init, norm_mix[l], w_in[l], qn_a[l], kn_a[l], rpb[l], qn_b[l], kn_b[l],
                              lam_q1[l], lam_k1[l], lam_q2[l], lam_k2[l], subln_b[l], qn_c[l], kn_c[l],
                              w_oa[l], w_ob[l], w_oc[l], w_out[l], norm_mlp[l], w_up[l], w_down[l])
        return x

    y_prompt = trunk(x_prompt)
    y_sample = trunk(x_sample)
    return (y_prompt, y_sample)
```

```python
import functools
import math

import jax
import jax.numpy as jnp
from jax import lax
from jax.experimental import pallas as pl
from jax.experimental.pallas import tpu as pltpu

F32 = jnp.float32
BF16 = jnp.bfloat16

HEAD_DIM = 128
GRID_W = 64
NA_HEADS = 12
NA_WIN_R = 8
NA_WIN_C = 16
DIFF_HEADS = 4
GQA_Q_HEADS = 12
GQA_KV_HEADS = 4
GQA_GROUP = GQA_Q_HEADS // GQA_KV_HEADS
ROPE_THETA = 10000.0
NORM_EPS = 1e-6

W_A = NA_HEADS * HEAD_DIM
W_B_QK = 2 * DIFF_HEADS * HEAD_DIM
W_B_V = DIFF_HEADS * 2 * HEAD_DIM
W_C_Q = GQA_Q_HEADS * HEAD_DIM
W_C_KV = GQA_KV_HEADS * HEAD_DIM

OFF_QA = 0
OFF_KA = OFF_QA + W_A
OFF_VA = OFF_KA + W_A
OFF_QB = OFF_VA + W_A
OFF_KB = OFF_QB + W_B_QK
OFF_VB = OFF_KB + W_B_QK
OFF_QC = OFF_VB + W_B_V
OFF_KC = OFF_QC + W_C_Q
OFF_VC = OFF_KC + W_C_KV
OFF_GA = OFF_VC + W_C_KV

V7X_VMEM_LIMIT_BYTES = 56 * 1024 * 1024

QK_SCALE = HEAD_DIM ** -0.5
NEG_BIG = -1e30


def _params(*semantics):
    return pltpu.CompilerParams(dimension_semantics=semantics, vmem_limit_bytes=V7X_VMEM_LIMIT_BYTES)


def _rmsnorm_kernel(x_ref, g_ref, o_ref):
    x = x_ref[...]
    ms = jnp.mean(x * x, axis=-1, keepdims=True)
    o_ref[...] = (x * lax.rsqrt(ms + NORM_EPS) * g_ref[...]).astype(o_ref.dtype)


def _rmsnorm(x, gain, tm=256):
    t, d = x.shape
    tm = min(tm, t)
    return pl.pallas_call(
        _rmsnorm_kernel,
        grid=(t // tm,),
        in_specs=[pl.BlockSpec((tm, d), lambda i: (i, 0)), pl.BlockSpec((1, d), lambda i: (0, 0))],
        out_specs=pl.BlockSpec((tm, d), lambda i: (i, 0)),
        out_shape=jax.ShapeDtypeStruct((t, d), BF16),
        compiler_params=_params("parallel"),
        name="rmsnorm",
    )(x, gain.reshape(1, d).astype(F32))


def _epilogue(acc, kind, res_ref):
    if kind == "residual":
        return res_ref[...] + acc
    if kind == "relu2":
        return jnp.square(jnp.maximum(acc, 0.0))
    return acc


def _matmul_kernel(*refs, kind, nk):
    if kind == "residual":
        a_ref, b_ref, res_ref, o_ref = refs[:4]
        rest = refs[4:]
    else:
        a_ref, b_ref, o_ref = refs[:3]
        res_ref = None
        rest = refs[3:]
    part = jnp.dot(a_ref[...], b_ref[...], preferred_element_type=F32)
    if nk == 1:
        o_ref[...] = _epilogue(part, kind, res_ref).astype(o_ref.dtype)
        return
    (acc_ref,) = rest
    k = pl.program_id(2)

    @pl.when(k == 0)
    def _():
        acc_ref[...] = part

    @pl.when(jnp.logical_and(k > 0, k < nk - 1))
    def _():
        acc_ref[...] += part

    @pl.when(k == nk - 1)
    def _():
        o_ref[...] = _epilogue(acc_ref[...] + part, kind, res_ref).astype(o_ref.dtype)


def _matmul(a, b, *, kind="plain", residual=None, out_dtype=BF16, tm=1024, tn=1024, tk=2048, name="matmul"):
    m, kdim = a.shape
    n = b.shape[1]
    tm, tn, tk = min(tm, m), min(tn, n), min(tk, kdim)
    nk = kdim // tk
    in_specs = [pl.BlockSpec((tm, tk), lambda i, j, k: (i, k)), pl.BlockSpec((tk, tn), lambda i, j, k: (k, j))]
    args = [a, b]
    if kind == "residual":
        in_specs.append(pl.BlockSpec((tm, tn), lambda i, j, k: (i, j)))
        args.append(residual)
    scratch = [pltpu.VMEM((tm, tn), F32)] if nk > 1 else []
    return pl.pallas_call(
        functools.partial(_matmul_kernel, kind=kind, nk=nk),
        grid=(m // tm, n // tn, nk),
        in_specs=in_specs,
        out_specs=pl.BlockSpec((tm, tn), lambda i, j, k: (i, j)),
        out_shape=jax.ShapeDtypeStruct((m, n), out_dtype),
        scratch_shapes=scratch,
        compiler_params=_params("parallel", "parallel", "arbitrary"),
        name=name,
    )(*args)


def _rotate_pairs(y, half):
    width = y.shape[-1]
    if 2 * half == width:
        return pltpu.roll(y, half, 1)
    lane = lax.broadcasted_iota(jnp.int32, y.shape, 1)
    first = (lane % (2 * half)) < half
    return jnp.where(first, pltpu.roll(y, width - half, 1), pltpu.roll(y, half, 1))


def _headnorm_kernel(*refs, heads, rot_half, scale):
    if rot_half:
        x_ref, g_ref, cos_ref, sin_ref, o_ref = refs
    else:
        x_ref, g_ref, o_ref = refs
    g = g_ref[...] * scale
    for h in range(heads):
        cols = slice(h * HEAD_DIM, (h + 1) * HEAD_DIM)
        x = x_ref[:, cols].astype(F32)
        ms = jnp.mean(x * x, axis=-1, keepdims=True)
        y = x * lax.rsqrt(ms + NORM_EPS) * g
        if rot_half:
            y = y * cos_ref[...] + _rotate_pairs(y, rot_half) * sin_ref[...]
        o_ref[:, cols] = y.astype(o_ref.dtype)


def _headnorm(proj, col_off, width, gain, seq, *, rope=None, scale=1.0, tm=512, heads=4):
    t = proj.shape[0]
    tm = min(tm, seq)
    bw = heads * HEAD_DIM
    assert col_off % bw == 0 and width % bw == 0 and seq % tm == 0
    cb = col_off // bw
    in_specs = [pl.BlockSpec((tm, bw), lambda i, j: (i, cb + j)), pl.BlockSpec((1, HEAD_DIM), lambda i, j: (0, 0))]
    args = [proj, gain.reshape(1, HEAD_DIM).astype(F32)]
    rot_half = 0
    if rope is not None:
        cos, sin, rot_half = rope
        nsb = seq // tm
        tab_spec = pl.BlockSpec((tm, HEAD_DIM), lambda i, j: (i % nsb, 0))
        in_specs += [tab_spec, tab_spec]
        args += [cos, sin]
    return pl.pallas_call(
        functools.partial(_headnorm_kernel, heads=heads, rot_half=rot_half, scale=scale),
        grid=(t // tm, width // bw),
        in_specs=in_specs,
        out_specs=pl.BlockSpec((tm, bw), lambda i, j: (i, j)),
        out_shape=jax.ShapeDtypeStruct((t, width), BF16),
        compiler_params=_params("parallel", "parallel"),
        name="headnorm",
    )(*args)


def _rope_tables(seq):
    pos = jnp.arange(seq, dtype=jnp.int32)

    def angles(p, dr):
        inv = ROPE_THETA ** (-jnp.arange(0, dr, 2, dtype=F32) / dr)
        return p.astype(F32)[:, None] * inv[None, :]

    a1 = angles(pos, HEAD_DIM)
    cos1 = jnp.concatenate([jnp.cos(a1), jnp.cos(a1)], axis=-1)
    sin1 = jnp.concatenate([-jnp.sin(a1), jnp.sin(a1)], axis=-1)
    ar = angles(pos // GRID_W, HEAD_DIM // 2)
    ac = angles(pos % GRID_W, HEAD_DIM // 2)
    cos2 = jnp.concatenate([jnp.cos(ar), jnp.cos(ar), jnp.cos(ac), jnp.cos(ac)], axis=-1)
    sin2 = jnp.concatenate([-jnp.sin(ar), jnp.sin(ar), -jnp.sin(ac), jnp.sin(ac)], axis=-1)
    return (cos1, sin1, HEAD_DIM // 2), (cos2, sin2, HEAD_DIM // 4)


def _qk(q, k):
    return lax.dot_general(q, k, (((1,), (1,)), ((), ())), preferred_element_type=F32)


def _online_step(q, kj, vj, m, l, acc):
    s = _qk(q, kj)
    m_new = jnp.maximum(m, jnp.max(s, axis=-1, keepdims=True))
    alpha = jnp.exp(m - m_new)
    p = jnp.exp(s - m_new)
    l_new = alpha * l + jnp.sum(p, axis=-1, keepdims=True)
    acc_new = alpha * acc + jnp.dot(p.astype(vj.dtype), vj, preferred_element_type=F32)
    return m_new, l_new, acc_new


def _softmax_init(rows, dv):
    return (jnp.full((rows, 1), -jnp.inf, F32), jnp.zeros((rows, 1), F32), jnp.zeros((rows, dv), F32))


def _gqa_kernel(q_ref, k_ref, v_ref, o_ref, *, seq, tk):
    tq = q_ref.shape[0]
    q = jnp.concatenate([q_ref[:, g * HEAD_DIM:(g + 1) * HEAD_DIM] for g in range(GQA_GROUP)], axis=0)

    def body(j, carry):
        start = pl.multiple_of(j * tk, tk)
        return _online_step(q, k_ref[pl.ds(start, tk), :], v_ref[pl.ds(start, tk), :], *carry)

    _, l, acc = lax.fori_loop(0, seq // tk, body, _softmax_init(GQA_GROUP * tq, HEAD_DIM))
    o = acc / l
    for g in range(GQA_GROUP):
        o_ref[:, g * HEAD_DIM:(g + 1) * HEAD_DIM] = o[g * tq:(g + 1) * tq].astype(o_ref.dtype)


def _gqa_attention(qc, kc, proj, batch, seq, tq=256, tk=512):
    t = batch * seq
    tq, tk = min(tq, seq), min(tk, seq)
    nqb = seq // tq
    gw = GQA_GROUP * HEAD_DIM
    vcb = OFF_VC // HEAD_DIM
    return pl.pallas_call(
        functools.partial(_gqa_kernel, seq=seq, tk=tk),
        grid=(batch, GQA_KV_HEADS, nqb),
        in_specs=[
            pl.BlockSpec((tq, gw), lambda b, n, i: (b * nqb + i, n)),
            pl.BlockSpec((seq, HEAD_DIM), lambda b, n, i: (b, n)),
            pl.BlockSpec((seq, HEAD_DIM), lambda b, n, i: (b, vcb + n)),
        ],
        out_specs=pl.BlockSpec((tq, gw), lambda b, n, i: (b * nqb + i, n)),
        out_shape=jax.ShapeDtypeStruct((t, W_C_Q), BF16),
        compiler_params=_params("parallel", "parallel", "parallel"),
        name="gqa_attention",
    )(qc, kc, proj)


def _diff_kernel(q_ref, k_ref, v_ref, lq1_ref, lk1_ref, lq2_ref, lk2_ref, sg_ref, o_ref, *, seq, tk, lam_init):
    tq = q_ref.shape[0]
    dv = 2 * HEAD_DIM
    q1 = q_ref[:, :HEAD_DIM]
    q2 = q_ref[:, HEAD_DIM:]

    def body(j, carry):
        start = pl.multiple_of(j * tk, tk)
        kj = k_ref[pl.ds(start, tk), :]
        vj = v_ref[pl.ds(start, tk), :]
        c1 = _online_step(q1, kj[:, :HEAD_DIM], vj, *carry[:3])
        c2 = _online_step(q2, kj[:, HEAD_DIM:], vj, *carry[3:])
        return c1 + c2

    init = _softmax_init(tq, dv)
    _, l1, acc1, _, l2, acc2 = lax.fori_loop(0, seq // tk, body, init + init)
    lam = (jnp.exp(jnp.sum(lq1_ref[...] * lk1_ref[...], axis=-1, keepdims=True))
           - jnp.exp(jnp.sum(lq2_ref[...] * lk2_ref[...], axis=-1, keepdims=True)) + lam_init)
    o = acc1 / l1 - lam * (acc2 / l2)
    ms = jnp.mean(o * o, axis=-1, keepdims=True)
    o = o * lax.rsqrt(ms + NORM_EPS) * sg_ref[...] * (1.0 - lam_init)
    o_ref[...] = o.astype(o_ref.dtype)


def _diff_attention(qb, kb, proj, lam_vecs, sub_gain, lam_init, batch, seq, tq=256, tk=512):
    t = batch * seq
    tq, tk = min(tq, seq), min(tk, seq)
    nqb = seq // tq
    pw = 2 * HEAD_DIM
    vcb = OFF_VB // pw
    vec_spec = pl.BlockSpec((1, HEAD_DIM), lambda b, h, i: (0, 0))
    return pl.pallas_call(
        functools.partial(_diff_kernel, seq=seq, tk=tk, lam_init=lam_init),
        grid=(batch, DIFF_HEADS, nqb),
        in_specs=[
            pl.BlockSpec((tq, pw), lambda b, h, i: (b * nqb + i, h)),
            pl.BlockSpec((seq, pw), lambda b, h, i: (b, h)),
            pl.BlockSpec((seq, pw), lambda b, h, i: (b, vcb + h)),
            vec_spec, vec_spec, vec_spec, vec_spec,
            pl.BlockSpec((1, pw), lambda b, h, i: (0, 0)),
        ],
        out_specs=pl.BlockSpec((tq, pw), lambda b, h, i: (b * nqb + i, h)),
        out_shape=jax.ShapeDtypeStruct((t, W_B_V), BF16),
        compiler_params=_params("parallel", "parallel", "parallel"),
        name="diff_attention",
    )(qb, kb, proj, *[v.reshape(1, HEAD_DIM).astype(F32) for v in lam_vecs], sub_gain.reshape(1, pw).astype(F32))


def _na_bias_table(rpb):
    col = jnp.arange(GRID_W)
    col_start = jnp.clip(col - NA_WIN_C // 2, 0, GRID_W - NA_WIN_C)
    col_mask = (col[None, :] >= col_start[:, None]) & (col[None, :] < col_start[:, None] + NA_WIN_C)
    dc = jnp.clip(col[None, :] - col[:, None], -(NA_WIN_C - 1), NA_WIN_C - 1) + NA_WIN_C - 1
    rpb_c = rpb.astype(F32)[:, :, dc]
    kr = NA_WIN_R
    j = jnp.arange(kr)
    dr = (NA_WIN_R - 1) - j[:, None] + jnp.arange(kr)[None, :]
    tab = rpb_c[:, dr]
    tab = jnp.where(col_mask[None, None, None], tab, NEG_BIG)
    tab = tab.transpose(0, 1, 3, 2, 4)
    return tab.reshape(rpb.shape[0], kr, GRID_W, kr * GRID_W)


def _na_kernel(q_ref, k_ref, v_ref, bias_ref, o_ref, *, rows, rows_per_step):
    kr = NA_WIN_R
    win = kr * GRID_W
    rb = pl.program_id(2)
    for rr in range(rows_per_step):
        r = rb * rows_per_step + rr
        rs = jnp.clip(r - kr // 2, 0, rows - kr)
        start = pl.multiple_of(rs * GRID_W, GRID_W)
        q = q_ref[rr * GRID_W:(rr + 1) * GRID_W, :]
        kw = k_ref[pl.ds(start, win), :]
        vw = v_ref[pl.ds(start, win), :]
        s = _qk(q, kw) + bias_ref[0, r - rs]
        m = jnp.max(s, axis=-1, keepdims=True)
        p = jnp.exp(s - m)
        l = jnp.sum(p, axis=-1, keepdims=True)
        o = jnp.dot(p.astype(vw.dtype), vw, preferred_element_type=F32) / l
        o_ref[rr * GRID_W:(rr + 1) * GRID_W, :] = o.astype(o_ref.dtype)


def _na_attention(qa, ka, proj, bias_tab, batch, seq, rows_per_step=8):
    t = batch * seq
    rows = seq // GRID_W
    assert rows >= NA_WIN_R and rows % rows_per_step == 0
    nrb = rows // rows_per_step
    tq = rows_per_step * GRID_W
    vcb = OFF_VA // HEAD_DIM
    return pl.pallas_call(
        functools.partial(_na_kernel, rows=rows, rows_per_step=rows_per_step),
        grid=(batch, NA_HEADS, nrb),
        in_specs=[
            pl.BlockSpec((tq, HEAD_DIM), lambda b, h, i: (b * nrb + i, h)),
            pl.BlockSpec((seq, HEAD_DIM), lambda b, h, i: (b, h)),
            pl.BlockSpec((seq, HEAD_DIM), lambda b, h, i: (b, vcb + h)),
            pl.BlockSpec((1, NA_WIN_R, GRID_W, NA_WIN_R * GRID_W), lambda b, h, i: (h, 0, 0, 0)),
        ],
        out_specs=pl.BlockSpec((tq, HEAD_DIM), lambda b, h, i: (b * nrb + i, h)),
        out_shape=jax.ShapeDtypeStruct((t, W_A), BF16),
        compiler_params=_params("parallel", "parallel", "parallel"),
        name="na_attention",
    )(qa, ka, proj, bias_tab)


def _merge_kernel(oa_ref, ob_ref, oc_ref, wa_ref, wb_ref, wc_ref, ga_ref, gb_ref, gc_ref, o_ref):
    def branch(o_r, w_r, g_r):
        y = jnp.dot(o_r[...], w_r[...], preferred_element_type=F32)
        return jax.nn.sigmoid(g_r[...].astype(F32)) * y

    merged = branch(oa_ref, wa_ref, ga_ref) + branch(ob_ref, wb_ref, gb_ref) + branch(oc_ref, wc_ref, gc_ref)
    o_ref[...] = merged.astype(o_ref.dtype)


def _merge(o_a, o_b, o_c, w_oa, w_ob, w_oc, proj, tm=512, tn=1024):
    t = o_a.shape[0]
    d = w_oa.shape[1]
    tm, tn = min(tm, t), min(tn, d)
    assert OFF_GA % tn == 0 and d % tn == 0
    gcb = OFF_GA // tn
    ncb = d // tn

    def act(width):
        return pl.BlockSpec((tm, width), lambda j, i: (i, 0))

    def wgt(width):
        return pl.BlockSpec((width, tn), lambda j, i: (0, j))

    def gate(branch):
        return pl.BlockSpec((tm, tn), lambda j, i: (i, gcb + branch * ncb + j))

    return pl.pallas_call(
        _merge_kernel,
        grid=(ncb, t // tm),
        in_specs=[act(W_A), act(W_B_V), act(W_C_Q), wgt(W_A), wgt(W_B_V), wgt(W_C_Q), gate(0), gate(1), gate(2)],
        out_specs=pl.BlockSpec((tm, tn), lambda j, i: (i, j)),
        out_shape=jax.ShapeDtypeStruct((t, d), BF16),
        compiler_params=_params("parallel", "parallel"),
        name="gated_merge",
    )(o_a, o_b, o_c, w_oa, w_ob, w_oc, proj, proj, proj)


def _layer(x, batch, seq, lam_init, w, tables):
    rope1, rope2 = tables
    h = _rmsnorm(x, w["norm_mix"])
    proj = _matmul(h, w["w_in"], name="proj_in")

    qa = _headnorm(proj, OFF_QA, W_A, w["qn_a"], seq, scale=QK_SCALE)
    ka = _headnorm(proj, OFF_KA, W_A, w["kn_a"], seq)
    o_a = _na_attention(qa, ka, proj, w["na_bias"], batch, seq)

    qb = _headnorm(proj, OFF_QB, W_B_QK, w["qn_b"], seq, rope=rope1, scale=QK_SCALE)
    kb = _headnorm(proj, OFF_KB, W_B_QK, w["kn_b"], seq, rope=rope1)
    o_b = _diff_attention(qb, kb, proj, w["lam_vecs"], w["subln_b"], lam_init, batch, seq)

    qc = _headnorm(proj, OFF_QC, W_C_Q, w["qn_c"], seq, rope=rope2, scale=QK_SCALE)
    kc = _headnorm(proj, OFF_KC, W_C_KV, w["kn_c"], seq, rope=rope2)
    o_c = _gqa_attention(qc, kc, proj, batch, seq)

    merged = _merge(o_a, o_b, o_c, w["w_oa"], w["w_ob"], w["w_oc"], proj)
    x = _matmul(merged, w["w_out"], kind="residual", residual=x, out_dtype=F32, name="proj_out")

    h = _rmsnorm(x, w["norm_mlp"])
    u = _matmul(h, w["w_up"], kind="relu2", name="mlp_up")
    return _matmul(u, w["w_down"], kind="residual", residual=x, out_dtype=F32, name="mlp_down")


def kernel(x_prompt, x_sample, norm_mix, w_in, qn_a, kn_a, rpb, qn_b, kn_b, lam_q1, lam_k1, lam_q2, lam_k2,
           subln_b, qn_c, kn_c, w_oa, w_ob, w_oc, w_out, norm_mlp, w_up, w_down):
    depth = w_in.shape[0]
    layers = []
    for l in range(depth):
        layers.append(dict(
            norm_mix=norm_mix[l], w_in=w_in[l].astype(BF16), qn_a=qn_a[l], kn_a=kn_a[l],
            na_bias=_na_bias_table(rpb[l]), qn_b=qn_b[l], kn_b=kn_b[l],
            lam_vecs=(lam_q1[l], lam_k1[l], lam_q2[l], lam_k2[l]), subln_b=subln_b[l],
            qn_c=qn_c[l], kn_c=kn_c[l], w_oa=w_oa[l].astype(BF16), w_ob=w_ob[l].astype(BF16),
            w_oc=w_oc[l].astype(BF16), w_out=w_out[l].astype(BF16), norm_mlp=norm_mlp[l],
            w_up=w_up[l].astype(BF16), w_down=w_down[l].astype(BF16)))

    def trunk(x):
        batch, seq, d = x.shape
        tables = _rope_tables(seq)
        y = x.reshape(batch * seq, d)
        for l in range(depth):
            lam_init = 0.8 - 0.6 * math.exp(-0.3 * l)
            y = _layer(y, batch, seq, lam_init, layers[l], tables)
        return y.reshape(batch, seq, d)

    return (trunk(x_prompt), trunk(x_sample))
```

```python
import functools
import math

import jax
import jax.numpy as jnp
from jax import lax
from jax.experimental import pallas as pl
from jax.experimental.pallas import tpu as pltpu

F32 = jnp.float32
BF16 = jnp.bfloat16

HEAD_DIM = 128
GRID_W = 64
NA_HEADS = 12
NA_WIN_R = 8
NA_WIN_C = 16
DIFF_HEADS = 4
GQA_Q_HEADS = 12
GQA_KV_HEADS = 4
GQA_GROUP = GQA_Q_HEADS // GQA_KV_HEADS
ROPE_THETA = 10000.0
NORM_EPS = 1e-6

W_A = NA_HEADS * HEAD_DIM
W_B_QK = 2 * DIFF_HEADS * HEAD_DIM
W_B_V = DIFF_HEADS * 2 * HEAD_DIM
W_C_Q = GQA_Q_HEADS * HEAD_DIM
W_C_KV = GQA_KV_HEADS * HEAD_DIM

OFF_QA = 0
OFF_KA = OFF_QA + W_A
OFF_VA = OFF_KA + W_A
OFF_QB = OFF_VA + W_A
OFF_KB = OFF_QB + W_B_QK
OFF_VB = OFF_KB + W_B_QK
OFF_QC = OFF_VB + W_B_V
OFF_KC = OFF_QC + W_C_Q
OFF_VC = OFF_KC + W_C_KV
OFF_GA = OFF_VC + W_C_KV

V7X_VMEM_LIMIT_BYTES = 56 * 1024 * 1024

LOG2E = 1.4426950408889634
QK_SCALE_LOG2 = HEAD_DIM ** -0.5 * LOG2E
NEG_BIG = -1e30
PLAIN_SOFTMAX_LOG2_LIMIT = 64.0


def _params(*semantics):
    return pltpu.CompilerParams(dimension_semantics=semantics, vmem_limit_bytes=V7X_VMEM_LIMIT_BYTES)


def _rmsnorm_kernel(x_ref, g_ref, o_ref):
    x = x_ref[...]
    ms = jnp.mean(x * x, axis=-1, keepdims=True)
    o_ref[...] = (x * lax.rsqrt(ms + NORM_EPS) * g_ref[...]).astype(o_ref.dtype)


def _rmsnorm(x, gain, tm=256):
    t, d = x.shape
    tm = min(tm, t)
    return pl.pallas_call(
        _rmsnorm_kernel,
        grid=(t // tm,),
        in_specs=[pl.BlockSpec((tm, d), lambda i: (i, 0)), pl.BlockSpec((1, d), lambda i: (0, 0))],
        out_specs=pl.BlockSpec((tm, d), lambda i: (i, 0)),
        out_shape=jax.ShapeDtypeStruct((t, d), BF16),
        compiler_params=_params("parallel"),
        name="rmsnorm",
    )(x, gain.reshape(1, d).astype(F32))


def _epilogue(acc, kind, res_ref):
    if kind == "residual":
        return res_ref[...] + acc
    if kind == "relu2":
        return jnp.square(jnp.maximum(acc, 0.0))
    return acc


def _matmul_kernel(*refs, kind, nk):
    if kind == "residual":
        a_ref, b_ref, res_ref, o_ref = refs[:4]
        rest = refs[4:]
    else:
        a_ref, b_ref, o_ref = refs[:3]
        res_ref = None
        rest = refs[3:]
    part = jnp.dot(a_ref[...], b_ref[...], preferred_element_type=F32)
    if nk == 1:
        o_ref[...] = _epilogue(part, kind, res_ref).astype(o_ref.dtype)
        return
    (acc_ref,) = rest
    k = pl.program_id(2)

    @pl.when(k == 0)
    def _():
        acc_ref[...] = part

    @pl.when(jnp.logical_and(k > 0, k < nk - 1))
    def _():
        acc_ref[...] += part

    @pl.when(k == nk - 1)
    def _():
        o_ref[...] = _epilogue(acc_ref[...] + part, kind, res_ref).astype(o_ref.dtype)


def _matmul(a, w, layer, *, kind="plain", residual=None, out_dtype=BF16, tm=1024, tn=1024, tk=4096, name="matmul"):
    m, kdim = a.shape
    n = w.shape[2]
    tm, tn, tk = min(tm, m), min(tn, n), min(tk, kdim)
    nk = kdim // tk
    in_specs = [pl.BlockSpec((tm, tk), lambda i, j, k: (i, k)),
                pl.BlockSpec((None, tk, tn), lambda i, j, k: (layer, k, j))]
    args = [a, w]
    if kind == "residual":
        in_specs.append(pl.BlockSpec((tm, tn), lambda i, j, k: (i, j)))
        args.append(residual)
    scratch = [pltpu.VMEM((tm, tn), F32)] if nk > 1 else []
    return pl.pallas_call(
        functools.partial(_matmul_kernel, kind=kind, nk=nk),
        grid=(m // tm, n // tn, nk),
        in_specs=in_specs,
        out_specs=pl.BlockSpec((tm, tn), lambda i, j, k: (i, j)),
        out_shape=jax.ShapeDtypeStruct((m, n), out_dtype),
        scratch_shapes=scratch,
        compiler_params=_params("parallel", "parallel", "arbitrary"),
        name=name,
    )(*args)


def _rotate_pairs(y, half):
    width = y.shape[-1]
    if 2 * half == width:
        return pltpu.roll(y, half, 1)
    lane = lax.broadcasted_iota(jnp.int32, y.shape, 1)
    first = (lane % (2 * half)) < half
    return jnp.where(first, pltpu.roll(y, width - half, 1), pltpu.roll(y, half, 1))


def _headnorm_kernel(*refs, heads, rot_half, scale):
    if rot_half:
        x_ref, g_ref, cos_ref, sin_ref, o_ref = refs
    else:
        x_ref, g_ref, o_ref = refs
    g = g_ref[...] * scale
    for h in range(heads):
        cols = slice(h * HEAD_DIM, (h + 1) * HEAD_DIM)
        x = x_ref[:, cols].astype(F32)
        ms = jnp.mean(x * x, axis=-1, keepdims=True)
        y = x * lax.rsqrt(ms + NORM_EPS) * g
        if rot_half:
            y = y * cos_ref[...] + _rotate_pairs(y, rot_half) * sin_ref[...]
        o_ref[:, cols] = y.astype(o_ref.dtype)


def _headnorm(proj, col_off, width, gain, seq, *, rope=None, scale=1.0, tm=1024, heads=4):
    t = proj.shape[0]
    tm = min(tm, seq)
    bw = heads * HEAD_DIM
    assert col_off % bw == 0 and width % bw == 0 and seq % tm == 0
    cb = col_off // bw
    in_specs = [pl.BlockSpec((tm, bw), lambda i, j: (i, cb + j)), pl.BlockSpec((1, HEAD_DIM), lambda i, j: (0, 0))]
    args = [proj, gain.reshape(1, HEAD_DIM).astype(F32)]
    rot_half = 0
    if rope is not None:
        cos, sin, rot_half = rope
        nsb = seq // tm
        tab_spec = pl.BlockSpec((tm, HEAD_DIM), lambda i, j: (i % nsb, 0))
        in_specs += [tab_spec, tab_spec]
        args += [cos, sin]
    return pl.pallas_call(
        functools.partial(_headnorm_kernel, heads=heads, rot_half=rot_half, scale=scale),
        grid=(t // tm, width // bw),
        in_specs=in_specs,
        out_specs=pl.BlockSpec((tm, bw), lambda i, j: (i, j)),
        out_shape=jax.ShapeDtypeStruct((t, width), BF16),
        compiler_params=_params("parallel", "parallel"),
        name="headnorm",
    )(*args)


def _rope_tables(seq):
    pos = jnp.arange(seq, dtype=jnp.int32)

    def angles(p, dr):
        inv = ROPE_THETA ** (-jnp.arange(0, dr, 2, dtype=F32) / dr)
        return p.astype(F32)[:, None] * inv[None, :]

    a1 = angles(pos, HEAD_DIM)
    cos1 = jnp.concatenate([jnp.cos(a1), jnp.cos(a1)], axis=-1)
    sin1 = jnp.concatenate([-jnp.sin(a1), jnp.sin(a1)], axis=-1)
    ar = angles(pos // GRID_W, HEAD_DIM // 2)
    ac = angles(pos % GRID_W, HEAD_DIM // 2)
    cos2 = jnp.concatenate([jnp.cos(ar), jnp.cos(ar), jnp.cos(ac), jnp.cos(ac)], axis=-1)
    sin2 = jnp.concatenate([-jnp.sin(ar), jnp.sin(ar), -jnp.sin(ac), jnp.sin(ac)], axis=-1)
    return (cos1, sin1, HEAD_DIM // 2), (cos2, sin2, HEAD_DIM // 4)


def _qk(q, k):
    return lax.dot_general(q, k, (((1,), (1,)), ((), ())), preferred_element_type=F32)


def _score_bound_ok(q_gain, k_gain, bias=None):
    bound = HEAD_DIM * QK_SCALE_LOG2 * jnp.max(jnp.abs(q_gain)) * jnp.max(jnp.abs(k_gain))
    if bias is not None:
        bound = bound + LOG2E * jnp.max(jnp.abs(bias))
    return bound <= PLAIN_SOFTMAX_LOG2_LIMIT


def _lane_fold(p):
    out = p[:, :HEAD_DIM]
    for c in range(1, p.shape[1] // HEAD_DIM):
        out = out + p[:, c * HEAD_DIM:(c + 1) * HEAD_DIM]
    return out


def _online_step(s, vj, m, l, acc):
    m_new = jnp.maximum(m, jnp.max(s, axis=-1, keepdims=True))
    alpha = jnp.exp2(m - m_new)
    p = jnp.exp2(s - m_new)
    l_new = alpha * l + jnp.sum(p, axis=-1, keepdims=True)
    acc_new = alpha * acc + jnp.dot(p.astype(vj.dtype), vj, preferred_element_type=F32)
    return m_new, l_new, acc_new


def _attend(score_fn, k_ref, v_ref, rows, dv, seq, tk, online, scratch):
    def chunk(j):
        start = pl.multiple_of(j * tk, tk)
        return k_ref[pl.ds(start, tk), :], v_ref[pl.ds(start, tk), :]

    if online:
        def body(j, carry):
            kj, vj = chunk(j)
            return _online_step(score_fn(kj), vj, *carry)

        init = (jnp.full((rows, 1), -jnp.inf, F32), jnp.zeros((rows, 1), F32), jnp.zeros((rows, dv), F32))
        _, l, acc = lax.fori_loop(0, seq // tk, body, init)
        return acc / l

    acc_ref, l_ref = scratch
    acc_ref[...] = jnp.zeros_like(acc_ref)
    l_ref[...] = jnp.zeros_like(l_ref)

    def body(j, carry):
        kj, vj = chunk(j)
        p = jnp.exp2(score_fn(kj))
        l_ref[...] += _lane_fold(p)
        acc_ref[...] += jnp.dot(p.astype(vj.dtype), vj, preferred_element_type=F32)
        return carry

    lax.fori_loop(0, seq // tk, body, 0)
    return acc_ref[...] / jnp.sum(l_ref[...], axis=-1, keepdims=True)


def _attend_scratch(rows, dv, online):
    return [] if online else [pltpu.VMEM((rows, dv), F32), pltpu.VMEM((rows, HEAD_DIM), F32)]


def _gqa_kernel(q_ref, k_ref, v_ref, o_ref, *scratch, seq, tk, online):
    tq = q_ref.shape[0]
    q = jnp.concatenate([q_ref[:, g * HEAD_DIM:(g + 1) * HEAD_DIM] for g in range(GQA_GROUP)], axis=0)
    o = _attend(lambda kj: _qk(q, kj), k_ref, v_ref, GQA_GROUP * tq, HEAD_DIM, seq, tk, online, scratch)
    for g in range(GQA_GROUP):
        o_ref[:, g * HEAD_DIM:(g + 1) * HEAD_DIM] = o[g * tq:(g + 1) * tq].astype(o_ref.dtype)


def _gqa_attention(qc, kc, proj, *, batch, seq, online, tq=256, tk=1024):
    t = batch * seq
    tq, tk = min(tq, seq), min(tk, seq)
    nqb = seq // tq
    gw = GQA_GROUP * HEAD_DIM
    vcb = OFF_VC // HEAD_DIM
    return pl.pallas_call(
        functools.partial(_gqa_kernel, seq=seq, tk=tk, online=online),
        grid=(batch, GQA_KV_HEADS, nqb),
        in_specs=[
            pl.BlockSpec((tq, gw), lambda b, n, i: (b * nqb + i, n)),
            pl.BlockSpec((seq, HEAD_DIM), lambda b, n, i: (b, n)),
            pl.BlockSpec((seq, HEAD_DIM), lambda b, n, i: (b, vcb + n)),
        ],
        out_specs=pl.BlockSpec((tq, gw), lambda b, n, i: (b * nqb + i, n)),
        out_shape=jax.ShapeDtypeStruct((t, W_C_Q), BF16),
        scratch_shapes=_attend_scratch(GQA_GROUP * tq, HEAD_DIM, online),
        compiler_params=_params("parallel", "parallel", "parallel"),
        name="gqa_attention",
    )(qc, kc, proj)


def _diff_kernel(q_ref, k_ref, v_ref, lq1_ref, lk1_ref, lq2_ref, lk2_ref, sg_ref, o_ref, *scratch,
                 seq, tk, lam_init, online):
    tq = q_ref.shape[0]
    dv = 2 * HEAD_DIM
    q1 = q_ref[:, :HEAD_DIM]
    q2 = q_ref[:, HEAD_DIM:]

    def scores(kj):
        return jnp.concatenate([_qk(q1, kj[:, :HEAD_DIM]), _qk(q2, kj[:, HEAD_DIM:])], axis=0)

    o12 = _attend(scores, k_ref, v_ref, 2 * tq, dv, seq, tk, online, scratch)
    lam = (jnp.exp(jnp.sum(lq1_ref[...] * lk1_ref[...], axis=-1, keepdims=True))
           - jnp.exp(jnp.sum(lq2_ref[...] * lk2_ref[...], axis=-1, keepdims=True)) + lam_init)
    o = o12[:tq] - lam * o12[tq:]
    ms = jnp.mean(o * o, axis=-1, keepdims=True)
    o = o * lax.rsqrt(ms + NORM_EPS) * sg_ref[...] * (1.0 - lam_init)
    o_ref[...] = o.astype(o_ref.dtype)


def _diff_attention(qb, kb, proj, lam_vecs, sub_gain, *, lam_init, batch, seq, online, tq=512, tk=1024):
    t = batch * seq
    tq, tk = min(tq, seq), min(tk, seq)
    nqb = seq // tq
    pw = 2 * HEAD_DIM
    vcb = OFF_VB // pw
    vec_spec = pl.BlockSpec((1, HEAD_DIM), lambda b, h, i: (0, 0))
    return pl.pallas_call(
        functools.partial(_diff_kernel, seq=seq, tk=tk, lam_init=lam_init, online=online),
        grid=(batch, DIFF_HEADS, nqb),
        in_specs=[
            pl.BlockSpec((tq, pw), lambda b, h, i: (b * nqb + i, h)),
            pl.BlockSpec((seq, pw), lambda b, h, i: (b, h)),
            pl.BlockSpec((seq, pw), lambda b, h, i: (b, vcb + h)),
            vec_spec, vec_spec, vec_spec, vec_spec,
            pl.BlockSpec((1, pw), lambda b, h, i: (0, 0)),
        ],
        out_specs=pl.BlockSpec((tq, pw), lambda b, h, i: (b * nqb + i, h)),
        out_shape=jax.ShapeDtypeStruct((t, W_B_V), BF16),
        scratch_shapes=_attend_scratch(2 * tq, pw, online),
        compiler_params=_params("parallel", "parallel", "parallel"),
        name="diff_attention",
    )(qb, kb, proj, *lam_vecs, sub_gain)


NA_Q_ROWS = 4
NA_K_ROWS = NA_Q_ROWS + NA_WIN_R
NA_BLOCK_TYPES = 3


def _na_bias_table(rpb):
    col = jnp.arange(GRID_W)
    col_start = jnp.clip(col - NA_WIN_C // 2, 0, GRID_W - NA_WIN_C)
    col_mask = (col[None, :] >= col_start[:, None]) & (col[None, :] < col_start[:, None] + NA_WIN_C)
    dc = jnp.clip(col[None, :] - col[:, None], -(NA_WIN_C - 1), NA_WIN_C - 1) + NA_WIN_C - 1
    rpb_c = rpb.astype(F32)[:, :, dc] * LOG2E
    rr = jnp.arange(NA_Q_ROWS)
    q_off = jnp.stack([rr, rr + NA_WIN_R // 2, rr + NA_K_ROWS - NA_Q_ROWS])
    w_off = jnp.stack([0 * rr, rr, 0 * rr + NA_K_ROWS - NA_WIN_R])
    i = jnp.arange(NA_K_ROWS)
    dr = i[None, None, :] - q_off[:, :, None] + NA_WIN_R - 1
    row_ok = (i[None, None, :] >= w_off[:, :, None]) & (i[None, None, :] < w_off[:, :, None] + NA_WIN_R)
    tab = rpb_c[:, jnp.clip(dr, 0, 2 * NA_WIN_R - 2)]
    ok = row_ok[None, :, :, :, None, None] & col_mask[None, None, None, None]
    tab = jnp.where(ok, tab, NEG_BIG).transpose(0, 1, 2, 4, 3, 5)
    return tab.reshape(rpb.shape[0], NA_BLOCK_TYPES, NA_Q_ROWS * GRID_W, NA_K_ROWS * GRID_W)


def _na_kernel(q_ref, k_ref, v_ref, bias_ref, o_ref, *, rows, online):
    rb = pl.program_id(2)
    ws = jnp.clip(rb * NA_Q_ROWS - NA_WIN_R // 2, 0, rows - NA_K_ROWS)
    start = pl.multiple_of(ws * GRID_W, GRID_W)
    kw = k_ref[pl.ds(start, NA_K_ROWS * GRID_W), :]
    vw = v_ref[pl.ds(start, NA_K_ROWS * GRID_W), :]
    s = _qk(q_ref[...], kw) + bias_ref[0, 0]
    if online:
        s = s - jnp.max(s, axis=-1, keepdims=True)
    p = jnp.exp2(s)
    l = jnp.sum(p, axis=-1, keepdims=True)
    o = jnp.dot(p.astype(vw.dtype), vw, preferred_element_type=F32) / l
    o_ref[...] = o.astype(o_ref.dtype)


def _na_attention(qa, ka, proj, bias_tab, *, batch, seq, online):
    t = batch * seq
    rows = seq // GRID_W
    assert rows >= NA_K_ROWS and rows % NA_Q_ROWS == 0
    nrb = rows // NA_Q_ROWS
    tq = NA_Q_ROWS * GRID_W
    vcb = OFF_VA // HEAD_DIM

    def block_type(i):
        return jnp.where(i == 0, 0, jnp.where(i == nrb - 1, 2, 1))

    return pl.pallas_call(
        functools.partial(_na_kernel, rows=rows, online=online),
        grid=(batch, NA_HEADS, nrb),
        in_specs=[
            pl.BlockSpec((tq, HEAD_DIM), lambda b, h, i: (b * nrb + i, h)),
            pl.BlockSpec((seq, HEAD_DIM), lambda b, h, i: (b, h)),
            pl.BlockSpec((seq, HEAD_DIM), lambda b, h, i: (b, vcb + h)),
            pl.BlockSpec((1, 1, tq, NA_K_ROWS * GRID_W), lambda b, h, i: (h, block_type(i), 0, 0)),
        ],
        out_specs=pl.BlockSpec((tq, HEAD_DIM), lambda b, h, i: (b * nrb + i, h)),
        out_shape=jax.ShapeDtypeStruct((t, W_A), BF16),
        compiler_params=_params("parallel", "parallel", "parallel"),
        name="na_attention",
    )(qa, ka, proj, bias_tab)


def _pick_softmax(plain_ok, attention, *operands):
    return lax.cond(plain_ok, functools.partial(attention, online=False), functools.partial(attention, online=True),
                    *operands)


def _merge_kernel(oa_ref, ob_ref, oc_ref, wa_ref, wb_ref, wc_ref, ga_ref, gb_ref, gc_ref, o_ref):
    def branch(o_r, w_r, g_r):
        y = jnp.dot(o_r[...], w_r[...], preferred_element_type=F32)
        return jax.nn.sigmoid(g_r[...].astype(F32)) * y

    merged = branch(oa_ref, wa_ref, ga_ref) + branch(ob_ref, wb_ref, gb_ref) + branch(oc_ref, wc_ref, gc_ref)
    o_ref[...] = merged.astype(o_ref.dtype)


def _merge(o_a, o_b, o_c, w_oa, w_ob, w_oc, layer, proj, tm=512, tn=1024):
    t = o_a.shape[0]
    d = w_oa.shape[2]
    tm, tn = min(tm, t), min(tn, d)
    assert OFF_GA % tn == 0 and d % tn == 0
    gcb = OFF_GA // tn
    ncb = d // tn

    def act(width):
        return pl.BlockSpec((tm, width), lambda j, i: (i, 0))

    def wgt(width):
        return pl.BlockSpec((None, width, tn), lambda j, i: (layer, 0, j))

    def gate(branch):
        return pl.BlockSpec((tm, tn), lambda j, i: (i, gcb + branch * ncb + j))

    return pl.pallas_call(
        _merge_kernel,
        grid=(ncb, t // tm),
        in_specs=[act(W_A), act(W_B_V), act(W_C_Q), wgt(W_A), wgt(W_B_V), wgt(W_C_Q), gate(0), gate(1), gate(2)],
        out_specs=pl.BlockSpec((tm, tn), lambda j, i: (i, j)),
        out_shape=jax.ShapeDtypeStruct((t, d), BF16),
        compiler_params=_params("parallel", "parallel"),
        name="gated_merge",
    )(o_a, o_b, o_c, w_oa, w_ob, w_oc, proj, proj, proj)


def _row(v):
    return v.reshape(1, -1).astype(F32)


def _layer(x, l, batch, seq, w, tables):
    rope1, rope2 = tables
    lam_init = 0.8 - 0.6 * math.exp(-0.3 * l)
    dims = dict(batch=batch, seq=seq)
    h = _rmsnorm(x, w["norm_mix"][l])
    proj = _matmul(h, w["w_in"], l, name="proj_in")

    qa = _headnorm(proj, OFF_QA, W_A, w["qn_a"][l], seq, scale=QK_SCALE_LOG2)
    ka = _headnorm(proj, OFF_KA, W_A, w["kn_a"][l], seq)
    o_a = _pick_softmax(_score_bound_ok(w["qn_a"][l], w["kn_a"][l], w["rpb"][l]),
                        functools.partial(_na_attention, **dims), qa, ka, proj, _na_bias_table(w["rpb"][l]))

    qb = _headnorm(proj, OFF_QB, W_B_QK, w["qn_b"][l], seq, rope=rope1, scale=QK_SCALE_LOG2)
    kb = _headnorm(proj, OFF_KB, W_B_QK, w["kn_b"][l], seq, rope=rope1)
    lam_vecs = tuple(_row(w[n][l]) for n in ("lam_q1", "lam_k1", "lam_q2", "lam_k2"))
    o_b = _pick_softmax(_score_bound_ok(w["qn_b"][l], w["kn_b"][l]),
                        functools.partial(_diff_attention, lam_init=lam_init, **dims),
                        qb, kb, proj, lam_vecs, _row(w["subln_b"][l]))

    qc = _headnorm(proj, OFF_QC, W_C_Q, w["qn_c"][l], seq, rope=rope2, scale=QK_SCALE_LOG2)
    kc = _headnorm(proj, OFF_KC, W_C_KV, w["kn_c"][l], seq, rope=rope2)
    o_c = _pick_softmax(_score_bound_ok(w["qn_c"][l], w["kn_c"][l]),
                        functools.partial(_gqa_attention, **dims), qc, kc, proj)

    merged = _merge(o_a, o_b, o_c, w["w_oa"], w["w_ob"], w["w_oc"], l, proj)
    x = _matmul(merged, w["w_out"], l, kind="residual", residual=x, out_dtype=F32, tn=512, name="proj_out")

    h = _rmsnorm(x, w["norm_mlp"][l])
    u = _matmul(h, w["w_up"], l, kind="relu2", name="mlp_up")
    return _matmul(u, w["w_down"], l, kind="residual", residual=x, out_dtype=F32, tk=2048, name="mlp_down")


_MATMUL_WEIGHTS = ("w_in", "w_oa", "w_ob", "w_oc", "w_out", "w_up", "w_down")


def kernel(x_prompt, x_sample, norm_mix, w_in, qn_a, kn_a, rpb, qn_b, kn_b, lam_q1, lam_k1, lam_q2, lam_k2,
           subln_b, qn_c, kn_c, w_oa, w_ob, w_oc, w_out, norm_mlp, w_up, w_down):
    w = dict(norm_mix=norm_mix, w_in=w_in, qn_a=qn_a, kn_a=kn_a, rpb=rpb, qn_b=qn_b, kn_b=kn_b, lam_q1=lam_q1,
             lam_k1=lam_k1, lam_q2=lam_q2, lam_k2=lam_k2, subln_b=subln_b, qn_c=qn_c, kn_c=kn_c, w_oa=w_oa,
             w_ob=w_ob, w_oc=w_oc, w_out=w_out, norm_mlp=norm_mlp, w_up=w_up, w_down=w_down)
    for name in _MATMUL_WEIGHTS:
        w[name] = w[name].astype(BF16)
    depth = w_in.shape[0]

    def trunk(x):
        batch, seq, d = x.shape
        tables = _rope_tables(seq)
        y = x.reshape(batch * seq, d)
        for l in range(depth):
            y = _layer(y, l, batch, seq, w, tables)
        return y.reshape(batch, seq, d)

    return (trunk(x_prompt), trunk(x_sample))
```

```python
import functools
import math

import jax
import jax.numpy as jnp
from jax import lax
from jax.experimental import pallas as pl
from jax.experimental.pallas import tpu as pltpu

F32 = jnp.float32
BF16 = jnp.bfloat16

HEAD_DIM = 128
GRID_W = 64
NA_HEADS = 12
NA_WIN_R = 8
NA_WIN_C = 16
DIFF_HEADS = 4
GQA_Q_HEADS = 12
GQA_KV_HEADS = 4
GQA_GROUP = GQA_Q_HEADS // GQA_KV_HEADS
ROPE_THETA = 10000.0
NORM_EPS = 1e-6

W_A = NA_HEADS * HEAD_DIM
W_B_QK = 2 * DIFF_HEADS * HEAD_DIM
W_B_V = DIFF_HEADS * 2 * HEAD_DIM
W_C_Q = GQA_Q_HEADS * HEAD_DIM
W_C_KV = GQA_KV_HEADS * HEAD_DIM

OFF_QA = 0
OFF_KA = OFF_QA + W_A
OFF_VA = OFF_KA + W_A
OFF_QB = OFF_VA + W_A
OFF_KB = OFF_QB + W_B_QK
OFF_VB = OFF_KB + W_B_QK
OFF_QC = OFF_VB + W_B_V
OFF_KC = OFF_QC + W_C_Q
OFF_VC = OFF_KC + W_C_KV
OFF_GA = OFF_VC + W_C_KV

V7X_VMEM_LIMIT_BYTES = 60 * 1024 * 1024

LOG2E = 1.4426950408889634
QK_SCALE_LOG2 = HEAD_DIM ** -0.5 * LOG2E
NEG_BIG = -1e30
PLAIN_SOFTMAX_LOG2_LIMIT = 64.0


def _params(*semantics):
    return pltpu.CompilerParams(dimension_semantics=semantics, vmem_limit_bytes=V7X_VMEM_LIMIT_BYTES)


def _rmsnorm_kernel(x_ref, g_ref, o_ref):
    x = x_ref[...]
    ms = jnp.mean(x * x, axis=-1, keepdims=True)
    o_ref[...] = (x * lax.rsqrt(ms + NORM_EPS) * g_ref[...]).astype(o_ref.dtype)


def _rmsnorm(x, gain, tm=256):
    t, d = x.shape
    tm = min(tm, t)
    return pl.pallas_call(
        _rmsnorm_kernel,
        grid=(t // tm,),
        in_specs=[pl.BlockSpec((tm, d), lambda i: (i, 0)), pl.BlockSpec((1, d), lambda i: (0, 0))],
        out_specs=pl.BlockSpec((tm, d), lambda i: (i, 0)),
        out_shape=jax.ShapeDtypeStruct((t, d), BF16),
        compiler_params=_params("parallel"),
        name="rmsnorm",
    )(x, gain.reshape(1, d).astype(F32))


def _epilogue(acc, kind, res_ref):
    if kind == "residual":
        return res_ref[...] + acc
    if kind == "relu2":
        return jnp.square(jnp.maximum(acc, 0.0))
    return acc


def _matmul_kernel(*refs, kind, nk):
    if kind == "residual":
        a_ref, b_ref, res_ref, o_ref = refs[:4]
        rest = refs[4:]
    else:
        a_ref, b_ref, o_ref = refs[:3]
        res_ref = None
        rest = refs[3:]
    part = jnp.dot(a_ref[...], b_ref[...], preferred_element_type=F32)
    if nk == 1:
        o_ref[...] = _epilogue(part, kind, res_ref).astype(o_ref.dtype)
        return
    k = pl.program_id(2)
    if kind == "residual" and o_ref.dtype == F32:
        @pl.when(k == 0)
        def _():
            o_ref[...] = res_ref[...] + part

        @pl.when(k > 0)
        def _():
            o_ref[...] += part

        return
    (acc_ref,) = rest

    @pl.when(k == 0)
    def _():
        acc_ref[...] = part

    @pl.when(jnp.logical_and(k > 0, k < nk - 1))
    def _():
        acc_ref[...] += part

    @pl.when(k == nk - 1)
    def _():
        o_ref[...] = _epilogue(acc_ref[...] + part, kind, res_ref).astype(o_ref.dtype)


def _matmul(a, w, layer, *, kind="plain", residual=None, out_dtype=BF16, tm=1024, tn=1024, tk=4096, name="matmul"):
    m, kdim = a.shape
    n = w.shape[2]
    tm, tn, tk = min(tm, m), min(tn, n), min(tk, kdim)
    nk = kdim // tk
    in_specs = [pl.BlockSpec((tm, tk), lambda i, j, k: (i, k)),
                pl.BlockSpec((None, tk, tn), lambda i, j, k: (layer, k, j))]
    args = [a, w]
    if kind == "residual":
        in_specs.append(pl.BlockSpec((tm, tn), lambda i, j, k: (i, j)))
        args.append(residual)
    out_is_acc = kind == "residual" and out_dtype == F32
    scratch = [pltpu.VMEM((tm, tn), F32)] if nk > 1 and not out_is_acc else []
    return pl.pallas_call(
        functools.partial(_matmul_kernel, kind=kind, nk=nk),
        grid=(m // tm, n // tn, nk),
        in_specs=in_specs,
        out_specs=pl.BlockSpec((tm, tn), lambda i, j, k: (i, j)),
        out_shape=jax.ShapeDtypeStruct((m, n), out_dtype),
        scratch_shapes=scratch,
        compiler_params=_params("parallel", "parallel", "arbitrary"),
        name=name,
    )(*args)


def _rotate_pairs(y, half):
    width = y.shape[-1]
    if 2 * half == width:
        return pltpu.roll(y, half, 1)
    lane = lax.broadcasted_iota(jnp.int32, y.shape, 1)
    first = (lane % (2 * half)) < half
    return jnp.where(first, pltpu.roll(y, width - half, 1), pltpu.roll(y, half, 1))


def _headnorm_kernel(*refs, heads, rot_half, scale):
    if rot_half:
        x_ref, g_ref, cos_ref, sin_ref, o_ref = refs
    else:
        x_ref, g_ref, o_ref = refs
    g = g_ref[...] * scale
    for h in range(heads):
        cols = slice(h * HEAD_DIM, (h + 1) * HEAD_DIM)
        x = x_ref[:, cols].astype(F32)
        ms = jnp.mean(x * x, axis=-1, keepdims=True)
        y = x * lax.rsqrt(ms + NORM_EPS) * g
        if rot_half:
            y = y * cos_ref[...] + _rotate_pairs(y, rot_half) * sin_ref[...]
        o_ref[:, cols] = y.astype(o_ref.dtype)


def _headnorm(proj, col_off, width, gain, seq, *, rope=None, scale=1.0, tm=1024):
    t = proj.shape[0]
    tm = min(tm, seq)
    heads = max(n for n in range(1, width // HEAD_DIM + 1)
                if width % (n * HEAD_DIM) == 0 and col_off % (n * HEAD_DIM) == 0)
    bw = heads * HEAD_DIM
    assert seq % tm == 0
    cb = col_off // bw
    in_specs = [pl.BlockSpec((tm, bw), lambda i, j: (i, cb + j)), pl.BlockSpec((1, HEAD_DIM), lambda i, j: (0, 0))]
    args = [proj, gain.reshape(1, HEAD_DIM).astype(F32)]
    rot_half = 0
    if rope is not None:
        cos, sin, rot_half = rope
        nsb = seq // tm
        tab_spec = pl.BlockSpec((tm, HEAD_DIM), lambda i, j: (i % nsb, 0))
        in_specs += [tab_spec, tab_spec]
        args += [cos, sin]
    return pl.pallas_call(
        functools.partial(_headnorm_kernel, heads=heads, rot_half=rot_half, scale=scale),
        grid=(t // tm, width // bw),
        in_specs=in_specs,
        out_specs=pl.BlockSpec((tm, bw), lambda i, j: (i, j)),
        out_shape=jax.ShapeDtypeStruct((t, width), BF16),
        compiler_params=_params("parallel", "parallel"),
        name="headnorm",
    )(*args)


def _rope_tables(seq):
    pos = jnp.arange(seq, dtype=jnp.int32)

    def angles(p, dr):
        inv = ROPE_THETA ** (-jnp.arange(0, dr, 2, dtype=F32) / dr)
        return p.astype(F32)[:, None] * inv[None, :]

    a1 = angles(pos, HEAD_DIM)
    cos1 = jnp.concatenate([jnp.cos(a1), jnp.cos(a1)], axis=-1)
    sin1 = jnp.concatenate([-jnp.sin(a1), jnp.sin(a1)], axis=-1)
    ar = angles(pos // GRID_W, HEAD_DIM // 2)
    ac = angles(pos % GRID_W, HEAD_DIM // 2)
    cos2 = jnp.concatenate([jnp.cos(ar), jnp.cos(ar), jnp.cos(ac), jnp.cos(ac)], axis=-1)
    sin2 = jnp.concatenate([-jnp.sin(ar), jnp.sin(ar), -jnp.sin(ac), jnp.sin(ac)], axis=-1)
    return (cos1, sin1, HEAD_DIM // 2), (cos2, sin2, HEAD_DIM // 4)


def _qk(q, k):
    return lax.dot_general(q, k, (((1,), (1,)), ((), ())), preferred_element_type=F32)


def _score_bound_ok(q_gain, k_gain, bias=None):
    bound = HEAD_DIM * QK_SCALE_LOG2 * jnp.max(jnp.abs(q_gain)) * jnp.max(jnp.abs(k_gain))
    if bias is not None:
        bound = bound + LOG2E * jnp.max(jnp.abs(bias))
    return bound <= PLAIN_SOFTMAX_LOG2_LIMIT


def _lane_fold(p):
    out = p[:, :HEAD_DIM]
    for c in range(1, p.shape[1] // HEAD_DIM):
        out = out + p[:, c * HEAD_DIM:(c + 1) * HEAD_DIM]
    return out


def _online_step(s, vj, m, l, acc):
    m_new = jnp.maximum(m, jnp.max(s, axis=-1, keepdims=True))
    alpha = jnp.exp2(m - m_new)
    p = jnp.exp2(s - m_new)
    l_new = alpha * l + jnp.sum(p, axis=-1, keepdims=True)
    acc_new = alpha * acc + jnp.dot(p.astype(vj.dtype), vj, preferred_element_type=F32)
    return m_new, l_new, acc_new


def _attend(score_fn, k_ref, v_ref, rows, dv, seq, tk, online, scratch):
    def chunk(j):
        start = pl.multiple_of(j * tk, tk)
        return k_ref[pl.ds(start, tk), :], v_ref[pl.ds(start, tk), :]

    if online:
        def body(j, carry):
            kj, vj = chunk(j)
            return _online_step(score_fn(kj), vj, *carry)

        init = (jnp.full((rows, 1), -jnp.inf, F32), jnp.zeros((rows, 1), F32), jnp.zeros((rows, dv), F32))
        _, l, acc = lax.fori_loop(0, seq // tk, body, init)
        return acc / l

    acc_ref, l_ref = scratch
    acc_ref[...] = jnp.zeros_like(acc_ref)
    l_ref[...] = jnp.zeros_like(l_ref)

    def body(j, carry):
        kj, vj = chunk(j)
        p = jnp.exp2(score_fn(kj))
        l_ref[...] += _lane_fold(p)
        acc_ref[...] += jnp.dot(p.astype(vj.dtype), vj, preferred_element_type=F32)
        return carry

    lax.fori_loop(0, seq // tk, body, 0)
    return acc_ref[...] / jnp.sum(l_ref[...], axis=-1, keepdims=True)


def _attend_scratch(rows, dv, online):
    return [] if online else [pltpu.VMEM((rows, dv), F32), pltpu.VMEM((rows, HEAD_DIM), F32)]


def _gqa_kernel(q_ref, k_ref, v_ref, o_ref, *scratch, seq, tk, online):
    tq = q_ref.shape[0]
    q = jnp.concatenate([q_ref[:, g * HEAD_DIM:(g + 1) * HEAD_DIM] for g in range(GQA_GROUP)], axis=0)
    o = _attend(lambda kj: _qk(q, kj), k_ref, v_ref, GQA_GROUP * tq, HEAD_DIM, seq, tk, online, scratch)
    for g in range(GQA_GROUP):
        o_ref[:, g * HEAD_DIM:(g + 1) * HEAD_DIM] = o[g * tq:(g + 1) * tq].astype(o_ref.dtype)


def _gqa_attention(qc, kc, proj, *, batch, seq, online, tq=256, tk=1024):
    t = batch * seq
    tq, tk = min(tq, seq), min(tk, seq)
    nqb = seq // tq
    gw = GQA_GROUP * HEAD_DIM
    vcb = OFF_VC // HEAD_DIM
    return pl.pallas_call(
        functools.partial(_gqa_kernel, seq=seq, tk=tk, online=online),
        grid=(batch, GQA_KV_HEADS, nqb),
        in_specs=[
            pl.BlockSpec((tq, gw), lambda b, n, i: (b * nqb + i, n)),
            pl.BlockSpec((seq, HEAD_DIM), lambda b, n, i: (b, n)),
            pl.BlockSpec((seq, HEAD_DIM), lambda b, n, i: (b, vcb + n)),
        ],
        out_specs=pl.BlockSpec((tq, gw), lambda b, n, i: (b * nqb + i, n)),
        out_shape=jax.ShapeDtypeStruct((t, W_C_Q), BF16),
        scratch_shapes=_attend_scratch(GQA_GROUP * tq, HEAD_DIM, online),
        compiler_params=_params("parallel", "parallel", "parallel"),
        name="gqa_attention",
    )(qc, kc, proj)


def _diff_kernel(q_ref, k_ref, v_ref, lq1_ref, lk1_ref, lq2_ref, lk2_ref, sg_ref, o_ref, *scratch,
                 seq, tk, lam_init, online):
    tq = q_ref.shape[0]
    dv = 2 * HEAD_DIM
    q1 = q_ref[:, :HEAD_DIM]
    q2 = q_ref[:, HEAD_DIM:]

    def scores(kj):
        return jnp.concatenate([_qk(q1, kj[:, :HEAD_DIM]), _qk(q2, kj[:, HEAD_DIM:])], axis=0)

    o12 = _attend(scores, k_ref, v_ref, 2 * tq, dv, seq, tk, online, scratch)
    lam = (jnp.exp(jnp.sum(lq1_ref[...] * lk1_ref[...], axis=-1, keepdims=True))
           - jnp.exp(jnp.sum(lq2_ref[...] * lk2_ref[...], axis=-1, keepdims=True)) + lam_init)
    o = o12[:tq] - lam * o12[tq:]
    ms = jnp.mean(o * o, axis=-1, keepdims=True)
    o = o * lax.rsqrt(ms + NORM_EPS) * sg_ref[...] * (1.0 - lam_init)
    o_ref[...] = o.astype(o_ref.dtype)


def _diff_attention(qb, kb, proj, lam_vecs, sub_gain, *, lam_init, batch, seq, online, tq=512, tk=1024):
    t = batch * seq
    tq, tk = min(tq, seq), min(tk, seq)
    nqb = seq // tq
    pw = 2 * HEAD_DIM
    vcb = OFF_VB // pw
    vec_spec = pl.BlockSpec((1, HEAD_DIM), lambda b, h, i: (0, 0))
    return pl.pallas_call(
        functools.partial(_diff_kernel, seq=seq, tk=tk, lam_init=lam_init, online=online),
        grid=(batch, DIFF_HEADS, nqb),
        in_specs=[
            pl.BlockSpec((tq, pw), lambda b, h, i: (b * nqb + i, h)),
            pl.BlockSpec((seq, pw), lambda b, h, i: (b, h)),
            pl.BlockSpec((seq, pw), lambda b, h, i: (b, vcb + h)),
            vec_spec, vec_spec, vec_spec, vec_spec,
            pl.BlockSpec((1, pw), lambda b, h, i: (0, 0)),
        ],
        out_specs=pl.BlockSpec((tq, pw), lambda b, h, i: (b * nqb + i, h)),
        out_shape=jax.ShapeDtypeStruct((t, W_B_V), BF16),
        scratch_shapes=_attend_scratch(2 * tq, pw, online),
        compiler_params=_params("parallel", "parallel", "parallel"),
        name="diff_attention",
    )(qb, kb, proj, *lam_vecs, sub_gain)


NA_Q_ROWS = 4
NA_K_ROWS = NA_Q_ROWS + NA_WIN_R
NA_BLOCK_TYPES = 3


def _na_bias_table(rpb):
    col = jnp.arange(GRID_W)
    col_start = jnp.clip(col - NA_WIN_C // 2, 0, GRID_W - NA_WIN_C)
    col_mask = (col[None, :] >= col_start[:, None]) & (col[None, :] < col_start[:, None] + NA_WIN_C)
    dc = jnp.clip(col[None, :] - col[:, None], -(NA_WIN_C - 1), NA_WIN_C - 1) + NA_WIN_C - 1
    rpb_c = rpb.astype(F32)[:, :, dc] * LOG2E
    rr = jnp.arange(NA_Q_ROWS)
    q_off = jnp.stack([rr, rr + NA_WIN_R // 2, rr + NA_K_ROWS - NA_Q_ROWS])
    w_off = jnp.stack([0 * rr, rr, 0 * rr + NA_K_ROWS - NA_WIN_R])
    i = jnp.arange(NA_K_ROWS)
    dr = i[None, None, :] - q_off[:, :, None] + NA_WIN_R - 1
    row_ok = (i[None, None, :] >= w_off[:, :, None]) & (i[None, None, :] < w_off[:, :, None] + NA_WIN_R)
    tab = rpb_c[:, jnp.clip(dr, 0, 2 * NA_WIN_R - 2)]
    ok = row_ok[None, :, :, :, None, None] & col_mask[None, None, None, None]
    tab = jnp.where(ok, tab, NEG_BIG).transpose(0, 1, 2, 4, 3, 5)
    return tab.reshape(rpb.shape[0], NA_BLOCK_TYPES, NA_Q_ROWS * GRID_W, NA_K_ROWS * GRID_W)


def _na_kernel(q_ref, k_ref, v_ref, bias_ref, o_ref, *, rows, online):
    nrb = rows // NA_Q_ROWS
    tq = NA_Q_ROWS * GRID_W

    def row_block(rb, carry):
        ws = jnp.clip(rb * NA_Q_ROWS - NA_WIN_R // 2, 0, rows - NA_K_ROWS)
        start = pl.multiple_of(ws * GRID_W, GRID_W)
        qstart = pl.multiple_of(rb * tq, tq)
        kw = k_ref[pl.ds(start, NA_K_ROWS * GRID_W), :]
        vw = v_ref[pl.ds(start, NA_K_ROWS * GRID_W), :]
        block_type = jnp.where(rb == 0, 0, jnp.where(rb == nrb - 1, 2, 1))
        s = _qk(q_ref[pl.ds(qstart, tq), :], kw) + bias_ref[0, block_type]
        if online:
            s = s - jnp.max(s, axis=-1, keepdims=True)
        p = jnp.exp2(s)
        l = jnp.sum(p, axis=-1, keepdims=True)
        o = jnp.dot(p.astype(vw.dtype), vw, preferred_element_type=F32) / l
        o_ref[pl.ds(qstart, tq), :] = o.astype(o_ref.dtype)
        return carry

    lax.fori_loop(0, nrb, row_block, 0, unroll=2)


def _na_attention(qa, ka, proj, bias_tab, *, batch, seq, online):
    t = batch * seq
    rows = seq // GRID_W
    assert rows >= NA_K_ROWS and rows % (2 * NA_Q_ROWS) == 0
    vcb = OFF_VA // HEAD_DIM
    return pl.pallas_call(
        functools.partial(_na_kernel, rows=rows, online=online),
        grid=(batch, NA_HEADS),
        in_specs=[
            pl.BlockSpec((seq, HEAD_DIM), lambda b, h: (b, h)),
            pl.BlockSpec((seq, HEAD_DIM), lambda b, h: (b, h)),
            pl.BlockSpec((seq, HEAD_DIM), lambda b, h: (b, vcb + h)),
            pl.BlockSpec((1, NA_BLOCK_TYPES, NA_Q_ROWS * GRID_W, NA_K_ROWS * GRID_W), lambda b, h: (h, 0, 0, 0)),
        ],
        out_specs=pl.BlockSpec((seq, HEAD_DIM), lambda b, h: (b, h)),
        out_shape=jax.ShapeDtypeStruct((t, W_A), BF16),
        compiler_params=_params("parallel", "parallel"),
        name="na_attention",
    )(qa, ka, proj, bias_tab)


def _pick_softmax(plain_ok, attention, *operands):
    return lax.cond(plain_ok, functools.partial(attention, online=False), functools.partial(attention, online=True),
                    *operands)


def _merge_kernel(oa_ref, ob_ref, oc_ref, wa_ref, wb_ref, wc_ref, ga_ref, gb_ref, gc_ref, o_ref):
    def branch(o_r, w_r, g_r):
        y = jnp.dot(o_r[...], w_r[...], preferred_element_type=F32)
        return jax.nn.sigmoid(g_r[...].astype(F32)) * y

    merged = branch(oa_ref, wa_ref, ga_ref) + branch(ob_ref, wb_ref, gb_ref) + branch(oc_ref, wc_ref, gc_ref)
    o_ref[...] = merged.astype(o_ref.dtype)


def _merge(o_a, o_b, o_c, w_oa, w_ob, w_oc, layer, proj, tm=512, tn=1024):
    t = o_a.shape[0]
    d = w_oa.shape[2]
    tm, tn = min(tm, t), min(tn, d)
    assert OFF_GA % tn == 0 and d % tn == 0
    gcb = OFF_GA // tn
    ncb = d // tn

    def act(width):
        return pl.BlockSpec((tm, width), lambda j, i: (i, 0))

    def wgt(width):
        return pl.BlockSpec((None, width, tn), lambda j, i: (layer, 0, j))

    def gate(branch):
        return pl.BlockSpec((tm, tn), lambda j, i: (i, gcb + branch * ncb + j))

    return pl.pallas_call(
        _merge_kernel,
        grid=(ncb, t // tm),
        in_specs=[act(W_A), act(W_B_V), act(W_C_Q), wgt(W_A), wgt(W_B_V), wgt(W_C_Q), gate(0), gate(1), gate(2)],
        out_specs=pl.BlockSpec((tm, tn), lambda j, i: (i, j)),
        out_shape=jax.ShapeDtypeStruct((t, d), BF16),
        compiler_params=_params("parallel", "parallel"),
        name="gated_merge",
    )(o_a, o_b, o_c, w_oa, w_ob, w_oc, proj, proj, proj)


def _row(v):
    return v.reshape(1, -1).astype(F32)


def _layer(x, l, batch, seq, w, tables):
    rope1, rope2 = tables
    lam_init = 0.8 - 0.6 * math.exp(-0.3 * l)
    dims = dict(batch=batch, seq=seq)
    h = _rmsnorm(x, w["norm_mix"][l])
    proj = _matmul(h, w["w_in"], l, name="proj_in")

    qa = _headnorm(proj, OFF_QA, W_A, w["qn_a"][l], seq, scale=QK_SCALE_LOG2)
    ka = _headnorm(proj, OFF_KA, W_A, w["kn_a"][l], seq)
    o_a = _pick_softmax(_score_bound_ok(w["qn_a"][l], w["kn_a"][l], w["rpb"][l]),
                        functools.partial(_na_attention, **dims), qa, ka, proj, _na_bias_table(w["rpb"][l]))

    qb = _headnorm(proj, OFF_QB, W_B_QK, w["qn_b"][l], seq, rope=rope1, scale=QK_SCALE_LOG2)
    kb = _headnorm(proj, OFF_KB, W_B_QK, w["kn_b"][l], seq, rope=rope1)
    lam_vecs = tuple(_row(w[n][l]) for n in ("lam_q1", "lam_k1", "lam_q2", "lam_k2"))
    o_b = _pick_softmax(_score_bound_ok(w["qn_b"][l], w["kn_b"][l]),
                        functools.partial(_diff_attention, lam_init=lam_init, **dims),
                        qb, kb, proj, lam_vecs, _row(w["subln_b"][l]))

    qc = _headnorm(proj, OFF_QC, W_C_Q, w["qn_c"][l], seq, rope=rope2, scale=QK_SCALE_LOG2)
    kc = _headnorm(proj, OFF_KC, W_C_KV, w["kn_c"][l], seq, rope=rope2)
    o_c = _pick_softmax(_score_bound_ok(w["qn_c"][l], w["kn_c"][l]),
                        functools.partial(_gqa_attention, **dims), qc, kc, proj)

    merged = _merge(o_a, o_b, o_c, w["w_oa"], w["w_ob"], w["w_oc"], l, proj)
    x = _matmul(merged, w["w_out"], l, kind="residual", residual=x, out_dtype=F32, tn=512, name="proj_out")

    h = _rmsnorm(x, w["norm_mlp"][l])
    u = _matmul(h, w["w_up"], l, kind="relu2", name="mlp_up")
    return _matmul(u, w["w_down"], l, kind="residual", residual=x, out_dtype=F32, name="mlp_down")


_MATMUL_WEIGHTS = ("w_in", "w_oa", "w_ob", "w_oc", "w_out", "w_up", "w_down")


def kernel(x_prompt, x_sample, norm_mix, w_in, qn_a, kn_a, rpb, qn_b, kn_b, lam_q1, lam_k1, lam_q2, lam_k2,
           subln_b, qn_c, kn_c, w_oa, w_ob, w_oc, w_out, norm_mlp, w_up, w_down):
    w = dict(norm_mix=norm_mix, w_in=w_in, qn_a=qn_a, kn_a=kn_a, rpb=rpb, qn_b=qn_b, kn_b=kn_b, lam_q1=lam_q1,
             lam_k1=lam_k1, lam_q2=lam_q2, lam_k2=lam_k2, subln_b=subln_b, qn_c=qn_c, kn_c=kn_c, w_oa=w_oa,
             w_ob=w_ob, w_oc=w_oc, w_out=w_out, norm_mlp=norm_mlp, w_up=w_up, w_down=w_down)
    for name in _MATMUL_WEIGHTS:
        w[name] = w[name].astype(BF16)
    depth = w_in.shape[0]

    def trunk(x):
        batch, seq, d = x.shape
        tables = _rope_tables(seq)
        y = x.reshape(batch * seq, d)
        for l in range(depth):
            y = _layer(y, l, batch, seq, w, tables)
        return y.reshape(batch, seq, d)

    return (trunk(x_prompt), trunk(x_sample))
```

```python
import functools
import math

import jax
import jax.numpy as jnp
from jax import lax
from jax.experimental import pallas as pl
from jax.experimental.pallas import tpu as pltpu

F32 = jnp.float32
BF16 = jnp.bfloat16

HEAD_DIM = 128
GRID_W = 64
NA_HEADS = 12
NA_WIN_R = 8
NA_WIN_C = 16
DIFF_HEADS = 4
GQA_Q_HEADS = 12
GQA_KV_HEADS = 4
GQA_GROUP = GQA_Q_HEADS // GQA_KV_HEADS
ROPE_THETA = 10000.0
NORM_EPS = 1e-6

W_A = NA_HEADS * HEAD_DIM
W_B_QK = 2 * DIFF_HEADS * HEAD_DIM
W_B_V = DIFF_HEADS * 2 * HEAD_DIM
W_C_Q = GQA_Q_HEADS * HEAD_DIM
W_C_KV = GQA_KV_HEADS * HEAD_DIM

OFF_QA = 0
OFF_KA = OFF_QA + W_A
OFF_VA = OFF_KA + W_A
OFF_QB = OFF_VA + W_A
OFF_KB = OFF_QB + W_B_QK
OFF_VB = OFF_KB + W_B_QK
OFF_QC = OFF_VB + W_B_V
OFF_KC = OFF_QC + W_C_Q
OFF_VC = OFF_KC + W_C_KV
OFF_GA = OFF_VC + W_C_KV

V7X_VMEM_LIMIT_BYTES = 60 * 1024 * 1024

LOG2E = 1.4426950408889634
QK_SCALE_LOG2 = HEAD_DIM ** -0.5 * LOG2E
NEG_BIG = -1e30
PLAIN_SOFTMAX_LOG2_LIMIT = 64.0


def _params(*semantics):
    return pltpu.CompilerParams(dimension_semantics=semantics, vmem_limit_bytes=V7X_VMEM_LIMIT_BYTES)


def _rmsnorm_kernel(x_ref, g_ref, o_ref):
    x = x_ref[...]
    ms = jnp.mean(x * x, axis=-1, keepdims=True)
    o_ref[...] = (x * lax.rsqrt(ms + NORM_EPS) * g_ref[...]).astype(o_ref.dtype)


def _rmsnorm(x, gain, tm=256):
    t, d = x.shape
    tm = min(tm, t)
    return pl.pallas_call(
        _rmsnorm_kernel,
        grid=(t // tm,),
        in_specs=[pl.BlockSpec((tm, d), lambda i: (i, 0)), pl.BlockSpec((1, d), lambda i: (0, 0))],
        out_specs=pl.BlockSpec((tm, d), lambda i: (i, 0)),
        out_shape=jax.ShapeDtypeStruct((t, d), BF16),
        compiler_params=_params("parallel"),
        name="rmsnorm",
    )(x, gain.reshape(1, d).astype(F32))


def _matmul_kernel(*refs, kind, nk):
    if kind == "residual":
        a_ref, b_ref, res_ref, o_ref = refs
    else:
        a_ref, b_ref, o_ref = refs
    prod = jnp.dot(a_ref[...], b_ref[...], preferred_element_type=F32)
    if kind == "residual":
        if nk == 1:
            o_ref[...] = prod + res_ref[...]
        else:
            o_ref[...] = prod + jnp.where(pl.program_id(2) == 0, res_ref[...], o_ref[...])
    elif kind == "relu2":
        o_ref[...] = jnp.square(jnp.maximum(prod, 0.0)).astype(o_ref.dtype)
    else:
        o_ref[...] = prod.astype(o_ref.dtype)


def _matmul(a, w, layer, *, kind="plain", residual=None, out_dtype=BF16, tm=1024, tn=1024, tk=4096, name="matmul"):
    m, kdim = a.shape
    n = w.shape[2]
    tm, tn, tk = min(tm, m), min(tn, n), min(tk, kdim)
    nk = kdim // tk
    in_specs = [pl.BlockSpec((tm, tk), lambda i, j, k: (i, k)),
                pl.BlockSpec((None, tk, tn), lambda i, j, k: (layer, k, j))]
    args = [a, w]
    if kind == "residual":
        in_specs.append(pl.BlockSpec((tm, tn), lambda i, j, k: (i, j)))
        args.append(residual)
    assert nk == 1 or (kind == "residual" and out_dtype == F32)
    return pl.pallas_call(
        functools.partial(_matmul_kernel, kind=kind, nk=nk),
        grid=(m // tm, n // tn, nk),
        in_specs=in_specs,
        out_specs=pl.BlockSpec((tm, tn), lambda i, j, k: (i, j)),
        out_shape=jax.ShapeDtypeStruct((m, n), out_dtype),
        compiler_params=_params("parallel", "parallel", "arbitrary"),
        name=name,
    )(*args)


def _rotate_pairs(y, half):
    width = y.shape[-1]
    if 2 * half == width:
        return pltpu.roll(y, half, 1)
    lane = lax.broadcasted_iota(jnp.int32, y.shape, 1)
    first = (lane % (2 * half)) < half
    return jnp.where(first, pltpu.roll(y, width - half, 1), pltpu.roll(y, half, 1))


def _headnorm_kernel(*refs, heads, rot_half, scale):
    if rot_half:
        x_ref, g_ref, cos_ref, sin_ref, o_ref = refs
    else:
        x_ref, g_ref, o_ref = refs
    g = g_ref[...] * scale
    for h in range(heads):
        cols = slice(h * HEAD_DIM, (h + 1) * HEAD_DIM)
        x = x_ref[:, cols].astype(F32)
        ms = jnp.mean(x * x, axis=-1, keepdims=True)
        y = x * lax.rsqrt(ms + NORM_EPS) * g
        if rot_half:
            y = y * cos_ref[...] + _rotate_pairs(y, rot_half) * sin_ref[...]
        o_ref[:, cols] = y.astype(o_ref.dtype)


def _headnorm(proj, col_off, width, gain, seq, *, rope=None, scale=1.0, tm=1024):
    t = proj.shape[0]
    tm = min(tm, seq)
    heads = max(n for n in range(1, width // HEAD_DIM + 1)
                if width % (n * HEAD_DIM) == 0 and col_off % (n * HEAD_DIM) == 0)
    bw = heads * HEAD_DIM
    assert seq % tm == 0
    cb = col_off // bw
    in_specs = [pl.BlockSpec((tm, bw), lambda i, j: (i, cb + j)), pl.BlockSpec((1, HEAD_DIM), lambda i, j: (0, 0))]
    args = [proj, gain.reshape(1, HEAD_DIM).astype(F32)]
    rot_half = 0
    if rope is not None:
        cos, sin, rot_half = rope
        nsb = seq // tm
        tab_spec = pl.BlockSpec((tm, HEAD_DIM), lambda i, j: (i % nsb, 0))
        in_specs += [tab_spec, tab_spec]
        args += [cos, sin]
    return pl.pallas_call(
        functools.partial(_headnorm_kernel, heads=heads, rot_half=rot_half, scale=scale),
        grid=(t // tm, width // bw),
        in_specs=in_specs,
        out_specs=pl.BlockSpec((tm, bw), lambda i, j: (i, j)),
        out_shape=jax.ShapeDtypeStruct((t, width), BF16),
        compiler_params=_params("parallel", "parallel"),
        name="headnorm",
    )(*args)


def _rope_tables(seq):
    pos = jnp.arange(seq, dtype=jnp.int32)

    def angles(p, dr):
        inv = ROPE_THETA ** (-jnp.arange(0, dr, 2, dtype=F32) / dr)
        return p.astype(F32)[:, None] * inv[None, :]

    a1 = angles(pos, HEAD_DIM)
    cos1 = jnp.concatenate([jnp.cos(a1), jnp.cos(a1)], axis=-1)
    sin1 = jnp.concatenate([-jnp.sin(a1), jnp.sin(a1)], axis=-1)
    ar = angles(pos // GRID_W, HEAD_DIM // 2)
    ac = angles(pos % GRID_W, HEAD_DIM // 2)
    cos2 = jnp.concatenate([jnp.cos(ar), jnp.cos(ar), jnp.cos(ac), jnp.cos(ac)], axis=-1)
    sin2 = jnp.concatenate([-jnp.sin(ar), jnp.sin(ar), -jnp.sin(ac), jnp.sin(ac)], axis=-1)
    return (cos1, sin1, HEAD_DIM // 2), (cos2, sin2, HEAD_DIM // 4)


def _qk(q, k):
    return lax.dot_general(q, k, (((1,), (1,)), ((), ())), preferred_element_type=F32)


def _score_bound_ok(q_gain, k_gain, bias=None):
    bound = HEAD_DIM * QK_SCALE_LOG2 * jnp.max(jnp.abs(q_gain)) * jnp.max(jnp.abs(k_gain))
    if bias is not None:
        bound = bound + LOG2E * jnp.max(jnp.abs(bias))
    return bound <= PLAIN_SOFTMAX_LOG2_LIMIT


def _lane_fold(p):
    out = p[:, :HEAD_DIM]
    for c in range(1, p.shape[1] // HEAD_DIM):
        out = out + p[:, c * HEAD_DIM:(c + 1) * HEAD_DIM]
    return out


def _online_step(s, vj, m, l, acc):
    m_new = jnp.maximum(m, jnp.max(s, axis=-1, keepdims=True))
    alpha = jnp.exp2(m - m_new)
    p = jnp.exp2(s - m_new)
    l_new = alpha * l + jnp.sum(p, axis=-1, keepdims=True)
    acc_new = alpha * acc + jnp.dot(p.astype(vj.dtype), vj, preferred_element_type=F32)
    return m_new, l_new, acc_new


def _attend(score_fn, k_ref, v_ref, rows, dv, seq, tk, online, scratch):
    def chunk(j):
        start = pl.multiple_of(j * tk, tk)
        return k_ref[pl.ds(start, tk), :], v_ref[pl.ds(start, tk), :]

    if online:
        def body(j, carry):
            kj, vj = chunk(j)
            return _online_step(score_fn(kj), vj, *carry)

        init = (jnp.full((rows, 1), -jnp.inf, F32), jnp.zeros((rows, 1), F32), jnp.zeros((rows, dv), F32))
        _, l, acc = lax.fori_loop(0, seq // tk, body, init)
        return acc / l

    acc_ref, l_ref = scratch
    acc_ref[...] = jnp.zeros_like(acc_ref)
    l_ref[...] = jnp.zeros_like(l_ref)

    def body(j, carry):
        kj, vj = chunk(j)
        p = jnp.exp2(score_fn(kj))
        l_ref[...] += _lane_fold(p)
        acc_ref[...] += jnp.dot(p.astype(vj.dtype), vj, preferred_element_type=F32)
        return carry

    lax.fori_loop(0, seq // tk, body, 0)
    return acc_ref[...] / jnp.sum(l_ref[...], axis=-1, keepdims=True)


def _attend_scratch(rows, dv, online):
    return [] if online else [pltpu.VMEM((rows, dv), F32), pltpu.VMEM((rows, HEAD_DIM), F32)]


def _gqa_kernel(q_ref, k_ref, v_ref, o_ref, *scratch, seq, tk, online):
    tq = q_ref.shape[0]
    q = jnp.concatenate([q_ref[:, g * HEAD_DIM:(g + 1) * HEAD_DIM] for g in range(GQA_GROUP)], axis=0)
    o = _attend(lambda kj: _qk(q, kj), k_ref, v_ref, GQA_GROUP * tq, HEAD_DIM, seq, tk, online, scratch)
    for g in range(GQA_GROUP):
        o_ref[:, g * HEAD_DIM:(g + 1) * HEAD_DIM] = o[g * tq:(g + 1) * tq].astype(o_ref.dtype)


def _gqa_attention(qc, kc, proj, *, batch, seq, online, tq=256, tk=1024):
    t = batch * seq
    tq, tk = min(tq, seq), min(tk, seq)
    nqb = seq // tq
    gw = GQA_GROUP * HEAD_DIM
    vcb = OFF_VC // HEAD_DIM
    return pl.pallas_call(
        functools.partial(_gqa_kernel, seq=seq, tk=tk, online=online),
        grid=(batch, GQA_KV_HEADS, nqb),
        in_specs=[
            pl.BlockSpec((tq, gw), lambda b, n, i: (b * nqb + i, n)),
            pl.BlockSpec((seq, HEAD_DIM), lambda b, n, i: (b, n)),
            pl.BlockSpec((seq, HEAD_DIM), lambda b, n, i: (b, vcb + n)),
        ],
        out_specs=pl.BlockSpec((tq, gw), lambda b, n, i: (b * nqb + i, n)),
        out_shape=jax.ShapeDtypeStruct((t, W_C_Q), BF16),
        scratch_shapes=_attend_scratch(GQA_GROUP * tq, HEAD_DIM, online),
        compiler_params=_params("parallel", "parallel", "parallel"),
        name="gqa_attention",
    )(qc, kc, proj)


def _diff_kernel(q_ref, k_ref, v_ref, lq1_ref, lk1_ref, lq2_ref, lk2_ref, sg_ref, o_ref, *scratch,
                 seq, tk, lam_init, online):
    tq = q_ref.shape[0]
    dv = 2 * HEAD_DIM
    q1 = q_ref[:, :HEAD_DIM]
    q2 = q_ref[:, HEAD_DIM:]

    def scores(kj):
        return jnp.concatenate([_qk(q1, kj[:, :HEAD_DIM]), _qk(q2, kj[:, HEAD_DIM:])], axis=0)

    o12 = _attend(scores, k_ref, v_ref, 2 * tq, dv, seq, tk, online, scratch)
    lam = (jnp.exp(jnp.sum(lq1_ref[...] * lk1_ref[...], axis=-1, keepdims=True))
           - jnp.exp(jnp.sum(lq2_ref[...] * lk2_ref[...], axis=-1, keepdims=True)) + lam_init)
    o = o12[:tq] - lam * o12[tq:]
    ms = jnp.mean(o * o, axis=-1, keepdims=True)
    o = o * lax.rsqrt(ms + NORM_EPS) * sg_ref[...] * (1.0 - lam_init)
    o_ref[...] = o.astype(o_ref.dtype)


def _diff_attention(qb, kb, proj, lam_vecs, sub_gain, *, lam_init, batch, seq, online, tq=512, tk=1024):
    t = batch * seq
    tq, tk = min(tq, seq), min(tk, seq)
    nqb = seq // tq
    pw = 2 * HEAD_DIM
    vcb = OFF_VB // pw
    vec_spec = pl.BlockSpec((1, HEAD_DIM), lambda b, h, i: (0, 0))
    return pl.pallas_call(
        functools.partial(_diff_kernel, seq=seq, tk=tk, lam_init=lam_init, online=online),
        grid=(batch, DIFF_HEADS, nqb),
        in_specs=[
            pl.BlockSpec((tq, pw), lambda b, h, i: (b * nqb + i, h)),
            pl.BlockSpec((seq, pw), lambda b, h, i: (b, h)),
            pl.BlockSpec((seq, pw), lambda b, h, i: (b, vcb + h)),
            vec_spec, vec_spec, vec_spec, vec_spec,
            pl.BlockSpec((1, pw), lambda b, h, i: (0, 0)),
        ],
        out_specs=pl.BlockSpec((tq, pw), lambda b, h, i: (b * nqb + i, h)),
        out_shape=jax.ShapeDtypeStruct((t, W_B_V), BF16),
        scratch_shapes=_attend_scratch(2 * tq, pw, online),
        compiler_params=_params("parallel", "parallel", "parallel"),
        name="diff_attention",
    )(qb, kb, proj, *lam_vecs, sub_gain)


NA_Q_ROWS = 4
NA_K_ROWS = NA_Q_ROWS + NA_WIN_R
NA_BLOCK_TYPES = 3


def _na_bias_table(rpb):
    col = jnp.arange(GRID_W)
    col_start = jnp.clip(col - NA_WIN_C // 2, 0, GRID_W - NA_WIN_C)
    col_mask = (col[None, :] >= col_start[:, None]) & (col[None, :] < col_start[:, None] + NA_WIN_C)
    dc = jnp.clip(col[None, :] - col[:, None], -(NA_WIN_C - 1), NA_WIN_C - 1) + NA_WIN_C - 1
    rpb_c = rpb.astype(F32)[:, :, dc] * LOG2E
    rr = jnp.arange(NA_Q_ROWS)
    q_off = jnp.stack([rr, rr + NA_WIN_R // 2, rr + NA_K_ROWS - NA_Q_ROWS])
    w_off = jnp.stack([0 * rr, rr, 0 * rr + NA_K_ROWS - NA_WIN_R])
    i = jnp.arange(NA_K_ROWS)
    dr = i[None, None, :] - q_off[:, :, None] + NA_WIN_R - 1
    row_ok = (i[None, None, :] >= w_off[:, :, None]) & (i[None, None, :] < w_off[:, :, None] + NA_WIN_R)
    tab = rpb_c[:, jnp.clip(dr, 0, 2 * NA_WIN_R - 2)]
    ok = row_ok[None, :, :, :, None, None] & col_mask[None, None, None, None]
    tab = jnp.where(ok, tab, NEG_BIG).transpose(0, 1, 2, 4, 3, 5)
    return tab.reshape(rpb.shape[0], NA_BLOCK_TYPES, NA_Q_ROWS * GRID_W, NA_K_ROWS * GRID_W)


def _na_kernel(q_ref, k_ref, v_ref, bias_ref, o_ref, *, rows, online):
    nrb = rows // NA_Q_ROWS
    tq = NA_Q_ROWS * GRID_W

    def row_block(rb, carry):
        ws = jnp.clip(rb * NA_Q_ROWS - NA_WIN_R // 2, 0, rows - NA_K_ROWS)
        start = pl.multiple_of(ws * GRID_W, GRID_W)
        qstart = pl.multiple_of(rb * tq, tq)
        kw = k_ref[pl.ds(start, NA_K_ROWS * GRID_W), :]
        vw = v_ref[pl.ds(start, NA_K_ROWS * GRID_W), :]
        block_type = jnp.where(rb == 0, 0, jnp.where(rb == nrb - 1, 2, 1))
        s = _qk(q_ref[pl.ds(qstart, tq), :], kw) + bias_ref[0, block_type]
        if online:
            s = s - jnp.max(s, axis=-1, keepdims=True)
        p = jnp.exp2(s)
        l = jnp.sum(p, axis=-1, keepdims=True)
        o = jnp.dot(p.astype(vw.dtype), vw, preferred_element_type=F32) / l
        o_ref[pl.ds(qstart, tq), :] = o.astype(o_ref.dtype)
        return carry

    lax.fori_loop(0, nrb, row_block, 0, unroll=2)


def _na_attention(qa, ka, proj, bias_tab, *, batch, seq, online):
    t = batch * seq
    rows = seq // GRID_W
    assert rows >= NA_K_ROWS and rows % (2 * NA_Q_ROWS) == 0
    vcb = OFF_VA // HEAD_DIM
    return pl.pallas_call(
        functools.partial(_na_kernel, rows=rows, online=online),
        grid=(batch, NA_HEADS),
        in_specs=[
            pl.BlockSpec((seq, HEAD_DIM), lambda b, h: (b, h)),
            pl.BlockSpec((seq, HEAD_DIM), lambda b, h: (b, h)),
            pl.BlockSpec((seq, HEAD_DIM), lambda b, h: (b, vcb + h)),
            pl.BlockSpec((1, NA_BLOCK_TYPES, NA_Q_ROWS * GRID_W, NA_K_ROWS * GRID_W), lambda b, h: (h, 0, 0, 0)),
        ],
        out_specs=pl.BlockSpec((seq, HEAD_DIM), lambda b, h: (b, h)),
        out_shape=jax.ShapeDtypeStruct((t, W_A), BF16),
        compiler_params=_params("parallel", "parallel"),
        name="na_attention",
    )(qa, ka, proj, bias_tab)


def _pick_softmax(plain_ok, attention, *operands):
    return lax.cond(plain_ok, functools.partial(attention, online=False), functools.partial(attention, online=True),
                    *operands)


def _merge_kernel(oa_ref, ob_ref, oc_ref, wa_ref, wb_ref, wc_ref, ga_ref, gb_ref, gc_ref, o_ref):
    def branch(o_r, w_r, g_r):
        y = jnp.dot(o_r[...], w_r[...], preferred_element_type=F32)
        return jax.nn.sigmoid(g_r[...].astype(F32)) * y

    merged = branch(oa_ref, wa_ref, ga_ref) + branch(ob_ref, wb_ref, gb_ref) + branch(oc_ref, wc_ref, gc_ref)
    o_ref[...] = merged.astype(o_ref.dtype)


def _merge(o_a, o_b, o_c, w_oa, w_ob, w_oc, layer, proj, tm=1024, tn=1024):
    t = o_a.shape[0]
    d = w_oa.shape[2]
    tm, tn = min(tm, t), min(tn, d)
    assert OFF_GA % tn == 0 and d % tn == 0
    gcb = OFF_GA // tn
    ncb = d // tn

    def act(width):
        return pl.BlockSpec((tm, width), lambda j, i: (i, 0))

    def wgt(width):
        return pl.BlockSpec((None, width, tn), lambda j, i: (layer, 0, j))

    def gate(branch):
        return pl.BlockSpec((tm, tn), lambda j, i: (i, gcb + branch * ncb + j))

    return pl.pallas_call(
        _merge_kernel,
        grid=(ncb, t // tm),
        in_specs=[act(W_A), act(W_B_V), act(W_C_Q), wgt(W_A), wgt(W_B_V), wgt(W_C_Q), gate(0), gate(1), gate(2)],
        out_specs=pl.BlockSpec((tm, tn), lambda j, i: (i, j)),
        out_shape=jax.ShapeDtypeStruct((t, d), BF16),
        compiler_params=_params("parallel", "parallel"),
        name="gated_merge",
    )(o_a, o_b, o_c, w_oa, w_ob, w_oc, proj, proj, proj)


def _row(v):
    return v.reshape(1, -1).astype(F32)


def _layer(x, l, batch, seq, w, tables):
    rope1, rope2 = tables
    lam_init = 0.8 - 0.6 * math.exp(-0.3 * l)
    dims = dict(batch=batch, seq=seq)
    h = _rmsnorm(x, w["norm_mix"][l])
    proj = _matmul(h, w["w_in"], l, name="proj_in")

    qa = _headnorm(proj, OFF_QA, W_A, w["qn_a"][l], seq, scale=QK_SCALE_LOG2)
    ka = _headnorm(proj, OFF_KA, W_A, w["kn_a"][l], seq)
    o_a = _pick_softmax(_score_bound_ok(w["qn_a"][l], w["kn_a"][l], w["rpb"][l]),
                        functools.partial(_na_attention, **dims), qa, ka, proj, w["na_bias"][l])

    qb = _headnorm(proj, OFF_QB, W_B_QK, w["qn_b"][l], seq, rope=rope1, scale=QK_SCALE_LOG2)
    kb = _headnorm(proj, OFF_KB, W_B_QK, w["kn_b"][l], seq, rope=rope1)
    lam_vecs = tuple(_row(w[n][l]) for n in ("lam_q1", "lam_k1", "lam_q2", "lam_k2"))
    o_b = _pick_softmax(_score_bound_ok(w["qn_b"][l], w["kn_b"][l]),
                        functools.partial(_diff_attention, lam_init=lam_init, **dims),
                        qb, kb, proj, lam_vecs, _row(w["subln_b"][l]))

    qc = _headnorm(proj, OFF_QC, W_C_Q, w["qn_c"][l], seq, rope=rope2, scale=QK_SCALE_LOG2)
    kc = _headnorm(proj, OFF_KC, W_C_KV, w["kn_c"][l], seq, rope=rope2)
    o_c = _pick_softmax(_score_bound_ok(w["qn_c"][l], w["kn_c"][l]),
                        functools.partial(_gqa_attention, **dims), qc, kc, proj)

    merged = _merge(o_a, o_b, o_c, w["w_oa"], w["w_ob"], w["w_oc"], l, proj)
    x = _matmul(merged, w["w_out"], l, kind="residual", residual=x, out_dtype=F32, name="proj_out")

    h = _rmsnorm(x, w["norm_mlp"][l])
    u = _matmul(h, w["w_up"], l, kind="relu2", name="mlp_up")
    return _matmul(u, w["w_down"], l, kind="residual", residual=x, out_dtype=F32, name="mlp_down")


_MATMUL_WEIGHTS = ("w_in", "w_oa", "w_ob", "w_oc", "w_out", "w_up", "w_down")


def kernel(x_prompt, x_sample, norm_mix, w_in, qn_a, kn_a, rpb, qn_b, kn_b, lam_q1, lam_k1, lam_q2, lam_k2,
           subln_b, qn_c, kn_c, w_oa, w_ob, w_oc, w_out, norm_mlp, w_up, w_down):
    w = dict(norm_mix=norm_mix, w_in=w_in, qn_a=qn_a, kn_a=kn_a, rpb=rpb, qn_b=qn_b, kn_b=kn_b, lam_q1=lam_q1,
             lam_k1=lam_k1, lam_q2=lam_q2, lam_k2=lam_k2, subln_b=subln_b, qn_c=qn_c, kn_c=kn_c, w_oa=w_oa,
             w_ob=w_ob, w_oc=w_oc, w_out=w_out, norm_mlp=norm_mlp, w_up=w_up, w_down=w_down)
    for name in _MATMUL_WEIGHTS:
        w[name] = w[name].astype(BF16)
    depth = w_in.shape[0]
    w["na_bias"] = [_na_bias_table(rpb[l]) for l in range(depth)]

    def trunk(x):
        batch, seq, d = x.shape
        tables = _rope_tables(seq)
        y = x.reshape(batch * seq, d)
        for l in range(depth):
            y = _layer(y, l, batch, seq, w, tables)
        return y.reshape(batch, seq, d)

    return (trunk(x_prompt), trunk(x_sample))
```

```python
import functools
import math

import jax
import jax.numpy as jnp
from jax import lax
from jax.experimental import pallas as pl
from jax.experimental.pallas import tpu as pltpu

F32 = jnp.float32
BF16 = jnp.bfloat16

LANES = 128
HEAD_DIM = 128
GRID_W = 64
NA_HEADS = 12
NA_WIN_R = 8
NA_WIN_C = 16
DIFF_HEADS = 4
GQA_Q_HEADS = 12
GQA_KV_HEADS = 4
GQA_GROUP = GQA_Q_HEADS // GQA_KV_HEADS
ROPE_THETA = 10000.0
NORM_EPS = 1e-6

W_A = NA_HEADS * HEAD_DIM
W_B_QK = 2 * DIFF_HEADS * HEAD_DIM
W_B_V = DIFF_HEADS * 2 * HEAD_DIM
W_C_Q = GQA_Q_HEADS * HEAD_DIM
W_C_KV = GQA_KV_HEADS * HEAD_DIM

OFF_QA = 0
OFF_KA = OFF_QA + W_A
OFF_VA = OFF_KA + W_A
OFF_QB = OFF_VA + W_A
OFF_KB = OFF_QB + W_B_QK
OFF_VB = OFF_KB + W_B_QK
OFF_QC = OFF_VB + W_B_V
OFF_KC = OFF_QC + W_C_Q
OFF_VC = OFF_KC + W_C_KV
OFF_GA = OFF_VC + W_C_KV

V7X_VMEM_LIMIT_BYTES = 62 * 1024 * 1024

LOG2E = 1.4426950408889634
QK_SCALE_LOG2 = HEAD_DIM ** -0.5 * LOG2E
NEG_BIG = -1e30
PLAIN_SOFTMAX_LOG2_LIMIT = 64.0


def _params(*semantics):
    return pltpu.CompilerParams(dimension_semantics=semantics, vmem_limit_bytes=V7X_VMEM_LIMIT_BYTES)


def _rmsnorm_kernel(x_ref, g_ref, o_ref):
    x = x_ref[...]
    ms = jnp.mean(x * x, axis=-1, keepdims=True)
    o_ref[...] = (x * lax.rsqrt(ms + NORM_EPS) * g_ref[...]).astype(o_ref.dtype)


def _rmsnorm(x, gain, tm=256):
    t, d = x.shape
    tm = min(tm, t)
    return pl.pallas_call(
        _rmsnorm_kernel,
        grid=(t // tm,),
        in_specs=[pl.BlockSpec((tm, d), lambda i: (i, 0)), pl.BlockSpec((1, d), lambda i: (0, 0))],
        out_specs=pl.BlockSpec((tm, d), lambda i: (i, 0)),
        out_shape=jax.ShapeDtypeStruct((t, d), BF16),
        compiler_params=_params("parallel"),
        name="rmsnorm",
    )(x, gain.reshape(1, d).astype(F32))


def _matmul_kernel(*refs, kind, nk, kdim):
    a_ref, b_ref = refs[:2]
    prod = jnp.dot(a_ref[...], b_ref[...], preferred_element_type=F32)
    if kind == "residual":
        res_ref, o_ref = refs[2:]
        if nk == 1:
            o_ref[...] = prod + res_ref[...]
        else:
            o_ref[...] = prod + jnp.where(pl.program_id(2) == 0, res_ref[...], o_ref[...])
    elif kind == "residual_stats":
        res_ref, o_ref, ob_ref, ss_ref = refs[2:]
        x = prod + res_ref[...]
        o_ref[...] = x
        ob_ref[...] = x.astype(ob_ref.dtype)
        ss_ref[...] = _lane_fold(x * x)
    elif kind == "rownorm_relu2":
        ss_ref, o_ref = refs[2:]
        ms = jnp.sum(ss_ref[...], axis=-1, keepdims=True) * (1.0 / kdim)
        o_ref[...] = jnp.square(jnp.maximum(prod * lax.rsqrt(ms + NORM_EPS), 0.0)).astype(o_ref.dtype)
    else:
        (o_ref,) = refs[2:]
        o_ref[...] = prod.astype(o_ref.dtype)


def _matmul(a, w, layer, *, kind="plain", residual=None, row_sumsq=None, out_dtype=BF16,
            tm=1024, tn=1024, tk=4096, name="matmul"):
    m, kdim = a.shape
    n = w.shape[2]
    tm, tn, tk = min(tm, m), min(tn, n), min(tk, kdim)
    nk = kdim // tk
    tile = pl.BlockSpec((tm, tn), lambda i, j, k: (i, j))
    in_specs = [pl.BlockSpec((tm, tk), lambda i, j, k: (i, k)),
                pl.BlockSpec((None, tk, tn), lambda i, j, k: (layer, k, j))]
    args = [a, w]
    out_specs, out_shape = tile, jax.ShapeDtypeStruct((m, n), out_dtype)
    if kind in ("residual", "residual_stats"):
        in_specs.append(tile)
        args.append(residual)
    if kind == "residual_stats":
        out_specs = (tile, tile, pl.BlockSpec((tm, LANES), lambda i, j, k: (i, j)))
        out_shape = (out_shape, jax.ShapeDtypeStruct((m, n), BF16),
                     jax.ShapeDtypeStruct((m, n // tn * LANES), F32))
    if kind == "rownorm_relu2":
        in_specs.append(pl.BlockSpec((tm, row_sumsq.shape[1]), lambda i, j, k: (i, 0)))
        args.append(row_sumsq)
    assert nk == 1 or (kind == "residual" and out_dtype == F32)
    return pl.pallas_call(
        functools.partial(_matmul_kernel, kind=kind, nk=nk, kdim=kdim),
        grid=(m // tm, n // tn, nk),
        in_specs=in_specs,
        out_specs=out_specs,
        out_shape=out_shape,
        compiler_params=_params("parallel", "parallel", "arbitrary"),
        name=name,
    )(*args)


def _rotate_pairs(y, half):
    width = y.shape[-1]
    if 2 * half == width:
        return pltpu.roll(y, half, 1)
    lane = lax.broadcasted_iota(jnp.int32, y.shape, 1)
    first = (lane % (2 * half)) < half
    return jnp.where(first, pltpu.roll(y, width - half, 1), pltpu.roll(y, half, 1))


def _headnorm_kernel(*refs, heads, rot_half, scale):
    if rot_half:
        x_ref, g_ref, cos_ref, sin_ref, o_ref = refs
    else:
        x_ref, g_ref, o_ref = refs
    g = g_ref[...] * scale
    for h in range(heads):
        cols = slice(h * HEAD_DIM, (h + 1) * HEAD_DIM)
        x = x_ref[:, cols].astype(F32)
        ms = jnp.mean(x * x, axis=-1, keepdims=True)
        y = x * lax.rsqrt(ms + NORM_EPS) * g
        if rot_half:
            y = y * cos_ref[...] + _rotate_pairs(y, rot_half) * sin_ref[...]
        o_ref[:, cols] = y.astype(o_ref.dtype)


def _headnorm(proj, col_off, width, gain, seq, *, rope=None, scale=1.0, tm=1024):
    t = proj.shape[0]
    tm = min(tm, seq)
    heads = max(n for n in range(1, width // HEAD_DIM + 1)
                if width % (n * HEAD_DIM) == 0 and col_off % (n * HEAD_DIM) == 0)
    bw = heads * HEAD_DIM
    assert seq % tm == 0
    cb = col_off // bw
    in_specs = [pl.BlockSpec((tm, bw), lambda i, j: (i, cb + j)), pl.BlockSpec((1, HEAD_DIM), lambda i, j: (0, 0))]
    args = [proj, gain.reshape(1, HEAD_DIM).astype(F32)]
    rot_half = 0
    if rope is not None:
        cos, sin, rot_half = rope
        nsb = seq // tm
        tab_spec = pl.BlockSpec((tm, HEAD_DIM), lambda i, j: (i % nsb, 0))
        in_specs += [tab_spec, tab_spec]
        args += [cos, sin]
    return pl.pallas_call(
        functools.partial(_headnorm_kernel, heads=heads, rot_half=rot_half, scale=scale),
        grid=(t // tm, width // bw),
        in_specs=in_specs,
        out_specs=pl.BlockSpec((tm, bw), lambda i, j: (i, j)),
        out_shape=jax.ShapeDtypeStruct((t, width), BF16),
        compiler_params=_params("parallel", "parallel"),
        name="headnorm",
    )(*args)


def _rope_tables(seq):
    pos = jnp.arange(seq, dtype=jnp.int32)

    def angles(p, dr):
        inv = ROPE_THETA ** (-jnp.arange(0, dr, 2, dtype=F32) / dr)
        return p.astype(F32)[:, None] * inv[None, :]

    a1 = angles(pos, HEAD_DIM)
    cos1 = jnp.concatenate([jnp.cos(a1), jnp.cos(a1)], axis=-1)
    sin1 = jnp.concatenate([-jnp.sin(a1), jnp.sin(a1)], axis=-1)
    ar = angles(pos // GRID_W, HEAD_DIM // 2)
    ac = angles(pos % GRID_W, HEAD_DIM // 2)
    cos2 = jnp.concatenate([jnp.cos(ar), jnp.cos(ar), jnp.cos(ac), jnp.cos(ac)], axis=-1)
    sin2 = jnp.concatenate([-jnp.sin(ar), jnp.sin(ar), -jnp.sin(ac), jnp.sin(ac)], axis=-1)
    return (cos1, sin1, HEAD_DIM // 2), (cos2, sin2, HEAD_DIM // 4)


def _qk(q, k):
    return lax.dot_general(q, k, (((1,), (1,)), ((), ())), preferred_element_type=F32)


def _score_bound_ok(q_gain, k_gain, bias=None):
    bound = HEAD_DIM * QK_SCALE_LOG2 * jnp.max(jnp.abs(q_gain)) * jnp.max(jnp.abs(k_gain))
    if bias is not None:
        bound = bound + LOG2E * jnp.max(jnp.abs(bias))
    return bound <= PLAIN_SOFTMAX_LOG2_LIMIT


def _lane_fold(p):
    out = p[:, :LANES]
    for c in range(1, p.shape[1] // LANES):
        out = out + p[:, c * LANES:(c + 1) * LANES]
    return out


def _online_step(s, vj, m, l, acc):
    m_new = jnp.maximum(m, jnp.max(s, axis=-1, keepdims=True))
    alpha = jnp.exp2(m - m_new)
    p = jnp.exp2(s - m_new)
    l_new = alpha * l + jnp.sum(p, axis=-1, keepdims=True)
    acc_new = alpha * acc + jnp.dot(p.astype(vj.dtype), vj, preferred_element_type=F32)
    return m_new, l_new, acc_new


def _attend(score_fn, k_ref, v_ref, rows, dv, seq, tk, online, scratch):
    def chunk(j):
        start = pl.multiple_of(j * tk, tk)
        return k_ref[pl.ds(start, tk), :], v_ref[pl.ds(start, tk), :]

    if online:
        def body(j, carry):
            kj, vj = chunk(j)
            return _online_step(score_fn(kj), vj, *carry)

        init = (jnp.full((rows, 1), -jnp.inf, F32), jnp.zeros((rows, 1), F32), jnp.zeros((rows, dv), F32))
        _, l, acc = lax.fori_loop(0, seq // tk, body, init)
        return acc / l

    acc_ref, l_ref = scratch
    acc_ref[...] = jnp.zeros_like(acc_ref)
    l_ref[...] = jnp.zeros_like(l_ref)

    def body(j, carry):
        kj, vj = chunk(j)
        p = jnp.exp2(score_fn(kj))
        l_ref[...] += _lane_fold(p)
        acc_ref[...] = jnp.dot(p.astype(vj.dtype), vj, preferred_element_type=F32) + acc_ref[...]
        return carry

    lax.fori_loop(0, seq // tk, body, 0)
    return acc_ref[...] / jnp.sum(l_ref[...], axis=-1, keepdims=True)


def _attend_scratch(rows, dv, online):
    return [] if online else [pltpu.VMEM((rows, dv), F32), pltpu.VMEM((rows, HEAD_DIM), F32)]


def _gqa_kernel(q_ref, k_ref, v_ref, o_ref, *scratch, seq, tk, online):
    tq = q_ref.shape[0]
    q = jnp.concatenate([q_ref[:, g * HEAD_DIM:(g + 1) * HEAD_DIM] for g in range(GQA_GROUP)], axis=0)
    o = _attend(lambda kj: _qk(q, kj), k_ref, v_ref, GQA_GROUP * tq, HEAD_DIM, seq, tk, online, scratch)
    for g in range(GQA_GROUP):
        o_ref[:, g * HEAD_DIM:(g + 1) * HEAD_DIM] = o[g * tq:(g + 1) * tq].astype(o_ref.dtype)


def _gqa_plain_kernel(q_ref, k_ref, vt_ref, o_ref, acc_ref, l_ref, *, seq, tk):
    tq = q_ref.shape[0]
    m = GQA_GROUP * tq
    q = jnp.concatenate([q_ref[:, g * HEAD_DIM:(g + 1) * HEAD_DIM] for g in range(GQA_GROUP)], axis=0)
    acc_ref[...] = jnp.zeros_like(acc_ref)
    l_ref[...] = jnp.zeros_like(l_ref)

    def body(j, carry):
        start = pl.multiple_of(j * tk, tk)
        pt = jnp.exp2(_qk(k_ref[pl.ds(start, tk), :], q))
        l_ref[...] += pt.reshape(tk // 8, 8, m).sum(axis=0)
        acc_ref[...] = jnp.dot(vt_ref[:, pl.ds(start, tk)], pt.astype(BF16), preferred_element_type=F32) + acc_ref[...]
        return carry

    lax.fori_loop(0, seq // tk, body, 0)
    o = (acc_ref[...] / jnp.sum(l_ref[...], axis=0, keepdims=True)).T
    for g in range(GQA_GROUP):
        o_ref[:, g * HEAD_DIM:(g + 1) * HEAD_DIM] = o[g * tq:(g + 1) * tq].astype(o_ref.dtype)


def _gqa_attention(qc, kc, proj, *, batch, seq, online, tq=512, tk=1024):
    t = batch * seq
    tq, tk = min(tq, seq), min(tk, seq)
    nqb = seq // tq
    gw = GQA_GROUP * HEAD_DIM
    vcb = OFF_VC // HEAD_DIM
    if not online:
        v_t = proj[:, OFF_VC:OFF_VC + W_C_KV].T
        return pl.pallas_call(
            functools.partial(_gqa_plain_kernel, seq=seq, tk=tk),
            grid=(batch, GQA_KV_HEADS, nqb),
            in_specs=[
                pl.BlockSpec((tq, gw), lambda b, n, i: (b * nqb + i, n)),
                pl.BlockSpec((seq, HEAD_DIM), lambda b, n, i: (b, n)),
                pl.BlockSpec((HEAD_DIM, seq), lambda b, n, i: (n, b)),
            ],
            out_specs=pl.BlockSpec((tq, gw), lambda b, n, i: (b * nqb + i, n)),
            out_shape=jax.ShapeDtypeStruct((t, W_C_Q), BF16),
            scratch_shapes=[pltpu.VMEM((HEAD_DIM, GQA_GROUP * tq), F32), pltpu.VMEM((8, GQA_GROUP * tq), F32)],
            compiler_params=_params("parallel", "parallel", "parallel"),
            name="gqa_attention",
        )(qc, kc, v_t)
    return pl.pallas_call(
        functools.partial(_gqa_kernel, seq=seq, tk=tk, online=online),
        grid=(batch, GQA_KV_HEADS, nqb),
        in_specs=[
            pl.BlockSpec((tq, gw), lambda b, n, i: (b * nqb + i, n)),
            pl.BlockSpec((seq, HEAD_DIM), lambda b, n, i: (b, n)),
            pl.BlockSpec((seq, HEAD_DIM), lambda b, n, i: (b, vcb + n)),
        ],
        out_specs=pl.BlockSpec((tq, gw), lambda b, n, i: (b * nqb + i, n)),
        out_shape=jax.ShapeDtypeStruct((t, W_C_Q), BF16),
        scratch_shapes=_attend_scratch(GQA_GROUP * tq, HEAD_DIM, online),
        compiler_params=_params("parallel", "parallel", "parallel"),
        name="gqa_attention",
    )(qc, kc, proj)


def _diff_kernel(q_ref, k_ref, v_ref, lq1_ref, lk1_ref, lq2_ref, lk2_ref, sg_ref, o_ref, *scratch,
                 seq, tk, lam_init, online):
    tq = q_ref.shape[0]
    dv = 2 * HEAD_DIM
    q1 = q_ref[:, :HEAD_DIM]
    q2 = q_ref[:, HEAD_DIM:]

    def scores(kj):
        return jnp.concatenate([_qk(q1, kj[:, :HEAD_DIM]), _qk(q2, kj[:, HEAD_DIM:])], axis=0)

    o12 = _attend(scores, k_ref, v_ref, 2 * tq, dv, seq, tk, online, scratch)
    lam = (jnp.exp(jnp.sum(lq1_ref[...] * lk1_ref[...], axis=-1, keepdims=True))
           - jnp.exp(jnp.sum(lq2_ref[...] * lk2_ref[...], axis=-1, keepdims=True)) + lam_init)
    o = o12[:tq] - lam * o12[tq:]
    ms = jnp.mean(o * o, axis=-1, keepdims=True)
    o = o * lax.rsqrt(ms + NORM_EPS) * sg_ref[...] * (1.0 - lam_init)
    o_ref[...] = o.astype(o_ref.dtype)


def _diff_attention(qb, kb, proj, lam_vecs, sub_gain, *, lam_init, batch, seq, online, tq=512, tk=1024):
    t = batch * seq
    tq, tk = min(tq, seq), min(tk, seq)
    nqb = seq // tq
    pw = 2 * HEAD_DIM
    vcb = OFF_VB // pw
    vec_spec = pl.BlockSpec((1, HEAD_DIM), lambda b, h, i: (0, 0))
    return pl.pallas_call(
        functools.partial(_diff_kernel, seq=seq, tk=tk, lam_init=lam_init, online=online),
        grid=(batch, DIFF_HEADS, nqb),
        in_specs=[
            pl.BlockSpec((tq, pw), lambda b, h, i: (b * nqb + i, h)),
            pl.BlockSpec((seq, pw), lambda b, h, i: (b, h)),
            pl.BlockSpec((seq, pw), lambda b, h, i: (b, vcb + h)),
            vec_spec, vec_spec, vec_spec, vec_spec,
            pl.BlockSpec((1, pw), lambda b, h, i: (0, 0)),
        ],
        out_specs=pl.BlockSpec((tq, pw), lambda b, h, i: (b * nqb + i, h)),
        out_shape=jax.ShapeDtypeStruct((t, W_B_V), BF16),
        scratch_shapes=_attend_scratch(2 * tq, pw, online),
        compiler_params=_params("parallel", "parallel", "parallel"),
        name="diff_attention",
    )(qb, kb, proj, *lam_vecs, sub_gain)


NA_Q_ROWS = 4
NA_K_ROWS = NA_Q_ROWS + NA_WIN_R
NA_BLOCK_TYPES = 3


def _na_bias_table(rpb):
    col = jnp.arange(GRID_W)
    col_start = jnp.clip(col - NA_WIN_C // 2, 0, GRID_W - NA_WIN_C)
    col_mask = (col[None, :] >= col_start[:, None]) & (col[None, :] < col_start[:, None] + NA_WIN_C)
    dc = jnp.clip(col[None, :] - col[:, None], -(NA_WIN_C - 1), NA_WIN_C - 1) + NA_WIN_C - 1
    rpb_c = rpb.astype(F32)[:, :, dc] * LOG2E
    rr = jnp.arange(NA_Q_ROWS)
    q_off = jnp.stack([rr, rr + NA_WIN_R // 2, rr + NA_K_ROWS - NA_Q_ROWS])
    w_off = jnp.stack([0 * rr, rr, 0 * rr + NA_K_ROWS - NA_WIN_R])
    i = jnp.arange(NA_K_ROWS)
    dr = i[None, None, :] - q_off[:, :, None] + NA_WIN_R - 1
    row_ok = (i[None, None, :] >= w_off[:, :, None]) & (i[None, None, :] < w_off[:, :, None] + NA_WIN_R)
    tab = rpb_c[:, jnp.clip(dr, 0, 2 * NA_WIN_R - 2)]
    ok = row_ok[None, :, :, :, None, None] & col_mask[None, None, None, None]
    tab = jnp.where(ok, tab, NEG_BIG).transpose(0, 1, 2, 4, 3, 5)
    return tab.reshape(rpb.shape[0], NA_BLOCK_TYPES, NA_Q_ROWS * GRID_W, NA_K_ROWS * GRID_W)


def _na_kernel(q_ref, k_ref, v_ref, bias_ref, o_ref, *, rows, online):
    nrb = rows // NA_Q_ROWS
    tq = NA_Q_ROWS * GRID_W

    def row_block(rb, carry):
        ws = jnp.clip(rb * NA_Q_ROWS - NA_WIN_R // 2, 0, rows - NA_K_ROWS)
        start = pl.multiple_of(ws * GRID_W, GRID_W)
        qstart = pl.multiple_of(rb * tq, tq)
        kw = k_ref[pl.ds(start, NA_K_ROWS * GRID_W), :]
        vw = v_ref[pl.ds(start, NA_K_ROWS * GRID_W), :]
        block_type = jnp.where(rb == 0, 0, jnp.where(rb == nrb - 1, 2, 1))
        s = _qk(q_ref[pl.ds(qstart, tq), :], kw) + bias_ref[0, block_type]
        if online:
            s = s - jnp.max(s, axis=-1, keepdims=True)
        p = jnp.exp2(s)
        l = jnp.sum(p, axis=-1, keepdims=True)
        o = jnp.dot(p.astype(vw.dtype), vw, preferred_element_type=F32) / l
        o_ref[pl.ds(qstart, tq), :] = o.astype(o_ref.dtype)
        return carry

    lax.fori_loop(0, nrb, row_block, 0, unroll=2)


def _na_attention(qa, ka, proj, bias_tab, *, batch, seq, online):
    t = batch * seq
    rows = seq // GRID_W
    assert rows >= NA_K_ROWS and rows % (2 * NA_Q_ROWS) == 0
    vcb = OFF_VA // HEAD_DIM
    return pl.pallas_call(
        functools.partial(_na_kernel, rows=rows, online=online),
        grid=(batch, NA_HEADS),
        in_specs=[
            pl.BlockSpec((seq, HEAD_DIM), lambda b, h: (b, h)),
            pl.BlockSpec((seq, HEAD_DIM), lambda b, h: (b, h)),
            pl.BlockSpec((seq, HEAD_DIM), lambda b, h: (b, vcb + h)),
            pl.BlockSpec((1, NA_BLOCK_TYPES, NA_Q_ROWS * GRID_W, NA_K_ROWS * GRID_W), lambda b, h: (h, 0, 0, 0)),
        ],
        out_specs=pl.BlockSpec((seq, HEAD_DIM), lambda b, h: (b, h)),
        out_shape=jax.ShapeDtypeStruct((t, W_A), BF16),
        compiler_params=_params("parallel", "parallel"),
        name="na_attention",
    )(qa, ka, proj, bias_tab)


def _pick_softmax(plain_ok, attention, *operands):
    return lax.cond(plain_ok, functools.partial(attention, online=False), functools.partial(attention, online=True),
                    *operands)


def _merge_kernel(oa_ref, ob_ref, oc_ref, wa_ref, wb_ref, wc_ref, ga_ref, gb_ref, gc_ref, o_ref):
    def branch(o_r, w_r, g_r):
        y = jnp.dot(o_r[...], w_r[...], preferred_element_type=F32)
        return jax.nn.sigmoid(g_r[...].astype(F32)) * y

    merged = branch(oa_ref, wa_ref, ga_ref) + branch(ob_ref, wb_ref, gb_ref) + branch(oc_ref, wc_ref, gc_ref)
    o_ref[...] = merged.astype(o_ref.dtype)


def _merge(o_a, o_b, o_c, w_oa, w_ob, w_oc, layer, proj, tm=1024, tn=1024):
    t = o_a.shape[0]
    d = w_oa.shape[2]
    tm, tn = min(tm, t), min(tn, d)
    assert OFF_GA % tn == 0 and d % tn == 0
    gcb = OFF_GA // tn
    ncb = d // tn

    def act(width):
        return pl.BlockSpec((tm, width), lambda j, i: (i, 0))

    def wgt(width):
        return pl.BlockSpec((None, width, tn), lambda j, i: (layer, 0, j))

    def gate(branch):
        return pl.BlockSpec((tm, tn), lambda j, i: (i, gcb + branch * ncb + j))

    return pl.pallas_call(
        _merge_kernel,
        grid=(ncb, t // tm),
        in_specs=[act(W_A), act(W_B_V), act(W_C_Q), wgt(W_A), wgt(W_B_V), wgt(W_C_Q), gate(0), gate(1), gate(2)],
        out_specs=pl.BlockSpec((tm, tn), lambda j, i: (i, j)),
        out_shape=jax.ShapeDtypeStruct((t, d), BF16),
        compiler_params=_params("parallel", "parallel"),
        name="gated_merge",
    )(o_a, o_b, o_c, w_oa, w_ob, w_oc, proj, proj, proj)


def _row(v):
    return v.reshape(1, -1).astype(F32)


def _layer(x, l, batch, seq, w, tables):
    rope1, rope2 = tables
    lam_init = 0.8 - 0.6 * math.exp(-0.3 * l)
    dims = dict(batch=batch, seq=seq)
    h = _rmsnorm(x, w["norm_mix"][l])
    proj = _matmul(h, w["w_in"], l, name="proj_in")

    qa = _headnorm(proj, OFF_QA, W_A, w["qn_a"][l], seq, scale=QK_SCALE_LOG2)
    ka = _headnorm(proj, OFF_KA, W_A, w["kn_a"][l], seq)
    o_a = _pick_softmax(_score_bound_ok(w["qn_a"][l], w["kn_a"][l], w["rpb"][l]),
                        functools.partial(_na_attention, **dims), qa, ka, proj, w["na_bias"][l])

    qb = _headnorm(proj, OFF_QB, W_B_QK, w["qn_b"][l], seq, rope=rope1, scale=QK_SCALE_LOG2)
    kb = _headnorm(proj, OFF_KB, W_B_QK, w["kn_b"][l], seq, rope=rope1)
    lam_vecs = tuple(_row(w[n][l]) for n in ("lam_q1", "lam_k1", "lam_q2", "lam_k2"))
    o_b = _pick_softmax(_score_bound_ok(w["qn_b"][l], w["kn_b"][l]),
                        functools.partial(_diff_attention, lam_init=lam_init, **dims),
                        qb, kb, proj, lam_vecs, _row(w["subln_b"][l]))

    qc = _headnorm(proj, OFF_QC, W_C_Q, w["qn_c"][l], seq, rope=rope2, scale=QK_SCALE_LOG2)
    kc = _headnorm(proj, OFF_KC, W_C_KV, w["kn_c"][l], seq, rope=rope2)
    o_c = _pick_softmax(_score_bound_ok(w["qn_c"][l], w["kn_c"][l]),
                        functools.partial(_gqa_attention, **dims), qc, kc, proj)

    merged = _merge(o_a, o_b, o_c, w["w_oa"], w["w_ob"], w["w_oc"], l, proj)
    x, x_bf16, x_sumsq = _matmul(merged, w["w_out"], l, kind="residual_stats", residual=x, out_dtype=F32,
                                 name="proj_out")

    u = _matmul(x_bf16, w["w_up"], l, kind="rownorm_relu2", row_sumsq=x_sumsq, name="mlp_up")
    return _matmul(u, w["w_down"], l, kind="residual", residual=x, out_dtype=F32, name="mlp_down")


_MATMUL_WEIGHTS = ("w_in", "w_oa", "w_ob", "w_oc", "w_out", "w_up", "w_down")


def kernel(x_prompt, x_sample, norm_mix, w_in, qn_a, kn_a, rpb, qn_b, kn_b, lam_q1, lam_k1, lam_q2, lam_k2,
           subln_b, qn_c, kn_c, w_oa, w_ob, w_oc, w_out, norm_mlp, w_up, w_down):
    w = dict(norm_mix=norm_mix, w_in=w_in, qn_a=qn_a, kn_a=kn_a, rpb=rpb, qn_b=qn_b, kn_b=kn_b, lam_q1=lam_q1,
             lam_k1=lam_k1, lam_q2=lam_q2, lam_k2=lam_k2, subln_b=subln_b, qn_c=qn_c, kn_c=kn_c, w_oa=w_oa,
             w_ob=w_ob, w_oc=w_oc, w_out=w_out, norm_mlp=norm_mlp, w_up=w_up, w_down=w_down)
    w["w_up"] = norm_mlp.astype(F32)[:, :, None] * w_up
    for name in _MATMUL_WEIGHTS:
        w[name] = w[name].astype(BF16)
    depth = w_in.shape[0]
    w["na_bias"] = [_na_bias_table(rpb[l]) for l in range(depth)]

    def trunk(x):
        batch, seq, d = x.shape
        tables = _rope_tables(seq)
        y = x.reshape(batch * seq, d)
        for l in range(depth):
            y = _layer(y, l, batch, seq, w, tables)
        return y.reshape(batch, seq, d)

    return (trunk(x_prompt), trunk(x_sample))
```

```python
import functools
import math

import jax
import jax.numpy as jnp
from jax import lax
from jax.experimental import pallas as pl
from jax.experimental.pallas import tpu as pltpu

F32 = jnp.float32
BF16 = jnp.bfloat16

LANES = 128
HEAD_DIM = 128
GRID_W = 64
NA_HEADS = 12
NA_WIN_R = 8
NA_WIN_C = 16
DIFF_HEADS = 4
GQA_Q_HEADS = 12
GQA_KV_HEADS = 4
GQA_GROUP = GQA_Q_HEADS // GQA_KV_HEADS
ROPE_THETA = 10000.0
NORM_EPS = 1e-6

W_A = NA_HEADS * HEAD_DIM
W_B_QK = 2 * DIFF_HEADS * HEAD_DIM
W_B_V = DIFF_HEADS * 2 * HEAD_DIM
W_C_Q = GQA_Q_HEADS * HEAD_DIM
W_C_KV = GQA_KV_HEADS * HEAD_DIM

OFF_QA = 0
OFF_KA = OFF_QA + W_A
OFF_VA = OFF_KA + W_A
OFF_QB = OFF_VA + W_A
OFF_KB = OFF_QB + W_B_QK
OFF_VB = OFF_KB + W_B_QK
OFF_QC = OFF_VB + W_B_V
OFF_KC = OFF_QC + W_C_Q
OFF_VC = OFF_KC + W_C_KV
OFF_GA = OFF_VC + W_C_KV

V7X_VMEM_LIMIT_BYTES = 62 * 1024 * 1024

LOG2E = 1.4426950408889634
QK_SCALE_LOG2 = HEAD_DIM ** -0.5 * LOG2E
NEG_BIG = -1e30
PLAIN_SOFTMAX_LOG2_LIMIT = 64.0


def _params(*semantics):
    return pltpu.CompilerParams(dimension_semantics=semantics, vmem_limit_bytes=V7X_VMEM_LIMIT_BYTES)


def _rmsnorm_kernel(x_ref, g_ref, o_ref):
    x = x_ref[...]
    ms = jnp.mean(x * x, axis=-1, keepdims=True)
    o_ref[...] = (x * lax.rsqrt(ms + NORM_EPS) * g_ref[...]).astype(o_ref.dtype)


def _rmsnorm(x, gain, tm=256):
    t, d = x.shape
    tm = min(tm, t)
    return pl.pallas_call(
        _rmsnorm_kernel,
        grid=(t // tm,),
        in_specs=[pl.BlockSpec((tm, d), lambda i: (i, 0)), pl.BlockSpec((1, d), lambda i: (0, 0))],
        out_specs=pl.BlockSpec((tm, d), lambda i: (i, 0)),
        out_shape=jax.ShapeDtypeStruct((t, d), BF16),
        compiler_params=_params("parallel"),
        name="rmsnorm",
    )(x, gain.reshape(1, d).astype(F32))


def _matmul_kernel(*refs, kind, nk, kdim):
    a_ref, b_ref = refs[:2]
    prod = jnp.dot(a_ref[...], b_ref[...], preferred_element_type=F32)
    if kind == "residual":
        res_ref, o_ref = refs[2:]
        if nk == 1:
            o_ref[...] = prod + res_ref[...]
        else:
            o_ref[...] = prod + jnp.where(pl.program_id(2) == 0, res_ref[...], o_ref[...])
    elif kind == "residual_stats":
        res_ref, o_ref, ob_ref, ss_ref = refs[2:]
        x = prod + res_ref[...]
        o_ref[...] = x
        ob_ref[...] = x.astype(ob_ref.dtype)
        ss_ref[...] = _lane_fold(x * x)
    elif kind == "rownorm_relu2":
        ss_ref, o_ref = refs[2:]
        ms = jnp.sum(ss_ref[...], axis=-1, keepdims=True) * (1.0 / kdim)
        o_ref[...] = jnp.square(jnp.maximum(prod * lax.rsqrt(ms + NORM_EPS), 0.0)).astype(o_ref.dtype)
    else:
        (o_ref,) = refs[2:]
        o_ref[...] = prod.astype(o_ref.dtype)


def _matmul(a, w, layer, *, kind="plain", residual=None, row_sumsq=None, out_dtype=BF16,
            tm=1024, tn=1024, tk=4096, name="matmul"):
    m, kdim = a.shape
    n = w.shape[2]
    tm, tn, tk = min(tm, m), min(tn, n), min(tk, kdim)
    nk = kdim // tk
    tile = pl.BlockSpec((tm, tn), lambda i, j, k: (i, j))
    in_specs = [pl.BlockSpec((tm, tk), lambda i, j, k: (i, k)),
                pl.BlockSpec((None, tk, tn), lambda i, j, k: (layer, k, j))]
    args = [a, w]
    out_specs, out_shape = tile, jax.ShapeDtypeStruct((m, n), out_dtype)
    if kind in ("residual", "residual_stats"):
        in_specs.append(tile)
        args.append(residual)
    if kind == "residual_stats":
        out_specs = (tile, tile, pl.BlockSpec((tm, LANES), lambda i, j, k: (i, j)))
        out_shape = (out_shape, jax.ShapeDtypeStruct((m, n), BF16),
                     jax.ShapeDtypeStruct((m, n // tn * LANES), F32))
    if kind == "rownorm_relu2":
        in_specs.append(pl.BlockSpec((tm, row_sumsq.shape[1]), lambda i, j, k: (i, 0)))
        args.append(row_sumsq)
    assert nk == 1 or (kind == "residual" and out_dtype == F32)
    return pl.pallas_call(
        functools.partial(_matmul_kernel, kind=kind, nk=nk, kdim=kdim),
        grid=(m // tm, n // tn, nk),
        in_specs=in_specs,
        out_specs=out_specs,
        out_shape=out_shape,
        compiler_params=_params("parallel", "parallel", "arbitrary"),
        name=name,
    )(*args)


def _rotate_pairs(y, half):
    width = y.shape[-1]
    if 2 * half == width:
        return pltpu.roll(y, half, 1)
    lane = lax.broadcasted_iota(jnp.int32, y.shape, 1)
    first = (lane % (2 * half)) < half
    return jnp.where(first, pltpu.roll(y, width - half, 1), pltpu.roll(y, half, 1))


def _headnorm_kernel(*refs, heads, rot_half, scale):
    if rot_half:
        x_ref, g_ref, cos_ref, sin_ref, o_ref = refs
    else:
        x_ref, g_ref, o_ref = refs
    g = g_ref[...] * scale
    for h in range(heads):
        cols = slice(h * HEAD_DIM, (h + 1) * HEAD_DIM)
        x = x_ref[:, cols].astype(F32)
        ms = jnp.mean(x * x, axis=-1, keepdims=True)
        y = x * lax.rsqrt(ms + NORM_EPS) * g
        if rot_half:
            y = y * cos_ref[...] + _rotate_pairs(y, rot_half) * sin_ref[...]
        o_ref[:, cols] = y.astype(o_ref.dtype)


def _headnorm(proj, col_off, width, gain, seq, *, rope=None, scale=1.0, tm=1024):
    t = proj.shape[0]
    tm = min(tm, seq)
    heads = max(n for n in range(1, width // HEAD_DIM + 1)
                if width % (n * HEAD_DIM) == 0 and col_off % (n * HEAD_DIM) == 0)
    bw = heads * HEAD_DIM
    assert seq % tm == 0
    cb = col_off // bw
    in_specs = [pl.BlockSpec((tm, bw), lambda i, j: (i, cb + j)), pl.BlockSpec((1, HEAD_DIM), lambda i, j: (0, 0))]
    args = [proj, gain.reshape(1, HEAD_DIM).astype(F32)]
    rot_half = 0
    if rope is not None:
        cos, sin, rot_half = rope
        nsb = seq // tm
        tab_spec = pl.BlockSpec((tm, HEAD_DIM), lambda i, j: (i % nsb, 0))
        in_specs += [tab_spec, tab_spec]
        args += [cos, sin]
    return pl.pallas_call(
        functools.partial(_headnorm_kernel, heads=heads, rot_half=rot_half, scale=scale),
        grid=(t // tm, width // bw),
        in_specs=in_specs,
        out_specs=pl.BlockSpec((tm, bw), lambda i, j: (i, j)),
        out_shape=jax.ShapeDtypeStruct((t, width), BF16),
        compiler_params=_params("parallel", "parallel"),
        name="headnorm",
    )(*args)


def _rope_tables(seq):
    pos = jnp.arange(seq, dtype=jnp.int32)

    def angles(p, dr):
        inv = ROPE_THETA ** (-jnp.arange(0, dr, 2, dtype=F32) / dr)
        return p.astype(F32)[:, None] * inv[None, :]

    a1 = angles(pos, HEAD_DIM)
    cos1 = jnp.concatenate([jnp.cos(a1), jnp.cos(a1)], axis=-1)
    sin1 = jnp.concatenate([-jnp.sin(a1), jnp.sin(a1)], axis=-1)
    ar = angles(pos // GRID_W, HEAD_DIM // 2)
    ac = angles(pos % GRID_W, HEAD_DIM // 2)
    cos2 = jnp.concatenate([jnp.cos(ar), jnp.cos(ar), jnp.cos(ac), jnp.cos(ac)], axis=-1)
    sin2 = jnp.concatenate([-jnp.sin(ar), jnp.sin(ar), -jnp.sin(ac), jnp.sin(ac)], axis=-1)
    return (cos1, sin1, HEAD_DIM // 2), (cos2, sin2, HEAD_DIM // 4)


def _qk(q, k):
    return lax.dot_general(q, k, (((1,), (1,)), ((), ())), preferred_element_type=F32)


def _score_bound_ok(q_gain, k_gain, bias=None):
    bound = HEAD_DIM * QK_SCALE_LOG2 * jnp.max(jnp.abs(q_gain)) * jnp.max(jnp.abs(k_gain))
    if bias is not None:
        bound = bound + LOG2E * jnp.max(jnp.abs(bias))
    return bound <= PLAIN_SOFTMAX_LOG2_LIMIT


def _lane_fold(p):
    out = p[:, :LANES]
    for c in range(1, p.shape[1] // LANES):
        out = out + p[:, c * LANES:(c + 1) * LANES]
    return out


def _online_step(s, vj, m, l, acc):
    m_new = jnp.maximum(m, jnp.max(s, axis=-1, keepdims=True))
    alpha = jnp.exp2(m - m_new)
    p = jnp.exp2(s - m_new)
    l_new = alpha * l + jnp.sum(p, axis=-1, keepdims=True)
    acc_new = alpha * acc + jnp.dot(p.astype(vj.dtype), vj, preferred_element_type=F32)
    return m_new, l_new, acc_new


def _attend(score_fn, k_ref, v_ref, rows, dv, seq, tk, online, scratch):
    def chunk(j):
        start = pl.multiple_of(j * tk, tk)
        return k_ref[pl.ds(start, tk), :], v_ref[pl.ds(start, tk), :]

    if online:
        def body(j, carry):
            kj, vj = chunk(j)
            return _online_step(score_fn(kj), vj, *carry)

        init = (jnp.full((rows, 1), -jnp.inf, F32), jnp.zeros((rows, 1), F32), jnp.zeros((rows, dv), F32))
        _, l, acc = lax.fori_loop(0, seq // tk, body, init)
        return acc / l

    acc_ref, l_ref = scratch
    acc_ref[...] = jnp.zeros_like(acc_ref)
    l_ref[...] = jnp.zeros_like(l_ref)

    def body(j, carry):
        kj, vj = chunk(j)
        p = jnp.exp2(score_fn(kj))
        l_ref[...] += _lane_fold(p)
        acc_ref[...] = jnp.dot(p.astype(vj.dtype), vj, preferred_element_type=F32) + acc_ref[...]
        return carry

    lax.fori_loop(0, seq // tk, body, 0)
    return acc_ref[...] / jnp.sum(l_ref[...], axis=-1, keepdims=True)


def _attend_scratch(rows, dv, online):
    return [] if online else [pltpu.VMEM((rows, dv), F32), pltpu.VMEM((rows, HEAD_DIM), F32)]


def _gqa_kernel(q_ref, k_ref, v_ref, o_ref, *scratch, seq, tk, online):
    tq = q_ref.shape[0]
    q = jnp.concatenate([q_ref[:, g * HEAD_DIM:(g + 1) * HEAD_DIM] for g in range(GQA_GROUP)], axis=0)
    o = _attend(lambda kj: _qk(q, kj), k_ref, v_ref, GQA_GROUP * tq, HEAD_DIM, seq, tk, online, scratch)
    for g in range(GQA_GROUP):
        o_ref[:, g * HEAD_DIM:(g + 1) * HEAD_DIM] = o[g * tq:(g + 1) * tq].astype(o_ref.dtype)


def _transpose_kernel(x_ref, o_ref):
    o_ref[...] = x_ref[...].T


def _transposed_columns(x, col_off, width, tm=1024):
    t = x.shape[0]
    tm = min(tm, t)
    assert col_off % width == 0 and t % tm == 0
    cb = col_off // width
    return pl.pallas_call(
        _transpose_kernel,
        grid=(t // tm,),
        in_specs=[pl.BlockSpec((tm, width), lambda i: (i, cb))],
        out_specs=pl.BlockSpec((width, tm), lambda i: (0, i)),
        out_shape=jax.ShapeDtypeStruct((width, t), x.dtype),
        compiler_params=_params("parallel"),
        name="transpose_cols",
    )(x)


def _gqa_plain_kernel(q_ref, k_ref, vt_ref, o_ref, acc_ref, l_ref, *, seq, tk):
    tq = q_ref.shape[0]
    m = GQA_GROUP * tq
    q = jnp.concatenate([q_ref[:, g * HEAD_DIM:(g + 1) * HEAD_DIM] for g in range(GQA_GROUP)], axis=0)
    acc_ref[...] = jnp.zeros_like(acc_ref)
    l_ref[...] = jnp.zeros_like(l_ref)

    def body(j, carry):
        start = pl.multiple_of(j * tk, tk)
        pt = jnp.exp2(_qk(k_ref[pl.ds(start, tk), :], q))
        l_ref[...] += pt.reshape(tk // 8, 8, m).sum(axis=0)
        acc_ref[...] = jnp.dot(vt_ref[:, pl.ds(start, tk)], pt.astype(BF16), preferred_element_type=F32) + acc_ref[...]
        return carry

    lax.fori_loop(0, seq // tk, body, 0)
    o = (acc_ref[...] / jnp.sum(l_ref[...], axis=0, keepdims=True)).T
    for g in range(GQA_GROUP):
        o_ref[:, g * HEAD_DIM:(g + 1) * HEAD_DIM] = o[g * tq:(g + 1) * tq].astype(o_ref.dtype)


def _gqa_attention(qc, kc, proj, *, batch, seq, online, tq=512, tk=1024):
    t = batch * seq
    tq, tk = min(tq, seq), min(tk, seq)
    nqb = seq // tq
    gw = GQA_GROUP * HEAD_DIM
    vcb = OFF_VC // HEAD_DIM
    if not online:
        v_t = _transposed_columns(proj, OFF_VC, W_C_KV)
        return pl.pallas_call(
            functools.partial(_gqa_plain_kernel, seq=seq, tk=tk),
            grid=(batch, GQA_KV_HEADS, nqb),
            in_specs=[
                pl.BlockSpec((tq, gw), lambda b, n, i: (b * nqb + i, n)),
                pl.BlockSpec((seq, HEAD_DIM), lambda b, n, i: (b, n)),
                pl.BlockSpec((HEAD_DIM, seq), lambda b, n, i: (n, b)),
            ],
            out_specs=pl.BlockSpec((tq, gw), lambda b, n, i: (b * nqb + i, n)),
            out_shape=jax.ShapeDtypeStruct((t, W_C_Q), BF16),
            scratch_shapes=[pltpu.VMEM((HEAD_DIM, GQA_GROUP * tq), F32), pltpu.VMEM((8, GQA_GROUP * tq), F32)],
            compiler_params=_params("parallel", "parallel", "parallel"),
            name="gqa_attention",
        )(qc, kc, v_t)
    return pl.pallas_call(
        functools.partial(_gqa_kernel, seq=seq, tk=tk, online=online),
        grid=(batch, GQA_KV_HEADS, nqb),
        in_specs=[
            pl.BlockSpec((tq, gw), lambda b, n, i: (b * nqb + i, n)),
            pl.BlockSpec((seq, HEAD_DIM), lambda b, n, i: (b, n)),
            pl.BlockSpec((seq, HEAD_DIM), lambda b, n, i: (b, vcb + n)),
        ],
        out_specs=pl.BlockSpec((tq, gw), lambda b, n, i: (b * nqb + i, n)),
        out_shape=jax.ShapeDtypeStruct((t, W_C_Q), BF16),
        scratch_shapes=_attend_scratch(GQA_GROUP * tq, HEAD_DIM, online),
        compiler_params=_params("parallel", "parallel", "parallel"),
        name="gqa_attention",
    )(qc, kc, proj)


def _diff_kernel(q_ref, k_ref, v_ref, lq1_ref, lk1_ref, lq2_ref, lk2_ref, sg_ref, o_ref, *scratch,
                 seq, tk, lam_init, online):
    tq = q_ref.shape[0]
    dv = 2 * HEAD_DIM
    q1 = q_ref[:, :HEAD_DIM]
    q2 = q_ref[:, HEAD_DIM:]

    def scores(kj):
        return jnp.concatenate([_qk(q1, kj[:, :HEAD_DIM]), _qk(q2, kj[:, HEAD_DIM:])], axis=0)

    o12 = _attend(scores, k_ref, v_ref, 2 * tq, dv, seq, tk, online, scratch)
    lam = (jnp.exp(jnp.sum(lq1_ref[...] * lk1_ref[...], axis=-1, keepdims=True))
           - jnp.exp(jnp.sum(lq2_ref[...] * lk2_ref[...], axis=-1, keepdims=True)) + lam_init)
    o = o12[:tq] - lam * o12[tq:]
    ms = jnp.mean(o * o, axis=-1, keepdims=True)
    o = o * lax.rsqrt(ms + NORM_EPS) * sg_ref[...] * (1.0 - lam_init)
    o_ref[...] = o.astype(o_ref.dtype)


def _diff_attention(qb, kb, proj, lam_vecs, sub_gain, *, lam_init, batch, seq, online, tq=512, tk=1024):
    t = batch * seq
    tq, tk = min(tq, seq), min(tk, seq)
    nqb = seq // tq
    pw = 2 * HEAD_DIM
    vcb = OFF_VB // pw
    vec_spec = pl.BlockSpec((1, HEAD_DIM), lambda b, h, i: (0, 0))
    return pl.pallas_call(
        functools.partial(_diff_kernel, seq=seq, tk=tk, lam_init=lam_init, online=online),
        grid=(batch, DIFF_HEADS, nqb),
        in_specs=[
            pl.BlockSpec((tq, pw), lambda b, h, i: (b * nqb + i, h)),
            pl.BlockSpec((seq, pw), lambda b, h, i: (b, h)),
            pl.BlockSpec((seq, pw), lambda b, h, i: (b, vcb + h)),
            vec_spec, vec_spec, vec_spec, vec_spec,
            pl.BlockSpec((1, pw), lambda b, h, i: (0, 0)),
        ],
        out_specs=pl.BlockSpec((tq, pw), lambda b, h, i: (b * nqb + i, h)),
        out_shape=jax.ShapeDtypeStruct((t, W_B_V), BF16),
        scratch_shapes=_attend_scratch(2 * tq, pw, online),
        compiler_params=_params("parallel", "parallel", "parallel"),
        name="diff_attention",
    )(qb, kb, proj, *lam_vecs, sub_gain)


NA_Q_ROWS = 4
NA_K_ROWS = NA_Q_ROWS + NA_WIN_R
NA_BLOCK_TYPES = 3


def _na_bias_table(rpb):
    col = jnp.arange(GRID_W)
    col_start = jnp.clip(col - NA_WIN_C // 2, 0, GRID_W - NA_WIN_C)
    col_mask = (col[None, :] >= col_start[:, None]) & (col[None, :] < col_start[:, None] + NA_WIN_C)
    dc = jnp.clip(col[None, :] - col[:, None], -(NA_WIN_C - 1), NA_WIN_C - 1) + NA_WIN_C - 1
    rpb_c = rpb.astype(F32)[:, :, dc] * LOG2E
    rr = jnp.arange(NA_Q_ROWS)
    q_off = jnp.stack([rr, rr + NA_WIN_R // 2, rr + NA_K_ROWS - NA_Q_ROWS])
    w_off = jnp.stack([0 * rr, rr, 0 * rr + NA_K_ROWS - NA_WIN_R])
    i = jnp.arange(NA_K_ROWS)
    dr = i[None, None, :] - q_off[:, :, None] + NA_WIN_R - 1
    row_ok = (i[None, None, :] >= w_off[:, :, None]) & (i[None, None, :] < w_off[:, :, None] + NA_WIN_R)
    tab = rpb_c[:, jnp.clip(dr, 0, 2 * NA_WIN_R - 2)]
    ok = row_ok[None, :, :, :, None, None] & col_mask[None, None, None, None]
    tab = jnp.where(ok, tab, NEG_BIG).transpose(0, 1, 2, 4, 3, 5)
    return tab.reshape(rpb.shape[0], NA_BLOCK_TYPES, NA_Q_ROWS * GRID_W, NA_K_ROWS * GRID_W)


def _na_kernel(q_ref, k_ref, v_ref, bias_ref, o_ref, *, rows, online):
    nrb = rows // NA_Q_ROWS
    tq = NA_Q_ROWS * GRID_W

    def row_block(rb, carry):
        ws = jnp.clip(rb * NA_Q_ROWS - NA_WIN_R // 2, 0, rows - NA_K_ROWS)
        start = pl.multiple_of(ws * GRID_W, GRID_W)
        qstart = pl.multiple_of(rb * tq, tq)
        kw = k_ref[pl.ds(start, NA_K_ROWS * GRID_W), :]
        vw = v_ref[pl.ds(start, NA_K_ROWS * GRID_W), :]
        block_type = jnp.where(rb == 0, 0, jnp.where(rb == nrb - 1, 2, 1))
        s = _qk(q_ref[pl.ds(qstart, tq), :], kw) + bias_ref[0, block_type]
        if online:
            s = s - jnp.max(s, axis=-1, keepdims=True)
        p = jnp.exp2(s)
        l = jnp.sum(p, axis=-1, keepdims=True)
        o = jnp.dot(p.astype(vw.dtype), vw, preferred_element_type=F32) / l
        o_ref[pl.ds(qstart, tq), :] = o.astype(o_ref.dtype)
        return carry

    lax.fori_loop(0, nrb, row_block, 0, unroll=2)


def _na_attention(qa, ka, proj, bias_tab, *, batch, seq, online):
    t = batch * seq
    rows = seq // GRID_W
    assert rows >= NA_K_ROWS and rows % (2 * NA_Q_ROWS) == 0
    vcb = OFF_VA // HEAD_DIM
    return pl.pallas_call(
        functools.partial(_na_kernel, rows=rows, online=online),
        grid=(batch, NA_HEADS),
        in_specs=[
            pl.BlockSpec((seq, HEAD_DIM), lambda b, h: (b, h)),
            pl.BlockSpec((seq, HEAD_DIM), lambda b, h: (b, h)),
            pl.BlockSpec((seq, HEAD_DIM), lambda b, h: (b, vcb + h)),
            pl.BlockSpec((1, NA_BLOCK_TYPES, NA_Q_ROWS * GRID_W, NA_K_ROWS * GRID_W), lambda b, h: (h, 0, 0, 0)),
        ],
        out_specs=pl.BlockSpec((seq, HEAD_DIM), lambda b, h: (b, h)),
        out_shape=jax.ShapeDtypeStruct((t, W_A), BF16),
        compiler_params=_params("parallel", "parallel"),
        name="na_attention",
    )(qa, ka, proj, bias_tab)


def _pick_softmax(plain_ok, attention, *operands):
    return lax.cond(plain_ok, functools.partial(attention, online=False), functools.partial(attention, online=True),
                    *operands)


def _merge_kernel(oa_ref, ob_ref, oc_ref, wa_ref, wb_ref, wc_ref, ga_ref, gb_ref, gc_ref, o_ref):
    def branch(o_r, w_r, g_r):
        y = jnp.dot(o_r[...], w_r[...], preferred_element_type=F32)
        return jax.nn.sigmoid(g_r[...].astype(F32)) * y

    merged = branch(oa_ref, wa_ref, ga_ref) + branch(ob_ref, wb_ref, gb_ref) + branch(oc_ref, wc_ref, gc_ref)
    o_ref[...] = merged.astype(o_ref.dtype)


def _merge(o_a, o_b, o_c, w_oa, w_ob, w_oc, layer, proj, tm=1024, tn=1024):
    t = o_a.shape[0]
    d = w_oa.shape[2]
    tm, tn = min(tm, t), min(tn, d)
    assert OFF_GA % tn == 0 and d % tn == 0
    gcb = OFF_GA // tn
    ncb = d // tn

    def act(width):
        return pl.BlockSpec((tm, width), lambda j, i: (i, 0))

    def wgt(width):
        return pl.BlockSpec((None, width, tn), lambda j, i: (layer, 0, j))

    def gate(branch):
        return pl.BlockSpec((tm, tn), lambda j, i: (i, gcb + branch * ncb + j))

    return pl.pallas_call(
        _merge_kernel,
        grid=(ncb, t // tm),
        in_specs=[act(W_A), act(W_B_V), act(W_C_Q), wgt(W_A), wgt(W_B_V), wgt(W_C_Q), gate(0), gate(1), gate(2)],
        out_specs=pl.BlockSpec((tm, tn), lambda j, i: (i, j)),
        out_shape=jax.ShapeDtypeStruct((t, d), BF16),
        compiler_params=_params("parallel", "parallel"),
        name="gated_merge",
    )(o_a, o_b, o_c, w_oa, w_ob, w_oc, proj, proj, proj)


def _row(v):
    return v.reshape(1, -1).astype(F32)


def _layer(x, l, batch, seq, w, tables):
    rope1, rope2 = tables
    lam_init = 0.8 - 0.6 * math.exp(-0.3 * l)
    dims = dict(batch=batch, seq=seq)
    h = _rmsnorm(x, w["norm_mix"][l])
    proj = _matmul(h, w["w_in"], l, name="proj_in")

    qa = _headnorm(proj, OFF_QA, W_A, w["qn_a"][l], seq, scale=QK_SCALE_LOG2)
    ka = _headnorm(proj, OFF_KA, W_A, w["kn_a"][l], seq)
    o_a = _pick_softmax(_score_bound_ok(w["qn_a"][l], w["kn_a"][l], w["rpb"][l]),
                        functools.partial(_na_attention, **dims), qa, ka, proj, w["na_bias"][l])

    qb = _headnorm(proj, OFF_QB, W_B_QK, w["qn_b"][l], seq, rope=rope1, scale=QK_SCALE_LOG2)
    kb = _headnorm(proj, OFF_KB, W_B_QK, w["kn_b"][l], seq, rope=rope1)
    lam_vecs = tuple(_row(w[n][l]) for n in ("lam_q1", "lam_k1", "lam_q2", "lam_k2"))
    o_b = _pick_softmax(_score_bound_ok(w["qn_b"][l], w["kn_b"][l]),
                        functools.partial(_diff_attention, lam_init=lam_init, **dims),
                        qb, kb, proj, lam_vecs, _row(w["subln_b"][l]))

    qc = _headnorm(proj, OFF_QC, W_C_Q, w["qn_c"][l], seq, rope=rope2, scale=QK_SCALE_LOG2)
    kc = _headnorm(proj, OFF_KC, W_C_KV, w["kn_c"][l], seq, rope=rope2)
    o_c = _pick_softmax(_score_bound_ok(w["qn_c"][l], w["kn_c"][l]),
                        functools.partial(_gqa_attention, **dims), qc, kc, proj)

    merged = _merge(o_a, o_b, o_c, w["w_oa"], w["w_ob"], w["w_oc"], l, proj)
    x, x_bf16, x_sumsq = _matmul(merged, w["w_out"], l, kind="residual_stats", residual=x, out_dtype=F32,
                                 name="proj_out")

    u = _matmul(x_bf16, w["w_up"], l, kind="rownorm_relu2", row_sumsq=x_sumsq, name="mlp_up")
    return _matmul(u, w["w_down"], l, kind="residual", residual=x, out_dtype=F32, name="mlp_down")


_MATMUL_WEIGHTS = ("w_in", "w_oa", "w_ob", "w_oc", "w_out", "w_up", "w_down")


def kernel(x_prompt, x_sample, norm_mix, w_in, qn_a, kn_a, rpb, qn_b, kn_b, lam_q1, lam_k1, lam_q2, lam_k2,
           subln_b, qn_c, kn_c, w_oa, w_ob, w_oc, w_out, norm_mlp, w_up, w_down):
    w = dict(norm_mix=norm_mix, w_in=w_in, qn_a=qn_a, kn_a=kn_a, rpb=rpb, qn_b=qn_b, kn_b=kn_b, lam_q1=lam_q1,
             lam_k1=lam_k1, lam_q2=lam_q2, lam_k2=lam_k2, subln_b=subln_b, qn_c=qn_c, kn_c=kn_c, w_oa=w_oa,
             w_ob=w_ob, w_oc=w_oc, w_out=w_out, norm_mlp=norm_mlp, w_up=w_up, w_down=w_down)
    w["w_up"] = norm_mlp.astype(F32)[:, :, None] * w_up
    for name in _MATMUL_WEIGHTS:
        w[name] = w[name].astype(BF16)
    depth = w_in.shape[0]
    w["na_bias"] = [_na_bias_table(rpb[l]) for l in range(depth)]

    def trunk(x):
        batch, seq, d = x.shape
        tables = _rope_tables(seq)
        y = x.reshape(batch * seq, d)
        for l in range(depth):
            y = _layer(y, l, batch, seq, w, tables)
        return y.reshape(batch, seq, d)

    return (trunk(x_prompt), trunk(x_sample))
```

```python
import functools
import math

import jax
import jax.numpy as jnp
from jax import lax
from jax.experimental import pallas as pl
from jax.experimental.pallas import tpu as pltpu

F32 = jnp.float32
BF16 = jnp.bfloat16

LANES = 128
HEAD_DIM = 128
GRID_W = 64
NA_HEADS = 12
NA_WIN_R = 8
NA_WIN_C = 16
DIFF_HEADS = 4
GQA_Q_HEADS = 12
GQA_KV_HEADS = 4
GQA_GROUP = GQA_Q_HEADS // GQA_KV_HEADS
ROPE_THETA = 10000.0
NORM_EPS = 1e-6

W_A = NA_HEADS * HEAD_DIM
W_B_QK = 2 * DIFF_HEADS * HEAD_DIM
W_B_V = DIFF_HEADS * 2 * HEAD_DIM
W_C_Q = GQA_Q_HEADS * HEAD_DIM
W_C_KV = GQA_KV_HEADS * HEAD_DIM

OFF_QA = 0
OFF_KA = OFF_QA + W_A
OFF_VA = OFF_KA + W_A
OFF_QB = OFF_VA + W_A
OFF_KB = OFF_QB + W_B_QK
OFF_VB = OFF_KB + W_B_QK
OFF_QC = OFF_VB + W_B_V
OFF_KC = OFF_QC + W_C_Q
OFF_VC = OFF_KC + W_C_KV
OFF_GA = OFF_VC + W_C_KV
REST_OFF = OFF_VA
V_OFF_A = OFF_VA - REST_OFF
V_OFF_B = OFF_VB - REST_OFF
V_OFF_C = OFF_VC - REST_OFF
G_OFF = OFF_GA - REST_OFF

V7X_VMEM_LIMIT_BYTES = 62 * 1024 * 1024

LOG2E = 1.4426950408889634
QK_SCALE_LOG2 = HEAD_DIM ** -0.5 * LOG2E
NEG_BIG = -1e30
PLAIN_SOFTMAX_LOG2_LIMIT = 64.0


def _params(*semantics):
    return pltpu.CompilerParams(dimension_semantics=semantics, vmem_limit_bytes=V7X_VMEM_LIMIT_BYTES)


def _rmsnorm_kernel(x_ref, g_ref, o_ref):
    x = x_ref[...]
    ms = jnp.mean(x * x, axis=-1, keepdims=True)
    o_ref[...] = (x * lax.rsqrt(ms + NORM_EPS) * g_ref[...]).astype(o_ref.dtype)


def _rmsnorm(x, gain, tm=256):
    t, d = x.shape
    tm = min(tm, t)
    return pl.pallas_call(
        _rmsnorm_kernel,
        grid=(t // tm,),
        in_specs=[pl.BlockSpec((tm, d), lambda i: (i, 0)), pl.BlockSpec((1, d), lambda i: (0, 0))],
        out_specs=pl.BlockSpec((tm, d), lambda i: (i, 0)),
        out_shape=jax.ShapeDtypeStruct((t, d), BF16),
        compiler_params=_params("parallel"),
        name="rmsnorm",
    )(x, gain.reshape(1, d).astype(F32))


def _store_normed_heads(x, g_ref, o_ref, rope_refs=None, rot_half=0):
    for h in range(x.shape[1] // HEAD_DIM):
        cols = slice(h * HEAD_DIM, (h + 1) * HEAD_DIM)
        xh = x[:, cols]
        ms = jnp.mean(xh * xh, axis=-1, keepdims=True)
        y = xh * lax.rsqrt(ms + NORM_EPS) * g_ref[:, cols]
        if rope_refs is not None:
            cos_ref, sin_ref = rope_refs
            y = y * cos_ref[...] + _rotate_pairs(y, rot_half) * sin_ref[...]
        o_ref[:, cols] = y.astype(o_ref.dtype)


def _matmul_kernel(*refs, kind, nk, kdim):
    a_ref, b_ref = refs[:2]
    prod = jnp.dot(a_ref[...], b_ref[...], preferred_element_type=F32)
    if kind == "residual":
        res_ref, o_ref = refs[2:]
        if nk == 1:
            o_ref[...] = prod + res_ref[...]
        else:
            o_ref[...] = prod + jnp.where(pl.program_id(2) == 0, res_ref[...], o_ref[...])
    elif kind == "residual_stats":
        res_ref, o_ref, ob_ref, ss_ref = refs[2:]
        x = prod + res_ref[...]
        o_ref[...] = x
        ob_ref[...] = x.astype(ob_ref.dtype)
        ss_ref[...] = _lane_fold(x * x)
    elif kind == "rownorm_relu2":
        ss_ref, o_ref = refs[2:]
        ms = jnp.sum(ss_ref[...], axis=-1, keepdims=True) * (1.0 / kdim)
        o_ref[...] = jnp.square(jnp.maximum(prod * lax.rsqrt(ms + NORM_EPS), 0.0)).astype(o_ref.dtype)
    elif kind == "headnorm":
        g_ref, o_ref = refs[2:]
        _store_normed_heads(prod, g_ref, o_ref)
    else:
        (o_ref,) = refs[2:]
        o_ref[...] = prod.astype(o_ref.dtype)


def _matmul(a, w, layer, *, kind="plain", residual=None, row_sumsq=None, head_gain=None,
            n=None, col_block0=0, out_dtype=BF16, tm=1024, tn=1024, tk=4096, name="matmul"):
    m, kdim = a.shape
    n = w.shape[2] if n is None else n
    tm, tn, tk = min(tm, m), min(tn, n), min(tk, kdim)
    nk = kdim // tk
    tile = pl.BlockSpec((tm, tn), lambda i, j, k: (i, j))
    in_specs = [pl.BlockSpec((tm, tk), lambda i, j, k: (i, k)),
                pl.BlockSpec((None, tk, tn), lambda i, j, k: (layer, k, col_block0 + j))]
    args = [a, w]
    out_specs, out_shape = tile, jax.ShapeDtypeStruct((m, n), out_dtype)
    if kind == "headnorm":
        in_specs.append(pl.BlockSpec((1, tn), lambda i, j, k: (0, j)))
        args.append(head_gain)
    if kind in ("residual", "residual_stats"):
        in_specs.append(tile)
        args.append(residual)
    if kind == "residual_stats":
        out_specs = (tile, tile, pl.BlockSpec((tm, LANES), lambda i, j, k: (i, j)))
        out_shape = (out_shape, jax.ShapeDtypeStruct((m, n), BF16),
                     jax.ShapeDtypeStruct((m, n // tn * LANES), F32))
    if kind == "rownorm_relu2":
        in_specs.append(pl.BlockSpec((tm, row_sumsq.shape[1]), lambda i, j, k: (i, 0)))
        args.append(row_sumsq)
    assert nk == 1 or (kind == "residual" and out_dtype == F32)
    return pl.pallas_call(
        functools.partial(_matmul_kernel, kind=kind, nk=nk, kdim=kdim),
        grid=(m // tm, n // tn, nk),
        in_specs=in_specs,
        out_specs=out_specs,
        out_shape=out_shape,
        compiler_params=_params("parallel", "parallel", "arbitrary"),
        name=name,
    )(*args)


def _rotate_pairs(y, half):
    width = y.shape[-1]
    if 2 * half == width:
        return pltpu.roll(y, half, 1)
    lane = lax.broadcasted_iota(jnp.int32, y.shape, 1)
    first = (lane % (2 * half)) < half
    return jnp.where(first, pltpu.roll(y, width - half, 1), pltpu.roll(y, half, 1))


def _headnorm_kernel(x_ref, g_ref, cos_ref, sin_ref, o_ref, *, rot_half):
    _store_normed_heads(x_ref[...].astype(F32), g_ref, o_ref, (cos_ref, sin_ref), rot_half)


def _headnorm(x, col_off, head_gain, rope, seq, tm=1024, bw=512):
    t = x.shape[0]
    width = head_gain.shape[1]
    tm = min(tm, seq)
    assert col_off % bw == 0 and width % bw == 0 and seq % tm == 0
    cb = col_off // bw
    cos, sin, rot_half = rope
    nsb = seq // tm
    tab_spec = pl.BlockSpec((tm, HEAD_DIM), lambda i, j: (i % nsb, 0))
    return pl.pallas_call(
        functools.partial(_headnorm_kernel, rot_half=rot_half),
        grid=(t // tm, width // bw),
        in_specs=[pl.BlockSpec((tm, bw), lambda i, j: (i, cb + j)), pl.BlockSpec((1, bw), lambda i, j: (0, j)),
                  tab_spec, tab_spec],
        out_specs=pl.BlockSpec((tm, bw), lambda i, j: (i, j)),
        out_shape=jax.ShapeDtypeStruct((t, width), BF16),
        compiler_params=_params("parallel", "parallel"),
        name="headnorm",
    )(x, head_gain, cos, sin)


def _rope_tables(seq):
    pos = jnp.arange(seq, dtype=jnp.int32)

    def angles(p, dr):
        inv = ROPE_THETA ** (-jnp.arange(0, dr, 2, dtype=F32) / dr)
        return p.astype(F32)[:, None] * inv[None, :]

    a1 = angles(pos, HEAD_DIM)
    cos1 = jnp.concatenate([jnp.cos(a1), jnp.cos(a1)], axis=-1)
    sin1 = jnp.concatenate([-jnp.sin(a1), jnp.sin(a1)], axis=-1)
    ar = angles(pos // GRID_W, HEAD_DIM // 2)
    ac = angles(pos % GRID_W, HEAD_DIM // 2)
    cos2 = jnp.concatenate([jnp.cos(ar), jnp.cos(ar), jnp.cos(ac), jnp.cos(ac)], axis=-1)
    sin2 = jnp.concatenate([-jnp.sin(ar), jnp.sin(ar), -jnp.sin(ac), jnp.sin(ac)], axis=-1)
    return (cos1, sin1, HEAD_DIM // 2), (cos2, sin2, HEAD_DIM // 4)


def _qk(q, k):
    return lax.dot_general(q, k, (((1,), (1,)), ((), ())), preferred_element_type=F32)


def _score_bound_ok(q_gain, k_gain, bias=None):
    bound = HEAD_DIM * QK_SCALE_LOG2 * jnp.max(jnp.abs(q_gain)) * jnp.max(jnp.abs(k_gain))
    if bias is not None:
        bound = bound + LOG2E * jnp.max(jnp.abs(bias))
    return bound <= PLAIN_SOFTMAX_LOG2_LIMIT


def _lane_fold(p):
    out = p[:, :LANES]
    for c in range(1, p.shape[1] // LANES):
        out = out + p[:, c * LANES:(c + 1) * LANES]
    return out


def _online_step(s, vj, m, l, acc):
    m_new = jnp.maximum(m, jnp.max(s, axis=-1, keepdims=True))
    alpha = jnp.exp2(m - m_new)
    p = jnp.exp2(s - m_new)
    l_new = alpha * l + jnp.sum(p, axis=-1, keepdims=True)
    acc_new = alpha * acc + jnp.dot(p.astype(vj.dtype), vj, preferred_element_type=F32)
    return m_new, l_new, acc_new


def _attend(score_fn, k_ref, v_ref, rows, dv, seq, tk, online, scratch):
    def chunk(j):
        start = pl.multiple_of(j * tk, tk)
        return k_ref[pl.ds(start, tk), :], v_ref[pl.ds(start, tk), :]

    if online:
        def body(j, carry):
            kj, vj = chunk(j)
            return _online_step(score_fn(kj), vj, *carry)

        init = (jnp.full((rows, 1), -jnp.inf, F32), jnp.zeros((rows, 1), F32), jnp.zeros((rows, dv), F32))
        _, l, acc = lax.fori_loop(0, seq // tk, body, init)
        return acc / l

    acc_ref, l_ref = scratch
    acc_ref[...] = jnp.zeros_like(acc_ref)
    l_ref[...] = jnp.zeros_like(l_ref)

    def body(j, carry):
        kj, vj = chunk(j)
        p = jnp.exp2(score_fn(kj))
        l_ref[...] += _lane_fold(p)
        acc_ref[...] = jnp.dot(p.astype(vj.dtype), vj, preferred_element_type=F32) + acc_ref[...]
        return carry

    lax.fori_loop(0, seq // tk, body, 0)
    return acc_ref[...] / jnp.sum(l_ref[...], axis=-1, keepdims=True)


def _attend_scratch(rows, dv, online):
    return [] if online else [pltpu.VMEM((rows, dv), F32), pltpu.VMEM((rows, HEAD_DIM), F32)]


def _gqa_kernel(q_ref, k_ref, v_ref, o_ref, *scratch, seq, tk, online):
    tq = q_ref.shape[0]
    q = jnp.concatenate([q_ref[:, g * HEAD_DIM:(g + 1) * HEAD_DIM] for g in range(GQA_GROUP)], axis=0)
    o = _attend(lambda kj: _qk(q, kj), k_ref, v_ref, GQA_GROUP * tq, HEAD_DIM, seq, tk, online, scratch)
    for g in range(GQA_GROUP):
        o_ref[:, g * HEAD_DIM:(g + 1) * HEAD_DIM] = o[g * tq:(g + 1) * tq].astype(o_ref.dtype)


def _transpose_kernel(x_ref, o_ref):
    o_ref[...] = x_ref[...].T


def _transposed_columns(x, col_off, width, tm=1024):
    t = x.shape[0]
    tm = min(tm, t)
    assert col_off % width == 0 and t % tm == 0
    cb = col_off // width
    return pl.pallas_call(
        _transpose_kernel,
        grid=(t // tm,),
        in_specs=[pl.BlockSpec((tm, width), lambda i: (i, cb))],
        out_specs=pl.BlockSpec((width, tm), lambda i: (0, i)),
        out_shape=jax.ShapeDtypeStruct((width, t), x.dtype),
        compiler_params=_params("parallel"),
        name="transpose_cols",
    )(x)


def _gqa_plain_kernel(q_ref, k_ref, vt_ref, o_ref, acc_ref, l_ref, *, seq, tk):
    tq = q_ref.shape[0]
    m = GQA_GROUP * tq
    q = jnp.concatenate([q_ref[:, g * HEAD_DIM:(g + 1) * HEAD_DIM] for g in range(GQA_GROUP)], axis=0)
    acc_ref[...] = jnp.zeros_like(acc_ref)
    l_ref[...] = jnp.zeros_like(l_ref)

    def body(j, carry):
        start = pl.multiple_of(j * tk, tk)
        pt = jnp.exp2(_qk(k_ref[pl.ds(start, tk), :], q))
        l_ref[...] += pt.reshape(tk // 8, 8, m).sum(axis=0)
        acc_ref[...] = jnp.dot(vt_ref[:, pl.ds(start, tk)], pt.astype(BF16), preferred_element_type=F32) + acc_ref[...]
        return carry

    lax.fori_loop(0, seq // tk, body, 0)
    o = (acc_ref[...] / jnp.sum(l_ref[...], axis=0, keepdims=True)).T
    for g in range(GQA_GROUP):
        o_ref[:, g * HEAD_DIM:(g + 1) * HEAD_DIM] = o[g * tq:(g + 1) * tq].astype(o_ref.dtype)


def _gqa_attention(qk, v_all, *, batch, seq, online, tq=512, tk=1024):
    t = batch * seq
    tq, tk = min(tq, seq), min(tk, seq)
    nqb = seq // tq
    gw = GQA_GROUP * HEAD_DIM
    kcb = W_C_Q // HEAD_DIM
    vcb = V_OFF_C // HEAD_DIM
    q_spec = pl.BlockSpec((tq, gw), lambda b, n, i: (b * nqb + i, n))
    k_spec = pl.BlockSpec((seq, HEAD_DIM), lambda b, n, i: (b, kcb + n))
    if not online:
        v_t = _transposed_columns(v_all, V_OFF_C, W_C_KV)
        return pl.pallas_call(
            functools.partial(_gqa_plain_kernel, seq=seq, tk=tk),
            grid=(batch, GQA_KV_HEADS, nqb),
            in_specs=[q_spec, k_spec, pl.BlockSpec((HEAD_DIM, seq), lambda b, n, i: (n, b))],
            out_specs=q_spec,
            out_shape=jax.ShapeDtypeStruct((t, W_C_Q), BF16),
            scratch_shapes=[pltpu.VMEM((HEAD_DIM, GQA_GROUP * tq), F32), pltpu.VMEM((8, GQA_GROUP * tq), F32)],
            compiler_params=_params("parallel", "parallel", "parallel"),
            name="gqa_attention",
        )(qk, qk, v_t)
    return pl.pallas_call(
        functools.partial(_gqa_kernel, seq=seq, tk=tk, online=online),
        grid=(batch, GQA_KV_HEADS, nqb),
        in_specs=[q_spec, k_spec, pl.BlockSpec((seq, HEAD_DIM), lambda b, n, i: (b, vcb + n))],
        out_specs=q_spec,
        out_shape=jax.ShapeDtypeStruct((t, W_C_Q), BF16),
        scratch_shapes=_attend_scratch(GQA_GROUP * tq, HEAD_DIM, online),
        compiler_params=_params("parallel", "parallel", "parallel"),
        name="gqa_attention",
    )(qk, qk, v_all)


def _diff_kernel(q_ref, k_ref, v_ref, lq1_ref, lk1_ref, lq2_ref, lk2_ref, sg_ref, o_ref, *scratch,
                 seq, tk, lam_init, online):
    tq = q_ref.shape[0]
    dv = 2 * HEAD_DIM
    q1 = q_ref[:, :HEAD_DIM]
    q2 = q_ref[:, HEAD_DIM:]

    def scores(kj):
        return jnp.concatenate([_qk(q1, kj[:, :HEAD_DIM]), _qk(q2, kj[:, HEAD_DIM:])], axis=0)

    o12 = _attend(scores, k_ref, v_ref, 2 * tq, dv, seq, tk, online, scratch)
    lam = (jnp.exp(jnp.sum(lq1_ref[...] * lk1_ref[...], axis=-1, keepdims=True))
           - jnp.exp(jnp.sum(lq2_ref[...] * lk2_ref[...], axis=-1, keepdims=True)) + lam_init)
    o = o12[:tq] - lam * o12[tq:]
    ms = jnp.mean(o * o, axis=-1, keepdims=True)
    o = o * lax.rsqrt(ms + NORM_EPS) * sg_ref[...] * (1.0 - lam_init)
    o_ref[...] = o.astype(o_ref.dtype)


def _diff_attention(qk, v_all, lam_vecs, sub_gain, *, lam_init, batch, seq, online, tq=512, tk=1024):
    t = batch * seq
    tq, tk = min(tq, seq), min(tk, seq)
    nqb = seq // tq
    pw = 2 * HEAD_DIM
    kcb = W_B_QK // pw
    vcb = V_OFF_B // pw
    vec_spec = pl.BlockSpec((1, HEAD_DIM), lambda b, h, i: (0, 0))
    return pl.pallas_call(
        functools.partial(_diff_kernel, seq=seq, tk=tk, lam_init=lam_init, online=online),
        grid=(batch, DIFF_HEADS, nqb),
        in_specs=[
            pl.BlockSpec((tq, pw), lambda b, h, i: (b * nqb + i, h)),
            pl.BlockSpec((seq, pw), lambda b, h, i: (b, kcb + h)),
            pl.BlockSpec((seq, pw), lambda b, h, i: (b, vcb + h)),
            vec_spec, vec_spec, vec_spec, vec_spec,
            pl.BlockSpec((1, pw), lambda b, h, i: (0, 0)),
        ],
        out_specs=pl.BlockSpec((tq, pw), lambda b, h, i: (b * nqb + i, h)),
        out_shape=jax.ShapeDtypeStruct((t, W_B_V), BF16),
        scratch_shapes=_attend_scratch(2 * tq, pw, online),
        compiler_params=_params("parallel", "parallel", "parallel"),
        name="diff_attention",
    )(qk, qk, v_all, *lam_vecs, sub_gain)


NA_Q_ROWS = 4
NA_K_ROWS = NA_Q_ROWS + NA_WIN_R
NA_BLOCK_TYPES = 3


def _na_bias_table(rpb):
    col = jnp.arange(GRID_W)
    col_start = jnp.clip(col - NA_WIN_C // 2, 0, GRID_W - NA_WIN_C)
    col_mask = (col[None, :] >= col_start[:, None]) & (col[None, :] < col_start[:, None] + NA_WIN_C)
    dc = jnp.clip(col[None, :] - col[:, None], -(NA_WIN_C - 1), NA_WIN_C - 1) + NA_WIN_C - 1
    rpb_c = rpb.astype(F32)[:, :, dc] * LOG2E
    rr = jnp.arange(NA_Q_ROWS)
    q_off = jnp.stack([rr, rr + NA_WIN_R // 2, rr + NA_K_ROWS - NA_Q_ROWS])
    w_off = jnp.stack([0 * rr, rr, 0 * rr + NA_K_ROWS - NA_WIN_R])
    i = jnp.arange(NA_K_ROWS)
    dr = i[None, None, :] - q_off[:, :, None] + NA_WIN_R - 1
    row_ok = (i[None, None, :] >= w_off[:, :, None]) & (i[None, None, :] < w_off[:, :, None] + NA_WIN_R)
    tab = rpb_c[:, jnp.clip(dr, 0, 2 * NA_WIN_R - 2)]
    ok = row_ok[None, :, :, :, None, None] & col_mask[None, None, None, None]
    tab = jnp.where(ok, tab, NEG_BIG).transpose(0, 1, 2, 4, 3, 5)
    return tab.reshape(rpb.shape[0], NA_BLOCK_TYPES, NA_Q_ROWS * GRID_W, NA_K_ROWS * GRID_W)


def _na_kernel(q_ref, k_ref, v_ref, bias_ref, o_ref, *, rows, online):
    nrb = rows // NA_Q_ROWS
    tq = NA_Q_ROWS * GRID_W

    def row_block(rb, carry):
        ws = jnp.clip(rb * NA_Q_ROWS - NA_WIN_R // 2, 0, rows - NA_K_ROWS)
        start = pl.multiple_of(ws * GRID_W, GRID_W)
        qstart = pl.multiple_of(rb * tq, tq)
        kw = k_ref[pl.ds(start, NA_K_ROWS * GRID_W), :]
        vw = v_ref[pl.ds(start, NA_K_ROWS * GRID_W), :]
        block_type = jnp.where(rb == 0, 0, jnp.where(rb == nrb - 1, 2, 1))
        s = _qk(q_ref[pl.ds(qstart, tq), :], kw) + bias_ref[0, block_type]
        if online:
            s = s - jnp.max(s, axis=-1, keepdims=True)
        p = jnp.exp2(s)
        l = jnp.sum(p, axis=-1, keepdims=True)
        o = jnp.dot(p.astype(vw.dtype), vw, preferred_element_type=F32) / l
        o_ref[pl.ds(qstart, tq), :] = o.astype(o_ref.dtype)
        return carry

    lax.fori_loop(0, nrb, row_block, 0, unroll=2)


def _na_attention(qk, v_all, bias_tab, *, batch, seq, online):
    t = batch * seq
    rows = seq // GRID_W
    assert rows >= NA_K_ROWS and rows % (2 * NA_Q_ROWS) == 0
    kcb = W_A // HEAD_DIM
    vcb = V_OFF_A // HEAD_DIM
    return pl.pallas_call(
        functools.partial(_na_kernel, rows=rows, online=online),
        grid=(batch, NA_HEADS),
        in_specs=[
            pl.BlockSpec((seq, HEAD_DIM), lambda b, h: (b, h)),
            pl.BlockSpec((seq, HEAD_DIM), lambda b, h: (b, kcb + h)),
            pl.BlockSpec((seq, HEAD_DIM), lambda b, h: (b, vcb + h)),
            pl.BlockSpec((1, NA_BLOCK_TYPES, NA_Q_ROWS * GRID_W, NA_K_ROWS * GRID_W), lambda b, h: (h, 0, 0, 0)),
        ],
        out_specs=pl.BlockSpec((seq, HEAD_DIM), lambda b, h: (b, h)),
        out_shape=jax.ShapeDtypeStruct((t, W_A), BF16),
        compiler_params=_params("parallel", "parallel"),
        name="na_attention",
    )(qk, qk, v_all, bias_tab)


def _pick_softmax(plain_ok, attention, *operands):
    return lax.cond(plain_ok, functools.partial(attention, online=False), functools.partial(attention, online=True),
                    *operands)


def _merge_kernel(oa_ref, ob_ref, oc_ref, wa_ref, wb_ref, wc_ref, ga_ref, gb_ref, gc_ref, o_ref):
    def branch(o_r, w_r, g_r):
        y = jnp.dot(o_r[...], w_r[...], preferred_element_type=F32)
        return jax.nn.sigmoid(g_r[...].astype(F32)) * y

    merged = branch(oa_ref, wa_ref, ga_ref) + branch(ob_ref, wb_ref, gb_ref) + branch(oc_ref, wc_ref, gc_ref)
    o_ref[...] = merged.astype(o_ref.dtype)


def _merge(o_a, o_b, o_c, w_oa, w_ob, w_oc, layer, rest, tm=1024, tn=1024):
    t = o_a.shape[0]
    d = w_oa.shape[2]
    tm, tn = min(tm, t), min(tn, d)
    assert d % tn == 0 and G_OFF % tn == 0
    gcb = G_OFF // tn
    ncb = d // tn

    def act(width):
        return pl.BlockSpec((tm, width), lambda j, i: (i, 0))

    def wgt(width):
        return pl.BlockSpec((None, width, tn), lambda j, i: (layer, 0, j))

    def gate(branch):
        return pl.BlockSpec((tm, tn), lambda j, i: (i, gcb + branch * ncb + j))

    return pl.pallas_call(
        _merge_kernel,
        grid=(ncb, t // tm),
        in_specs=[act(W_A), act(W_B_V), act(W_C_Q), wgt(W_A), wgt(W_B_V), wgt(W_C_Q), gate(0), gate(1), gate(2)],
        out_specs=pl.BlockSpec((tm, tn), lambda j, i: (i, j)),
        out_shape=jax.ShapeDtypeStruct((t, d), BF16),
        compiler_params=_params("parallel", "parallel"),
        name="gated_merge",
    )(o_a, o_b, o_c, w_oa, w_ob, w_oc, rest, rest, rest)


def _row(v):
    return v.reshape(1, -1).astype(F32)


def _input_projection(h, w, l, seq, tables):
    rope1, rope2 = tables
    w_in = w["w_in"]
    tn = 1024
    assert REST_OFF == 2 * W_A and REST_OFF % tn == 0

    def gains(q_gain, k_gain, q_heads, k_heads):
        return jnp.concatenate([jnp.tile(q_gain.astype(F32) * QK_SCALE_LOG2, q_heads),
                                jnp.tile(k_gain.astype(F32), k_heads)]).reshape(1, -1)

    qk_a = _matmul(h, w_in, l, kind="headnorm", n=REST_OFF, tn=tn,
                   head_gain=gains(w["qn_a"][l], w["kn_a"][l], NA_HEADS, NA_HEADS), name="proj_qk_a")
    rest = _matmul(h, w_in, l, n=w_in.shape[2] - REST_OFF, tn=tn, col_block0=REST_OFF // tn, name="proj_rest")
    qk_b = _headnorm(rest, OFF_QB - REST_OFF, gains(w["qn_b"][l], w["kn_b"][l], 2 * DIFF_HEADS, 2 * DIFF_HEADS),
                     rope1, seq)
    qk_c = _headnorm(rest, OFF_QC - REST_OFF, gains(w["qn_c"][l], w["kn_c"][l], GQA_Q_HEADS, GQA_KV_HEADS),
                     rope2, seq)
    return qk_a, qk_b, qk_c, rest


def _layer(x, l, batch, seq, w, tables):
    lam_init = 0.8 - 0.6 * math.exp(-0.3 * l)
    dims = dict(batch=batch, seq=seq)
    h = _rmsnorm(x, w["norm_mix"][l])
    qk_a, qk_b, qk_c, rest = _input_projection(h, w, l, seq, tables)

    o_a = _pick_softmax(_score_bound_ok(w["qn_a"][l], w["kn_a"][l], w["rpb"][l]),
                        functools.partial(_na_attention, **dims), qk_a, rest, w["na_bias"][l])
    lam_vecs = tuple(_row(w[n][l]) for n in ("lam_q1", "lam_k1", "lam_q2", "lam_k2"))
    o_b = _pick_softmax(_score_bound_ok(w["qn_b"][l], w["kn_b"][l]),
                        functools.partial(_diff_attention, lam_init=lam_init, **dims),
                        qk_b, rest, lam_vecs, _row(w["subln_b"][l]))
    o_c = _pick_softmax(_score_bound_ok(w["qn_c"][l], w["kn_c"][l]),
                        functools.partial(_gqa_attention, **dims), qk_c, rest)

    merged = _merge(o_a, o_b, o_c, w["w_oa"], w["w_ob"], w["w_oc"], l, rest)
    x, x_bf16, x_sumsq = _matmul(merged, w["w_out"], l, kind="residual_stats", residual=x, out_dtype=F32,
                                 name="proj_out")

    u = _matmul(x_bf16, w["w_up"], l, kind="rownorm_relu2", row_sumsq=x_sumsq, name="mlp_up")
    return _matmul(u, w["w_down"], l, kind="residual", residual=x, out_dtype=F32, name="mlp_down")


_MATMUL_WEIGHTS = ("w_in", "w_oa", "w_ob", "w_oc", "w_out", "w_up", "w_down")


def kernel(x_prompt, x_sample, norm_mix, w_in, qn_a, kn_a, rpb, qn_b, kn_b, lam_q1, lam_k1, lam_q2, lam_k2,
           subln_b, qn_c, kn_c, w_oa, w_ob, w_oc, w_out, norm_mlp, w_up, w_down):
    w = dict(norm_mix=norm_mix, w_in=w_in, qn_a=qn_a, kn_a=kn_a, rpb=rpb, qn_b=qn_b, kn_b=kn_b, lam_q1=lam_q1,
             lam_k1=lam_k1, lam_q2=lam_q2, lam_k2=lam_k2, subln_b=subln_b, qn_c=qn_c, kn_c=kn_c, w_oa=w_oa,
             w_ob=w_ob, w_oc=w_oc, w_out=w_out, norm_mlp=norm_mlp, w_up=w_up, w_down=w_down)
    w["w_up"] = norm_mlp.astype(F32)[:, :, None] * w_up
    for name in _MATMUL_WEIGHTS:
        w[name] = w[name].astype(BF16)
    depth = w_in.shape[0]
    w["na_bias"] = [_na_bias_table(rpb[l]) for l in range(depth)]

    def trunk(x):
        batch, seq, d = x.shape
        tables = _rope_tables(seq)
        y = x.reshape(batch * seq, d)
        for l in range(depth):
            y = _layer(y, l, batch, seq, w, tables)
        return y.reshape(batch, seq, d)

    return (trunk(x_prompt), trunk(x_sample))
```

```python
import functools
import math

import jax
import jax.numpy as jnp
from jax import lax
from jax.experimental import pallas as pl
from jax.experimental.pallas import tpu as pltpu

F32 = jnp.float32
BF16 = jnp.bfloat16

LANES = 128
HEAD_DIM = 128
GRID_W = 64
NA_HEADS = 12
NA_WIN_R = 8
NA_WIN_C = 16
DIFF_HEADS = 4
GQA_Q_HEADS = 12
GQA_KV_HEADS = 4
GQA_GROUP = GQA_Q_HEADS // GQA_KV_HEADS
ROPE_THETA = 10000.0
NORM_EPS = 1e-6

W_A = NA_HEADS * HEAD_DIM
W_B_QK = 2 * DIFF_HEADS * HEAD_DIM
W_B_V = DIFF_HEADS * 2 * HEAD_DIM
W_C_Q = GQA_Q_HEADS * HEAD_DIM
W_C_KV = GQA_KV_HEADS * HEAD_DIM

OFF_QA = 0
OFF_KA = OFF_QA + W_A
OFF_VA = OFF_KA + W_A
OFF_QB = OFF_VA + W_A
OFF_KB = OFF_QB + W_B_QK
OFF_VB = OFF_KB + W_B_QK
OFF_QC = OFF_VB + W_B_V
OFF_KC = OFF_QC + W_C_Q
OFF_VC = OFF_KC + W_C_KV
OFF_GA = OFF_VC + W_C_KV
REST_OFF = OFF_VA
V_OFF_A = OFF_VA - REST_OFF
V_OFF_B = OFF_VB - REST_OFF
V_OFF_C = OFF_VC - REST_OFF
G_OFF = OFF_GA - REST_OFF

V7X_VMEM_LIMIT_BYTES = 62 * 1024 * 1024

LOG2E = 1.4426950408889634
QK_SCALE_LOG2 = HEAD_DIM ** -0.5 * LOG2E
NEG_BIG = -1e30
PLAIN_SOFTMAX_LOG2_LIMIT = 64.0


def _params(*semantics):
    return pltpu.CompilerParams(dimension_semantics=semantics, vmem_limit_bytes=V7X_VMEM_LIMIT_BYTES)


def _rmsnorm_kernel(x_ref, g_ref, o_ref):
    x = x_ref[...]
    ms = jnp.mean(x * x, axis=-1, keepdims=True)
    o_ref[...] = (x * lax.rsqrt(ms + NORM_EPS) * g_ref[...]).astype(o_ref.dtype)


def _rmsnorm(x, gain, tm=256):
    t, d = x.shape
    tm = min(tm, t)
    return pl.pallas_call(
        _rmsnorm_kernel,
        grid=(t // tm,),
        in_specs=[pl.BlockSpec((tm, d), lambda i: (i, 0)), pl.BlockSpec((1, d), lambda i: (0, 0))],
        out_specs=pl.BlockSpec((tm, d), lambda i: (i, 0)),
        out_shape=jax.ShapeDtypeStruct((t, d), BF16),
        compiler_params=_params("parallel"),
        name="rmsnorm",
    )(x, gain.reshape(1, d).astype(F32))


def _store_normed_heads(x, g_ref, o_ref, rope_refs=None, rot_half=0):
    for h in range(x.shape[1] // HEAD_DIM):
        cols = slice(h * HEAD_DIM, (h + 1) * HEAD_DIM)
        xh = x[:, cols]
        ms = jnp.mean(xh * xh, axis=-1, keepdims=True)
        y = xh * lax.rsqrt(ms + NORM_EPS) * g_ref[:, cols]
        if rope_refs is not None:
            cos_ref, sin_ref = rope_refs
            y = y * cos_ref[...] + _rotate_pairs(y, rot_half) * sin_ref[...]
        o_ref[:, cols] = y.astype(o_ref.dtype)


def _matmul_kernel(*refs, kind, nk, kdim):
    a_ref, b_ref = refs[:2]
    prod = jnp.dot(a_ref[...], b_ref[...], preferred_element_type=F32)
    if kind == "residual":
        res_ref, o_ref = refs[2:]
        if nk == 1:
            o_ref[...] = prod + res_ref[...]
        else:
            o_ref[...] = prod + jnp.where(pl.program_id(2) == 0, res_ref[...], o_ref[...])
    elif kind == "residual_stats":
        res_ref, o_ref, ob_ref, ss_ref = refs[2:]
        x = prod + res_ref[...]
        o_ref[...] = x
        ob_ref[...] = x.astype(ob_ref.dtype)
        ss_ref[...] = _lane_fold(x * x)
    elif kind == "rownorm_relu2":
        ss_ref, o_ref = refs[2:]
        ms = jnp.sum(ss_ref[...], axis=-1, keepdims=True) * (1.0 / kdim)
        o_ref[...] = jnp.square(jnp.maximum(prod * lax.rsqrt(ms + NORM_EPS), 0.0)).astype(o_ref.dtype)
    elif kind == "headnorm":
        g_ref, o_ref = refs[2:]
        _store_normed_heads(prod, g_ref, o_ref)
    else:
        (o_ref,) = refs[2:]
        o_ref[...] = prod.astype(o_ref.dtype)


def _matmul(a, w, layer, *, kind="plain", residual=None, row_sumsq=None, head_gain=None,
            n=None, col_block0=0, out_dtype=BF16, tm=1024, tn=1024, tk=4096, name="matmul"):
    m, kdim = a.shape
    n = w.shape[2] if n is None else n
    tm, tn, tk = min(tm, m), min(tn, n), min(tk, kdim)
    nk = kdim // tk
    tile = pl.BlockSpec((tm, tn), lambda i, j, k: (i, j))
    in_specs = [pl.BlockSpec((tm, tk), lambda i, j, k: (i, k)),
                pl.BlockSpec((None, tk, tn), lambda i, j, k: (layer, k, col_block0 + j))]
    args = [a, w]
    out_specs, out_shape = tile, jax.ShapeDtypeStruct((m, n), out_dtype)
    if kind == "headnorm":
        in_specs.append(pl.BlockSpec((1, tn), lambda i, j, k: (0, j)))
        args.append(head_gain)
    if kind in ("residual", "residual_stats"):
        in_specs.append(tile)
        args.append(residual)
    if kind == "residual_stats":
        out_specs = (tile, tile, pl.BlockSpec((tm, LANES), lambda i, j, k: (i, j)))
        out_shape = (out_shape, jax.ShapeDtypeStruct((m, n), BF16),
                     jax.ShapeDtypeStruct((m, n // tn * LANES), F32))
    if kind == "rownorm_relu2":
        in_specs.append(pl.BlockSpec((tm, row_sumsq.shape[1]), lambda i, j, k: (i, 0)))
        args.append(row_sumsq)
    assert nk == 1 or (kind == "residual" and out_dtype == F32)
    return pl.pallas_call(
        functools.partial(_matmul_kernel, kind=kind, nk=nk, kdim=kdim),
        grid=(m // tm, n // tn, nk),
        in_specs=in_specs,
        out_specs=out_specs,
        out_shape=out_shape,
        compiler_params=_params("parallel", "parallel", "arbitrary"),
        name=name,
    )(*args)


def _rotate_pairs(y, half):
    width = y.shape[-1]
    if 2 * half == width:
        return pltpu.roll(y, half, 1)
    lane = lax.broadcasted_iota(jnp.int32, y.shape, 1)
    first = (lane % (2 * half)) < half
    return jnp.where(first, pltpu.roll(y, width - half, 1), pltpu.roll(y, half, 1))


def _headnorm_kernel(x_ref, g_ref, cos_ref, sin_ref, o_ref, *, rot_half):
    _store_normed_heads(x_ref[...].astype(F32), g_ref, o_ref, (cos_ref, sin_ref), rot_half)


def _headnorm(x, col_off, head_gain, rope, seq, tm=1024, bw=512):
    t = x.shape[0]
    width = head_gain.shape[1]
    tm = min(tm, seq)
    assert col_off % bw == 0 and width % bw == 0 and seq % tm == 0
    cb = col_off // bw
    cos, sin, rot_half = rope
    nsb = seq // tm
    tab_spec = pl.BlockSpec((tm, HEAD_DIM), lambda i, j: (i % nsb, 0))
    return pl.pallas_call(
        functools.partial(_headnorm_kernel, rot_half=rot_half),
        grid=(t // tm, width // bw),
        in_specs=[pl.BlockSpec((tm, bw), lambda i, j: (i, cb + j)), pl.BlockSpec((1, bw), lambda i, j: (0, j)),
                  tab_spec, tab_spec],
        out_specs=pl.BlockSpec((tm, bw), lambda i, j: (i, j)),
        out_shape=jax.ShapeDtypeStruct((t, width), BF16),
        compiler_params=_params("parallel", "parallel"),
        name="headnorm",
    )(x, head_gain, cos, sin)


def _rope_tables(seq):
    pos = jnp.arange(seq, dtype=jnp.int32)

    def angles(p, dr):
        inv = ROPE_THETA ** (-jnp.arange(0, dr, 2, dtype=F32) / dr)
        return p.astype(F32)[:, None] * inv[None, :]

    a1 = angles(pos, HEAD_DIM)
    cos1 = jnp.concatenate([jnp.cos(a1), jnp.cos(a1)], axis=-1)
    sin1 = jnp.concatenate([-jnp.sin(a1), jnp.sin(a1)], axis=-1)
    ar = angles(pos // GRID_W, HEAD_DIM // 2)
    ac = angles(pos % GRID_W, HEAD_DIM // 2)
    cos2 = jnp.concatenate([jnp.cos(ar), jnp.cos(ar), jnp.cos(ac), jnp.cos(ac)], axis=-1)
    sin2 = jnp.concatenate([-jnp.sin(ar), jnp.sin(ar), -jnp.sin(ac), jnp.sin(ac)], axis=-1)
    return (cos1, sin1, HEAD_DIM // 2), (cos2, sin2, HEAD_DIM // 4)


def _qk(q, k):
    return lax.dot_general(q, k, (((1,), (1,)), ((), ())), preferred_element_type=F32)


def _score_bound_ok(q_gain, k_gain, bias=None):
    bound = HEAD_DIM * QK_SCALE_LOG2 * jnp.max(jnp.abs(q_gain)) * jnp.max(jnp.abs(k_gain))
    if bias is not None:
        bound = bound + LOG2E * jnp.max(jnp.abs(bias))
    return bound <= PLAIN_SOFTMAX_LOG2_LIMIT


def _lane_fold(p):
    out = p[:, :LANES]
    for c in range(1, p.shape[1] // LANES):
        out = out + p[:, c * LANES:(c + 1) * LANES]
    return out


def _online_step(s, vj, m, l, acc):
    m_new = jnp.maximum(m, jnp.max(s, axis=-1, keepdims=True))
    alpha = jnp.exp2(m - m_new)
    p = jnp.exp2(s - m_new)
    l_new = alpha * l + jnp.sum(p, axis=-1, keepdims=True)
    acc_new = alpha * acc + jnp.dot(p.astype(vj.dtype), vj, preferred_element_type=F32)
    return m_new, l_new, acc_new


def _attend(score_fn, k_ref, v_ref, rows, dv, seq, tk, online, scratch):
    def chunk(j):
        start = pl.multiple_of(j * tk, tk)
        return k_ref[pl.ds(start, tk), :], v_ref[pl.ds(start, tk), :]

    if online:
        def body(j, carry):
            kj, vj = chunk(j)
            return _online_step(score_fn(kj), vj, *carry)

        init = (jnp.full((rows, 1), -jnp.inf, F32), jnp.zeros((rows, 1), F32), jnp.zeros((rows, dv), F32))
        _, l, acc = lax.fori_loop(0, seq // tk, body, init)
        return acc / l

    acc_ref, l_ref = scratch
    acc_ref[...] = jnp.zeros_like(acc_ref)
    l_ref[...] = jnp.zeros_like(l_ref)

    def body(j, carry):
        kj, vj = chunk(j)
        p = jnp.exp2(score_fn(kj))
        l_ref[...] += _lane_fold(p)
        acc_ref[...] = jnp.dot(p.astype(vj.dtype), vj, preferred_element_type=F32) + acc_ref[...]
        return carry

    lax.fori_loop(0, seq // tk, body, 0)
    return acc_ref[...] / jnp.sum(l_ref[...], axis=-1, keepdims=True)


def _attend_scratch(rows, dv):
    return [pltpu.VMEM((rows, dv), F32), pltpu.VMEM((rows, LANES), F32)]


def _by_softmax_kind(plain_ref, run):
    @pl.when(plain_ref[0] != 0)
    def _():
        run(False)

    @pl.when(plain_ref[0] == 0)
    def _():
        run(True)


def _plain_flag(plain_ok):
    return plain_ok.astype(jnp.int32).reshape(1)


_FLAG_SPEC = pl.BlockSpec(memory_space=pltpu.SMEM)


def _transpose_kernel(x_ref, o_ref):
    o_ref[...] = x_ref[...].T


def _transposed_columns(x, col_off, width, tm=1024):
    t = x.shape[0]
    tm = min(tm, t)
    assert col_off % width == 0 and t % tm == 0
    cb = col_off // width
    return pl.pallas_call(
        _transpose_kernel,
        grid=(t // tm,),
        in_specs=[pl.BlockSpec((tm, width), lambda i: (i, cb))],
        out_specs=pl.BlockSpec((width, tm), lambda i: (0, i)),
        out_shape=jax.ShapeDtypeStruct((width, t), x.dtype),
        compiler_params=_params("parallel"),
        name="transpose_cols",
    )(x)


def _gqa_kernel(plain_ref, q_ref, k_ref, v_ref, vt_ref, o_ref, acc_ref, l_ref, *, seq, tk):
    tq = q_ref.shape[0]
    m = GQA_GROUP * tq
    q = jnp.concatenate([q_ref[:, g * HEAD_DIM:(g + 1) * HEAD_DIM] for g in range(GQA_GROUP)], axis=0)

    def store(o):
        for g in range(GQA_GROUP):
            o_ref[:, g * HEAD_DIM:(g + 1) * HEAD_DIM] = o[g * tq:(g + 1) * tq].astype(o_ref.dtype)

    def run(online):
        if online:
            store(_attend(lambda kj: _qk(q, kj), k_ref, v_ref, m, HEAD_DIM, seq, tk, True, None))
            return
        acc_ref[...] = jnp.zeros_like(acc_ref)
        l_ref[...] = jnp.zeros_like(l_ref)

        def body(j, carry):
            start = pl.multiple_of(j * tk, tk)
            pt = jnp.exp2(_qk(k_ref[pl.ds(start, tk), :], q))
            l_ref[...] += pt.reshape(tk // 8, 8, m).sum(axis=0)
            acc_ref[...] = (jnp.dot(vt_ref[:, pl.ds(start, tk)], pt.astype(BF16), preferred_element_type=F32)
                            + acc_ref[...])
            return carry

        lax.fori_loop(0, seq // tk, body, 0)
        store((acc_ref[...] / jnp.sum(l_ref[...], axis=0, keepdims=True)).T)

    _by_softmax_kind(plain_ref, run)


def _gqa_attention(plain_ok, qk, v_all, *, batch, seq, tq=512, tk=1024):
    t = batch * seq
    tq, tk = min(tq, seq), min(tk, seq)
    nqb = seq // tq
    gw = GQA_GROUP * HEAD_DIM
    kcb = W_C_Q // HEAD_DIM
    vcb = V_OFF_C // HEAD_DIM
    q_spec = pl.BlockSpec((tq, gw), lambda b, n, i: (b * nqb + i, n))
    v_t = _transposed_columns(v_all, V_OFF_C, W_C_KV)
    return pl.pallas_call(
        functools.partial(_gqa_kernel, seq=seq, tk=tk),
        grid=(batch, GQA_KV_HEADS, nqb),
        in_specs=[_FLAG_SPEC, q_spec,
                  pl.BlockSpec((seq, HEAD_DIM), lambda b, n, i: (b, kcb + n)),
                  pl.BlockSpec((seq, HEAD_DIM), lambda b, n, i: (b, vcb + n)),
                  pl.BlockSpec((HEAD_DIM, seq), lambda b, n, i: (n, b))],
        out_specs=q_spec,
        out_shape=jax.ShapeDtypeStruct((t, W_C_Q), BF16),
        scratch_shapes=[pltpu.VMEM((HEAD_DIM, GQA_GROUP * tq), F32), pltpu.VMEM((8, GQA_GROUP * tq), F32)],
        compiler_params=_params("parallel", "parallel", "parallel"),
        name="gqa_attention",
    )(_plain_flag(plain_ok), qk, qk, v_all, v_t)


def _diff_kernel(plain_ref, q_ref, k_ref, v_ref, lq1_ref, lk1_ref, lq2_ref, lk2_ref, sg_ref, o_ref, *scratch,
                 seq, tk, lam_init):
    tq = q_ref.shape[0]
    dv = 2 * HEAD_DIM
    q1 = q_ref[:, :HEAD_DIM]
    q2 = q_ref[:, HEAD_DIM:]

    def scores(kj):
        return jnp.concatenate([_qk(q1, kj[:, :HEAD_DIM]), _qk(q2, kj[:, HEAD_DIM:])], axis=0)

    def run(online):
        o12 = _attend(scores, k_ref, v_ref, 2 * tq, dv, seq, tk, online, scratch)
        lam = (jnp.exp(jnp.sum(lq1_ref[...] * lk1_ref[...], axis=-1, keepdims=True))
               - jnp.exp(jnp.sum(lq2_ref[...] * lk2_ref[...], axis=-1, keepdims=True)) + lam_init)
        o = o12[:tq] - lam * o12[tq:]
        ms = jnp.mean(o * o, axis=-1, keepdims=True)
        o = o * lax.rsqrt(ms + NORM_EPS) * sg_ref[...] * (1.0 - lam_init)
        o_ref[...] = o.astype(o_ref.dtype)

    _by_softmax_kind(plain_ref, run)


def _diff_attention(plain_ok, qk, v_all, lam_vecs, sub_gain, *, lam_init, batch, seq, tq=512, tk=1024):
    t = batch * seq
    tq, tk = min(tq, seq), min(tk, seq)
    nqb = seq // tq
    pw = 2 * HEAD_DIM
    kcb = W_B_QK // pw
    vcb = V_OFF_B // pw
    vec_spec = pl.BlockSpec((1, HEAD_DIM), lambda b, h, i: (0, 0))
    return pl.pallas_call(
        functools.partial(_diff_kernel, seq=seq, tk=tk, lam_init=lam_init),
        grid=(batch, DIFF_HEADS, nqb),
        in_specs=[
            _FLAG_SPEC,
            pl.BlockSpec((tq, pw), lambda b, h, i: (b * nqb + i, h)),
            pl.BlockSpec((seq, pw), lambda b, h, i: (b, kcb + h)),
            pl.BlockSpec((seq, pw), lambda b, h, i: (b, vcb + h)),
            vec_spec, vec_spec, vec_spec, vec_spec,
            pl.BlockSpec((1, pw), lambda b, h, i: (0, 0)),
        ],
        out_specs=pl.BlockSpec((tq, pw), lambda b, h, i: (b * nqb + i, h)),
        out_shape=jax.ShapeDtypeStruct((t, W_B_V), BF16),
        scratch_shapes=_attend_scratch(2 * tq, pw),
        compiler_params=_params("parallel", "parallel", "parallel"),
        name="diff_attention",
    )(_plain_flag(plain_ok), qk, qk, v_all, *lam_vecs, sub_gain)


NA_Q_ROWS = 4
NA_K_ROWS = NA_Q_ROWS + NA_WIN_R
NA_BLOCK_TYPES = 3


def _na_bias_table(rpb):
    col = jnp.arange(GRID_W)
    col_start = jnp.clip(col - NA_WIN_C // 2, 0, GRID_W - NA_WIN_C)
    col_mask = (col[None, :] >= col_start[:, None]) & (col[None, :] < col_start[:, None] + NA_WIN_C)
    dc = jnp.clip(col[None, :] - col[:, None], -(NA_WIN_C - 1), NA_WIN_C - 1) + NA_WIN_C - 1
    rpb_c = rpb.astype(F32)[:, :, dc] * LOG2E
    by_offset = jnp.where(col_mask[None, None], rpb_c, NEG_BIG).transpose(0, 2, 1, 3)
    heads = rpb.shape[0]
    blocks = []
    for block_type in range(NA_BLOCK_TYPES):
        for rr in range(NA_Q_ROWS):
            q_off, w_off = ((rr, 0), (rr + NA_WIN_R // 2, rr), (rr + NA_K_ROWS - NA_Q_ROWS, NA_K_ROWS - NA_WIN_R))[block_type]
            lo = w_off - q_off + NA_WIN_R - 1
            own = by_offset[:, :, lo:lo + NA_WIN_R]
            own = jnp.pad(own, ((0, 0), (0, 0), (w_off, NA_K_ROWS - NA_WIN_R - w_off), (0, 0)),
                          constant_values=NEG_BIG)
            blocks.append(own.reshape(heads, GRID_W, NA_K_ROWS * GRID_W))
    return jnp.stack(blocks, axis=1).reshape(heads, NA_BLOCK_TYPES, NA_Q_ROWS * GRID_W, NA_K_ROWS * GRID_W)


def _na_kernel(plain_ref, q_ref, k_ref, v_ref, bias_ref, o_ref, *, rows):
    nrb = rows // NA_Q_ROWS
    tq = NA_Q_ROWS * GRID_W

    def run(online):
        def row_block(rb, carry):
            ws = jnp.clip(rb * NA_Q_ROWS - NA_WIN_R // 2, 0, rows - NA_K_ROWS)
            start = pl.multiple_of(ws * GRID_W, GRID_W)
            qstart = pl.multiple_of(rb * tq, tq)
            kw = k_ref[pl.ds(start, NA_K_ROWS * GRID_W), :]
            vw = v_ref[pl.ds(start, NA_K_ROWS * GRID_W), :]
            block_type = jnp.where(rb == 0, 0, jnp.where(rb == nrb - 1, 2, 1))
            s = _qk(q_ref[pl.ds(qstart, tq), :], kw) + bias_ref[0, block_type]
            if online:
                s = s - jnp.max(s, axis=-1, keepdims=True)
            p = jnp.exp2(s)
            l = jnp.sum(p, axis=-1, keepdims=True)
            o = jnp.dot(p.astype(vw.dtype), vw, preferred_element_type=F32) / l
            o_ref[pl.ds(qstart, tq), :] = o.astype(o_ref.dtype)
            return carry

        lax.fori_loop(0, nrb, row_block, 0, unroll=2)

    _by_softmax_kind(plain_ref, run)


def _na_attention(plain_ok, qk, v_all, bias_tab, *, batch, seq):
    t = batch * seq
    rows = seq // GRID_W
    assert rows >= NA_K_ROWS and rows % (2 * NA_Q_ROWS) == 0
    kcb = W_A // HEAD_DIM
    vcb = V_OFF_A // HEAD_DIM
    return pl.pallas_call(
        functools.partial(_na_kernel, rows=rows),
        grid=(batch, NA_HEADS),
        in_specs=[
            _FLAG_SPEC,
            pl.BlockSpec((seq, HEAD_DIM), lambda b, h: (b, h)),
            pl.BlockSpec((seq, HEAD_DIM), lambda b, h: (b, kcb + h)),
            pl.BlockSpec((seq, HEAD_DIM), lambda b, h: (b, vcb + h)),
            pl.BlockSpec((1, NA_BLOCK_TYPES, NA_Q_ROWS * GRID_W, NA_K_ROWS * GRID_W), lambda b, h: (h, 0, 0, 0)),
        ],
        out_specs=pl.BlockSpec((seq, HEAD_DIM), lambda b, h: (b, h)),
        out_shape=jax.ShapeDtypeStruct((t, W_A), BF16),
        compiler_params=_params("parallel", "parallel"),
        name="na_attention",
    )(_plain_flag(plain_ok), qk, qk, v_all, bias_tab)


def _merge_kernel(oa_ref, ob_ref, oc_ref, wa_ref, wb_ref, wc_ref, ga_ref, gb_ref, gc_ref, o_ref):
    def branch(o_r, w_r, g_r):
        y = jnp.dot(o_r[...], w_r[...], preferred_element_type=F32)
        return jax.nn.sigmoid(g_r[...].astype(F32)) * y

    merged = branch(oa_ref, wa_ref, ga_ref) + branch(ob_ref, wb_ref, gb_ref) + branch(oc_ref, wc_ref, gc_ref)
    o_ref[...] = merged.astype(o_ref.dtype)


def _merge(o_a, o_b, o_c, w_oa, w_ob, w_oc, layer, rest, tm=1024, tn=1024):
    t = o_a.shape[0]
    d = w_oa.shape[2]
    tm, tn = min(tm, t), min(tn, d)
    assert d % tn == 0 and G_OFF % tn == 0
    gcb = G_OFF // tn
    ncb = d // tn

    def act(width):
        return pl.BlockSpec((tm, width), lambda j, i: (i, 0))

    def wgt(width):
        return pl.BlockSpec((None, width, tn), lambda j, i: (layer, 0, j))

    def gate(branch):
        return pl.BlockSpec((tm, tn), lambda j, i: (i, gcb + branch * ncb + j))

    return pl.pallas_call(
        _merge_kernel,
        grid=(ncb, t // tm),
        in_specs=[act(W_A), act(W_B_V), act(W_C_Q), wgt(W_A), wgt(W_B_V), wgt(W_C_Q), gate(0), gate(1), gate(2)],
        out_specs=pl.BlockSpec((tm, tn), lambda j, i: (i, j)),
        out_shape=jax.ShapeDtypeStruct((t, d), BF16),
        compiler_params=_params("parallel", "parallel"),
        name="gated_merge",
    )(o_a, o_b, o_c, w_oa, w_ob, w_oc, rest, rest, rest)


def _row(v):
    return v.reshape(1, -1).astype(F32)


def _input_projection(h, w, l, seq, tables):
    rope1, rope2 = tables
    w_in = w["w_in"]
    tn = 1024
    assert REST_OFF == 2 * W_A and REST_OFF % tn == 0

    def gains(q_gain, k_gain, q_heads, k_heads):
        return jnp.concatenate([jnp.tile(q_gain.astype(F32) * QK_SCALE_LOG2, q_heads),
                                jnp.tile(k_gain.astype(F32), k_heads)]).reshape(1, -1)

    qk_a = _matmul(h, w_in, l, kind="headnorm", n=REST_OFF, tn=tn,
                   head_gain=gains(w["qn_a"][l], w["kn_a"][l], NA_HEADS, NA_HEADS), name="proj_qk_a")
    rest = _matmul(h, w_in, l, n=w_in.shape[2] - REST_OFF, tn=tn, col_block0=REST_OFF // tn, name="proj_rest")
    qk_b = _headnorm(rest, OFF_QB - REST_OFF, gains(w["qn_b"][l], w["kn_b"][l], 2 * DIFF_HEADS, 2 * DIFF_HEADS),
                     rope1, seq)
    qk_c = _headnorm(rest, OFF_QC - REST_OFF, gains(w["qn_c"][l], w["kn_c"][l], GQA_Q_HEADS, GQA_KV_HEADS),
                     rope2, seq)
    return qk_a, qk_b, qk_c, rest


def _layer(x, l, batch, seq, w, tables):
    lam_init = 0.8 - 0.6 * math.exp(-0.3 * l)
    dims = dict(batch=batch, seq=seq)
    h = _rmsnorm(x, w["norm_mix"][l])
    qk_a, qk_b, qk_c, rest = _input_projection(h, w, l, seq, tables)

    o_a = _na_attention(_score_bound_ok(w["qn_a"][l], w["kn_a"][l], w["rpb"][l]), qk_a, rest, w["na_bias"][l],
                        **dims)
    lam_vecs = tuple(_row(w[n][l]) for n in ("lam_q1", "lam_k1", "lam_q2", "lam_k2"))
    o_b = _diff_attention(_score_bound_ok(w["qn_b"][l], w["kn_b"][l]), qk_b, rest, lam_vecs, _row(w["subln_b"][l]),
                          lam_init=lam_init, **dims)
    o_c = _gqa_attention(_score_bound_ok(w["qn_c"][l], w["kn_c"][l]), qk_c, rest, **dims)

    merged = _merge(o_a, o_b, o_c, w["w_oa"], w["w_ob"], w["w_oc"], l, rest)
    x, x_bf16, x_sumsq = _matmul(merged, w["w_out"], l, kind="residual_stats", residual=x, out_dtype=F32,
                                 name="proj_out")

    u = _matmul(x_bf16, w["w_up"], l, kind="rownorm_relu2", row_sumsq=x_sumsq, name="mlp_up")
    return _matmul(u, w["w_down"], l, kind="residual", residual=x, out_dtype=F32, name="mlp_down")


_MATMUL_WEIGHTS = ("w_in", "w_oa", "w_ob", "w_oc", "w_out", "w_up", "w_down")


def kernel(x_prompt, x_sample, norm_mix, w_in, qn_a, kn_a, rpb, qn_b, kn_b, lam_q1, lam_k1, lam_q2, lam_k2,
           subln_b, qn_c, kn_c, w_oa, w_ob, w_oc, w_out, norm_mlp, w_up, w_down):
    w = dict(norm_mix=norm_mix, w_in=w_in, qn_a=qn_a, kn_a=kn_a, rpb=rpb, qn_b=qn_b, kn_b=kn_b, lam_q1=lam_q1,
             lam_k1=lam_k1, lam_q2=lam_q2, lam_k2=lam_k2, subln_b=subln_b, qn_c=qn_c, kn_c=kn_c, w_oa=w_oa,
             w_ob=w_ob, w_oc=w_oc, w_out=w_out, norm_mlp=norm_mlp, w_up=w_up, w_down=w_down)
    w["w_up"] = norm_mlp.astype(F32)[:, :, None] * w_up
    for name in _MATMUL_WEIGHTS:
        w[name] = w[name].astype(BF16)
    depth = w_in.shape[0]
    w["na_bias"] = [_na_bias_table(rpb[l]) for l in range(depth)]

    def trunk(x):
        batch, seq, d = x.shape
        tables = _rope_tables(seq)
        y = x.reshape(batch * seq, d)
        for l in range(depth):
            y = _layer(y, l, batch, seq, w, tables)
        return y.reshape(batch, seq, d)

    return (trunk(x_prompt), trunk(x_sample))
```

```python
import functools
import math

import jax
import jax.numpy as jnp
from jax import lax
from jax.experimental import pallas as pl
from jax.experimental.pallas import tpu as pltpu

F32 = jnp.float32
BF16 = jnp.bfloat16

LANES = 128
HEAD_DIM = 128
GRID_W = 64
NA_HEADS = 12
NA_WIN_R = 8
NA_WIN_C = 16
DIFF_HEADS = 4
GQA_Q_HEADS = 12
GQA_KV_HEADS = 4
GQA_GROUP = GQA_Q_HEADS // GQA_KV_HEADS
ROPE_THETA = 10000.0
NORM_EPS = 1e-6

W_A = NA_HEADS * HEAD_DIM
W_B_QK = 2 * DIFF_HEADS * HEAD_DIM
W_B_V = DIFF_HEADS * 2 * HEAD_DIM
W_C_Q = GQA_Q_HEADS * HEAD_DIM
W_C_KV = GQA_KV_HEADS * HEAD_DIM

OFF_QA = 0
OFF_KA = OFF_QA + W_A
OFF_VA = OFF_KA + W_A
OFF_QB = OFF_VA + W_A
OFF_KB = OFF_QB + W_B_QK
OFF_VB = OFF_KB + W_B_QK
OFF_QC = OFF_VB + W_B_V
OFF_KC = OFF_QC + W_C_Q
OFF_VC = OFF_KC + W_C_KV
OFF_GA = OFF_VC + W_C_KV
REST_OFF = OFF_VA
V_OFF_A = OFF_VA - REST_OFF
V_OFF_B = OFF_VB - REST_OFF
V_OFF_C = OFF_VC - REST_OFF
G_OFF = OFF_GA - REST_OFF

V7X_VMEM_LIMIT_BYTES = 62 * 1024 * 1024

LOG2E = 1.4426950408889634
QK_SCALE_LOG2 = HEAD_DIM ** -0.5 * LOG2E
NEG_BIG = -1e30
PLAIN_SOFTMAX_LOG2_LIMIT = 64.0


def _params(*semantics):
    return pltpu.CompilerParams(dimension_semantics=semantics, vmem_limit_bytes=V7X_VMEM_LIMIT_BYTES)


def _rmsnorm_kernel(x_ref, g_ref, o_ref):
    x = x_ref[...]
    ms = jnp.mean(x * x, axis=-1, keepdims=True)
    o_ref[...] = (x * lax.rsqrt(ms + NORM_EPS) * g_ref[...]).astype(o_ref.dtype)


def _rmsnorm(x, gain, tm=512):
    t, d = x.shape
    tm = min(tm, t)
    return pl.pallas_call(
        _rmsnorm_kernel,
        grid=(t // tm,),
        in_specs=[pl.BlockSpec((tm, d), lambda i: (i, 0)), pl.BlockSpec((1, d), lambda i: (0, 0))],
        out_specs=pl.BlockSpec((tm, d), lambda i: (i, 0)),
        out_shape=jax.ShapeDtypeStruct((t, d), BF16),
        compiler_params=_params("parallel"),
        name="rmsnorm",
    )(x, gain.reshape(1, d).astype(F32))


def _store_normed_heads(x, g_ref, o_ref, rope_refs=None, rot_half=0):
    for h in range(x.shape[1] // HEAD_DIM):
        cols = slice(h * HEAD_DIM, (h + 1) * HEAD_DIM)
        xh = x[:, cols]
        ms = jnp.mean(xh * xh, axis=-1, keepdims=True)
        y = xh * lax.rsqrt(ms + NORM_EPS) * g_ref[:, cols]
        if rope_refs is not None:
            cos_ref, sin_ref = rope_refs
            y = y * cos_ref[...] + _rotate_pairs(y, rot_half) * sin_ref[...]
        o_ref[:, cols] = y.astype(o_ref.dtype)


def _matmul_kernel(*refs, kind, nk, kdim):
    a_ref, b_ref = refs[:2]
    prod = jnp.dot(a_ref[...], b_ref[...], preferred_element_type=F32)
    if kind == "residual":
        res_ref, o_ref = refs[2:]
        if nk == 1:
            o_ref[...] = prod + res_ref[...]
        else:
            o_ref[...] = prod + jnp.where(pl.program_id(2) == 0, res_ref[...], o_ref[...])
    elif kind == "residual_stats":
        res_ref, o_ref, ob_ref, ss_ref = refs[2:]
        x = prod + res_ref[...]
        o_ref[...] = x
        ob_ref[...] = x.astype(ob_ref.dtype)
        ss_ref[...] = _lane_fold(x * x)
    elif kind == "rownorm_relu2":
        ss_ref, o_ref = refs[2:]
        ms = jnp.sum(ss_ref[...], axis=-1, keepdims=True) * (1.0 / kdim)
        o_ref[...] = jnp.square(jnp.maximum(prod * lax.rsqrt(ms + NORM_EPS), 0.0)).astype(o_ref.dtype)
    elif kind == "headnorm":
        g_ref, o_ref = refs[2:]
        _store_normed_heads(prod, g_ref, o_ref)
    else:
        (o_ref,) = refs[2:]
        o_ref[...] = prod.astype(o_ref.dtype)


def _matmul(a, w, layer, *, kind="plain", residual=None, row_sumsq=None, head_gain=None,
            n=None, col_block0=0, out_dtype=BF16, tm=1024, tn=1024, tk=4096, name="matmul"):
    m, kdim = a.shape
    n = w.shape[2] if n is None else n
    tm, tn, tk = min(tm, m), min(tn, n), min(tk, kdim)
    nk = kdim // tk
    tile = pl.BlockSpec((tm, tn), lambda i, j, k: (i, j))
    in_specs = [pl.BlockSpec((tm, tk), lambda i, j, k: (i, k)),
                pl.BlockSpec((None, tk, tn), lambda i, j, k: (layer, k, col_block0 + j))]
    args = [a, w]
    out_specs, out_shape = tile, jax.ShapeDtypeStruct((m, n), out_dtype)
    if kind == "headnorm":
        in_specs.append(pl.BlockSpec((1, tn), lambda i, j, k: (0, j)))
        args.append(head_gain)
    if kind in ("residual", "residual_stats"):
        in_specs.append(tile)
        args.append(residual)
    if kind == "residual_stats":
        out_specs = (tile, tile, pl.BlockSpec((tm, LANES), lambda i, j, k: (i, j)))
        out_shape = (out_shape, jax.ShapeDtypeStruct((m, n), BF16),
                     jax.ShapeDtypeStruct((m, n // tn * LANES), F32))
    if kind == "rownorm_relu2":
        in_specs.append(pl.BlockSpec((tm, row_sumsq.shape[1]), lambda i, j, k: (i, 0)))
        args.append(row_sumsq)
    assert nk == 1 or (kind == "residual" and out_dtype == F32)
    return pl.pallas_call(
        functools.partial(_matmul_kernel, kind=kind, nk=nk, kdim=kdim),
        grid=(m // tm, n // tn, nk),
        in_specs=in_specs,
        out_specs=out_specs,
        out_shape=out_shape,
        compiler_params=_params("parallel", "parallel", "arbitrary"),
        name=name,
    )(*args)


def _rotate_pairs(y, half):
    width = y.shape[-1]
    if 2 * half == width:
        return pltpu.roll(y, half, 1)
    lane = lax.broadcasted_iota(jnp.int32, y.shape, 1)
    first = (lane % (2 * half)) < half
    return jnp.where(first, pltpu.roll(y, width - half, 1), pltpu.roll(y, half, 1))


def _headnorm_kernel(x_ref, g_ref, cos_ref, sin_ref, o_ref, *, rot_half):
    _store_normed_heads(x_ref[...].astype(F32), g_ref, o_ref, (cos_ref, sin_ref), rot_half)


def _headnorm(x, col_off, head_gain, rope, seq, tm=1024, bw=512):
    t = x.shape[0]
    width = head_gain.shape[1]
    tm = min(tm, seq)
    assert col_off % bw == 0 and width % bw == 0 and seq % tm == 0
    cb = col_off // bw
    cos, sin, rot_half = rope
    nsb = seq // tm
    tab_spec = pl.BlockSpec((tm, HEAD_DIM), lambda i, j: (i % nsb, 0))
    return pl.pallas_call(
        functools.partial(_headnorm_kernel, rot_half=rot_half),
        grid=(t // tm, width // bw),
        in_specs=[pl.BlockSpec((tm, bw), lambda i, j: (i, cb + j)), pl.BlockSpec((1, bw), lambda i, j: (0, j)),
                  tab_spec, tab_spec],
        out_specs=pl.BlockSpec((tm, bw), lambda i, j: (i, j)),
        out_shape=jax.ShapeDtypeStruct((t, width), BF16),
        compiler_params=_params("parallel", "parallel"),
        name="headnorm",
    )(x, head_gain, cos, sin)


def _rope_tables(seq):
    pos = jnp.arange(seq, dtype=jnp.int32)

    def angles(p, dr):
        inv = ROPE_THETA ** (-jnp.arange(0, dr, 2, dtype=F32) / dr)
        return p.astype(F32)[:, None] * inv[None, :]

    a1 = angles(pos, HEAD_DIM)
    cos1 = jnp.concatenate([jnp.cos(a1), jnp.cos(a1)], axis=-1)
    sin1 = jnp.concatenate([-jnp.sin(a1), jnp.sin(a1)], axis=-1)
    ar = angles(pos // GRID_W, HEAD_DIM // 2)
    ac = angles(pos % GRID_W, HEAD_DIM // 2)
    cos2 = jnp.concatenate([jnp.cos(ar), jnp.cos(ar), jnp.cos(ac), jnp.cos(ac)], axis=-1)
    sin2 = jnp.concatenate([-jnp.sin(ar), jnp.sin(ar), -jnp.sin(ac), jnp.sin(ac)], axis=-1)
    return (cos1, sin1, HEAD_DIM // 2), (cos2, sin2, HEAD_DIM // 4)


def _qk(q, k):
    return lax.dot_general(q, k, (((1,), (1,)), ((), ())), preferred_element_type=F32)


def _score_bound_ok(q_gain, k_gain, bias=None):
    bound = HEAD_DIM * QK_SCALE_LOG2 * jnp.max(jnp.abs(q_gain)) * jnp.max(jnp.abs(k_gain))
    if bias is not None:
        bound = bound + LOG2E * jnp.max(jnp.abs(bias))
    return bound <= PLAIN_SOFTMAX_LOG2_LIMIT


def _lane_fold(p):
    out = p[:, :LANES]
    for c in range(1, p.shape[1] // LANES):
        out = out + p[:, c * LANES:(c + 1) * LANES]
    return out


def _online_step(s, vj, m, l, acc):
    m_new = jnp.maximum(m, jnp.max(s, axis=-1, keepdims=True))
    alpha = jnp.exp2(m - m_new)
    p = jnp.exp2(s - m_new)
    l_new = alpha * l + jnp.sum(p, axis=-1, keepdims=True)
    acc_new = alpha * acc + jnp.dot(p.astype(vj.dtype), vj, preferred_element_type=F32)
    return m_new, l_new, acc_new


def _attend(score_fn, k_ref, v_ref, rows, dv, seq, tk, online, scratch):
    def chunk(j):
        start = pl.multiple_of(j * tk, tk)
        return k_ref[pl.ds(start, tk), :], v_ref[pl.ds(start, tk), :]

    if online:
        def body(j, carry):
            kj, vj = chunk(j)
            return _online_step(score_fn(kj), vj, *carry)

        init = (jnp.full((rows, 1), -jnp.inf, F32), jnp.zeros((rows, 1), F32), jnp.zeros((rows, dv), F32))
        _, l, acc = lax.fori_loop(0, seq // tk, body, init)
        return acc / l

    acc_ref, l_ref = scratch
    acc_ref[...] = jnp.zeros_like(acc_ref)
    l_ref[...] = jnp.zeros_like(l_ref)

    def body(j, carry):
        kj, vj = chunk(j)
        p = jnp.exp2(score_fn(kj))
        l_ref[...] += _lane_fold(p)
        acc_ref[...] = jnp.dot(p.astype(vj.dtype), vj, preferred_element_type=F32) + acc_ref[...]
        return carry

    lax.fori_loop(0, seq // tk, body, 0)
    return acc_ref[...] / jnp.sum(l_ref[...], axis=-1, keepdims=True)


def _attend_scratch(rows, dv):
    return [pltpu.VMEM((rows, dv), F32), pltpu.VMEM((rows, LANES), F32)]


def _by_softmax_kind(plain_ref, run):
    @pl.when(plain_ref[0] != 0)
    def _():
        run(False)

    @pl.when(plain_ref[0] == 0)
    def _():
        run(True)


def _plain_flag(plain_ok):
    return plain_ok.astype(jnp.int32).reshape(1)


_FLAG_SPEC = pl.BlockSpec(memory_space=pltpu.SMEM)


def _transpose_kernel(x_ref, o_ref):
    o_ref[...] = x_ref[...].T


def _transposed_columns(x, col_off, width, tm=1024):
    t = x.shape[0]
    tm = min(tm, t)
    assert col_off % width == 0 and t % tm == 0
    cb = col_off // width
    return pl.pallas_call(
        _transpose_kernel,
        grid=(t // tm,),
        in_specs=[pl.BlockSpec((tm, width), lambda i: (i, cb))],
        out_specs=pl.BlockSpec((width, tm), lambda i: (0, i)),
        out_shape=jax.ShapeDtypeStruct((width, t), x.dtype),
        compiler_params=_params("parallel"),
        name="transpose_cols",
    )(x)


def _gqa_kernel(plain_ref, q_ref, k_ref, v_ref, vt_ref, o_ref, acc_ref, l_ref, *, seq, tk):
    tq = q_ref.shape[0]
    m = GQA_GROUP * tq
    q = jnp.concatenate([q_ref[:, g * HEAD_DIM:(g + 1) * HEAD_DIM] for g in range(GQA_GROUP)], axis=0)

    def store(o):
        for g in range(GQA_GROUP):
            o_ref[:, g * HEAD_DIM:(g + 1) * HEAD_DIM] = o[g * tq:(g + 1) * tq].astype(o_ref.dtype)

    def run(online):
        if online:
            store(_attend(lambda kj: _qk(q, kj), k_ref, v_ref, m, HEAD_DIM, seq, tk, True, None))
            return
        acc_ref[...] = jnp.zeros_like(acc_ref)
        l_ref[...] = jnp.zeros_like(l_ref)

        def body(j, carry):
            start = pl.multiple_of(j * tk, tk)
            pt = jnp.exp2(_qk(k_ref[pl.ds(start, tk), :], q))
            l_ref[...] += pt.reshape(tk // 8, 8, m).sum(axis=0)
            acc_ref[...] = (jnp.dot(vt_ref[:, pl.ds(start, tk)], pt.astype(BF16), preferred_element_type=F32)
                            + acc_ref[...])
            return carry

        lax.fori_loop(0, seq // tk, body, 0)
        store((acc_ref[...] / jnp.sum(l_ref[...], axis=0, keepdims=True)).T)

    _by_softmax_kind(plain_ref, run)


def _gqa_attention(plain_ok, qk, v_all, *, batch, seq, tq=512, tk=2048):
    t = batch * seq
    tq, tk = min(tq, seq), min(tk, seq)
    nqb = seq // tq
    gw = GQA_GROUP * HEAD_DIM
    kcb = W_C_Q // HEAD_DIM
    vcb = V_OFF_C // HEAD_DIM
    q_spec = pl.BlockSpec((tq, gw), lambda b, n, i: (b * nqb + i, n))
    v_t = _transposed_columns(v_all, V_OFF_C, W_C_KV)
    return pl.pallas_call(
        functools.partial(_gqa_kernel, seq=seq, tk=tk),
        grid=(batch, GQA_KV_HEADS, nqb),
        in_specs=[_FLAG_SPEC, q_spec,
                  pl.BlockSpec((seq, HEAD_DIM), lambda b, n, i: (b, kcb + n)),
                  pl.BlockSpec((seq, HEAD_DIM), lambda b, n, i: (b, vcb + n)),
                  pl.BlockSpec((HEAD_DIM, seq), lambda b, n, i: (n, b))],
        out_specs=q_spec,
        out_shape=jax.ShapeDtypeStruct((t, W_C_Q), BF16),
        scratch_shapes=[pltpu.VMEM((HEAD_DIM, GQA_GROUP * tq), F32), pltpu.VMEM((8, GQA_GROUP * tq), F32)],
        compiler_params=_params("parallel", "parallel", "parallel"),
        name="gqa_attention",
    )(_plain_flag(plain_ok), qk, qk, v_all, v_t)


def _diff_kernel(plain_ref, q_ref, k_ref, v_ref, lq1_ref, lk1_ref, lq2_ref, lk2_ref, sg_ref, o_ref, *scratch,
                 seq, tk, lam_init):
    tq = q_ref.shape[0]
    dv = 2 * HEAD_DIM
    q1 = q_ref[:, :HEAD_DIM]
    q2 = q_ref[:, HEAD_DIM:]

    def scores(kj):
        return jnp.concatenate([_qk(q1, kj[:, :HEAD_DIM]), _qk(q2, kj[:, HEAD_DIM:])], axis=0)

    def run(online):
        o12 = _attend(scores, k_ref, v_ref, 2 * tq, dv, seq, tk, online, scratch)
        lam = (jnp.exp(jnp.sum(lq1_ref[...] * lk1_ref[...], axis=-1, keepdims=True))
               - jnp.exp(jnp.sum(lq2_ref[...] * lk2_ref[...], axis=-1, keepdims=True)) + lam_init)
        o = o12[:tq] - lam * o12[tq:]
        ms = jnp.mean(o * o, axis=-1, keepdims=True)
        o = o * lax.rsqrt(ms + NORM_EPS) * sg_ref[...] * (1.0 - lam_init)
        o_ref[...] = o.astype(o_ref.dtype)

    _by_softmax_kind(plain_ref, run)


def _diff_attention(plain_ok, qk, v_all, lam_vecs, sub_gain, *, lam_init, batch, seq, tq=512, tk=2048):
    t = batch * seq
    tq, tk = min(tq, seq), min(tk, seq)
    nqb = seq // tq
    pw = 2 * HEAD_DIM
    kcb = W_B_QK // pw
    vcb = V_OFF_B // pw
    vec_spec = pl.BlockSpec((1, HEAD_DIM), lambda b, h, i: (0, 0))
    return pl.pallas_call(
        functools.partial(_diff_kernel, seq=seq, tk=tk, lam_init=lam_init),
        grid=(batch, DIFF_HEADS, nqb),
        in_specs=[
            _FLAG_SPEC,
            pl.BlockSpec((tq, pw), lambda b, h, i: (b * nqb + i, h)),
            pl.BlockSpec((seq, pw), lambda b, h, i: (b, kcb + h)),
            pl.BlockSpec((seq, pw), lambda b, h, i: (b, vcb + h)),
            vec_spec, vec_spec, vec_spec, vec_spec,
            pl.BlockSpec((1, pw), lambda b, h, i: (0, 0)),
        ],
        out_specs=pl.BlockSpec((tq, pw), lambda b, h, i: (b * nqb + i, h)),
        out_shape=jax.ShapeDtypeStruct((t, W_B_V), BF16),
        scratch_shapes=_attend_scratch(2 * tq, pw),
        compiler_params=_params("parallel", "parallel", "parallel"),
        name="diff_attention",
    )(_plain_flag(plain_ok), qk, qk, v_all, *lam_vecs, sub_gain)


NA_Q_ROWS = 4
NA_K_ROWS = NA_Q_ROWS + NA_WIN_R
NA_BLOCK_TYPES = 3


def _na_bias_table(rpb):
    col = jnp.arange(GRID_W)
    col_start = jnp.clip(col - NA_WIN_C // 2, 0, GRID_W - NA_WIN_C)
    col_mask = (col[None, :] >= col_start[:, None]) & (col[None, :] < col_start[:, None] + NA_WIN_C)
    dc = jnp.clip(col[None, :] - col[:, None], -(NA_WIN_C - 1), NA_WIN_C - 1) + NA_WIN_C - 1
    rpb_c = rpb.astype(F32)[:, :, dc] * LOG2E
    rr = jnp.arange(NA_Q_ROWS)
    q_off = jnp.stack([rr, rr + NA_WIN_R // 2, rr + NA_K_ROWS - NA_Q_ROWS])
    w_off = jnp.stack([0 * rr, rr, 0 * rr + NA_K_ROWS - NA_WIN_R])
    i = jnp.arange(NA_K_ROWS)
    dr = i[None, None, :] - q_off[:, :, None] + NA_WIN_R - 1
    row_ok = (i[None, None, :] >= w_off[:, :, None]) & (i[None, None, :] < w_off[:, :, None] + NA_WIN_R)
    tab = rpb_c[:, jnp.clip(dr, 0, 2 * NA_WIN_R - 2)]
    ok = row_ok[None, :, :, :, None, None] & col_mask[None, None, None, None]
    tab = jnp.where(ok, tab, NEG_BIG).transpose(0, 1, 2, 4, 3, 5)
    return tab.reshape(rpb.shape[0], NA_BLOCK_TYPES, NA_Q_ROWS * GRID_W, NA_K_ROWS * GRID_W)


def _na_kernel(plain_ref, q_ref, k_ref, v_ref, bias_ref, o_ref, *, rows):
    nrb = rows // NA_Q_ROWS
    tq = NA_Q_ROWS * GRID_W

    def run(online):
        def row_block(rb, carry):
            ws = jnp.clip(rb * NA_Q_ROWS - NA_WIN_R // 2, 0, rows - NA_K_ROWS)
            start = pl.multiple_of(ws * GRID_W, GRID_W)
            qstart = pl.multiple_of(rb * tq, tq)
            kw = k_ref[pl.ds(start, NA_K_ROWS * GRID_W), :]
            vw = v_ref[pl.ds(start, NA_K_ROWS * GRID_W), :]
            block_type = jnp.where(rb == 0, 0, jnp.where(rb == nrb - 1, 2, 1))
            s = _qk(q_ref[pl.ds(qstart, tq), :], kw) + bias_ref[0, block_type]
            if online:
                s = s - jnp.max(s, axis=-1, keepdims=True)
            p = jnp.exp2(s)
            l = jnp.sum(p, axis=-1, keepdims=True)
            o = jnp.dot(p.astype(vw.dtype), vw, preferred_element_type=F32) / l
            o_ref[pl.ds(qstart, tq), :] = o.astype(o_ref.dtype)
            return carry

        lax.fori_loop(0, nrb, row_block, 0, unroll=2)

    _by_softmax_kind(plain_ref, run)


def _na_attention(plain_ok, qk, v_all, bias_tab, *, batch, seq):
    t = batch * seq
    rows = seq // GRID_W
    assert rows >= NA_K_ROWS and rows % (2 * NA_Q_ROWS) == 0
    kcb = W_A // HEAD_DIM
    vcb = V_OFF_A // HEAD_DIM
    return pl.pallas_call(
        functools.partial(_na_kernel, rows=rows),
        grid=(batch, NA_HEADS),
        in_specs=[
            _FLAG_SPEC,
            pl.BlockSpec((seq, HEAD_DIM), lambda b, h: (b, h)),
            pl.BlockSpec((seq, HEAD_DIM), lambda b, h: (b, kcb + h)),
            pl.BlockSpec((seq, HEAD_DIM), lambda b, h: (b, vcb + h)),
            pl.BlockSpec((1, NA_BLOCK_TYPES, NA_Q_ROWS * GRID_W, NA_K_ROWS * GRID_W), lambda b, h: (h, 0, 0, 0)),
        ],
        out_specs=pl.BlockSpec((seq, HEAD_DIM), lambda b, h: (b, h)),
        out_shape=jax.ShapeDtypeStruct((t, W_A), BF16),
        compiler_params=_params("parallel", "parallel"),
        name="na_attention",
    )(_plain_flag(plain_ok), qk, qk, v_all, bias_tab)


def _merge_kernel(oa_ref, ob_ref, oc_ref, wa_ref, wb_ref, wc_ref, ga_ref, gb_ref, gc_ref, o_ref):
    def branch(o_r, w_r, g_r):
        y = jnp.dot(o_r[...], w_r[...], preferred_element_type=F32)
        return jax.nn.sigmoid(g_r[...].astype(F32)) * y

    merged = branch(oa_ref, wa_ref, ga_ref) + branch(ob_ref, wb_ref, gb_ref) + branch(oc_ref, wc_ref, gc_ref)
    o_ref[...] = merged.astype(o_ref.dtype)


def _merge(o_a, o_b, o_c, w_oa, w_ob, w_oc, layer, rest, tm=1024, tn=1024):
    t = o_a.shape[0]
    d = w_oa.shape[2]
    tm, tn = min(tm, t), min(tn, d)
    assert d % tn == 0 and G_OFF % tn == 0
    gcb = G_OFF // tn
    ncb = d // tn

    def act(width):
        return pl.BlockSpec((tm, width), lambda j, i: (i, 0))

    def wgt(width):
        return pl.BlockSpec((None, width, tn), lambda j, i: (layer, 0, j))

    def gate(branch):
        return pl.BlockSpec((tm, tn), lambda j, i: (i, gcb + branch * ncb + j))

    return pl.pallas_call(
        _merge_kernel,
        grid=(ncb, t // tm),
        in_specs=[act(W_A), act(W_B_V), act(W_C_Q), wgt(W_A), wgt(W_B_V), wgt(W_C_Q), gate(0), gate(1), gate(2)],
        out_specs=pl.BlockSpec((tm, tn), lambda j, i: (i, j)),
        out_shape=jax.ShapeDtypeStruct((t, d), BF16),
        compiler_params=_params("parallel", "parallel"),
        name="gated_merge",
    )(o_a, o_b, o_c, w_oa, w_ob, w_oc, rest, rest, rest)


def _row(v):
    return v.reshape(1, -1).astype(F32)


def _input_projection(h, w, l, seq, tables):
    rope1, rope2 = tables
    w_in = w["w_in"]
    tn = 1024
    assert REST_OFF == 2 * W_A and REST_OFF % tn == 0

    def gains(q_gain, k_gain, q_heads, k_heads):
        return jnp.concatenate([jnp.tile(q_gain.astype(F32) * QK_SCALE_LOG2, q_heads),
                                jnp.tile(k_gain.astype(F32), k_heads)]).reshape(1, -1)

    qk_a = _matmul(h, w_in, l, kind="headnorm", n=REST_OFF, tn=tn,
                   head_gain=gains(w["qn_a"][l], w["kn_a"][l], NA_HEADS, NA_HEADS), name="proj_qk_a")
    rest = _matmul(h, w_in, l, n=w_in.shape[2] - REST_OFF, tn=tn, col_block0=REST_OFF // tn, name="proj_rest")
    qk_b = _headnorm(rest, OFF_QB - REST_OFF, gains(w["qn_b"][l], w["kn_b"][l], 2 * DIFF_HEADS, 2 * DIFF_HEADS),
                     rope1, seq)
    qk_c = _headnorm(rest, OFF_QC - REST_OFF, gains(w["qn_c"][l], w["kn_c"][l], GQA_Q_HEADS, GQA_KV_HEADS),
                     rope2, seq)
    return qk_a, qk_b, qk_c, rest


def _layer(x, l, batch, seq, w, tables):
    lam_init = 0.8 - 0.6 * math.exp(-0.3 * l)
    dims = dict(batch=batch, seq=seq)
    h = _rmsnorm(x, w["norm_mix"][l])
    qk_a, qk_b, qk_c, rest = _input_projection(h, w, l, seq, tables)

    o_a = _na_attention(_score_bound_ok(w["qn_a"][l], w["kn_a"][l], w["rpb"][l]), qk_a, rest, w["na_bias"][l],
                        **dims)
    lam_vecs = tuple(_row(w[n][l]) for n in ("lam_q1", "lam_k1", "lam_q2", "lam_k2"))
    o_b = _diff_attention(_score_bound_ok(w["qn_b"][l], w["kn_b"][l]), qk_b, rest, lam_vecs, _row(w["subln_b"][l]),
                          lam_init=lam_init, **dims)
    o_c = _gqa_attention(_score_bound_ok(w["qn_c"][l], w["kn_c"][l]), qk_c, rest, **dims)

    merged = _merge(o_a, o_b, o_c, w["w_oa"], w["w_ob"], w["w_oc"], l, rest)
    x, x_bf16, x_sumsq = _matmul(merged, w["w_out"], l, kind="residual_stats", residual=x, out_dtype=F32,
                                 name="proj_out")

    u = _matmul(x_bf16, w["w_up"], l, kind="rownorm_relu2", row_sumsq=x_sumsq, name="mlp_up")
    return _matmul(u, w["w_down"], l, kind="residual", residual=x, out_dtype=F32, name="mlp_down")


_MATMUL_WEIGHTS = ("w_in", "w_oa", "w_ob", "w_oc", "w_out", "w_up", "w_down")


def kernel(x_prompt, x_sample, norm_mix, w_in, qn_a, kn_a, rpb, qn_b, kn_b, lam_q1, lam_k1, lam_q2, lam_k2,
           subln_b, qn_c, kn_c, w_oa, w_ob, w_oc, w_out, norm_mlp, w_up, w_down):
    w = dict(norm_mix=norm_mix, w_in=w_in, qn_a=qn_a, kn_a=kn_a, rpb=rpb, qn_b=qn_b, kn_b=kn_b, lam_q1=lam_q1,
             lam_k1=lam_k1, lam_q2=lam_q2, lam_k2=lam_k2, subln_b=subln_b, qn_c=qn_c, kn_c=kn_c, w_oa=w_oa,
             w_ob=w_ob, w_oc=w_oc, w_out=w_out, norm_mlp=norm_mlp, w_up=w_up, w_down=w_down)
    w["w_up"] = norm_mlp.astype(F32)[:, :, None] * w_up
    for name in _MATMUL_WEIGHTS:
        w[name] = w[name].astype(BF16)
    depth = w_in.shape[0]
    w["na_bias"] = [_na_bias_table(rpb[l]) for l in range(depth)]

    def trunk(x):
        batch, seq, d = x.shape
        tables = _rope_tables(seq)
        y = x.reshape(batch * seq, d)
        for l in range(depth):
            y = _layer(y, l, batch, seq, w, tables)
        return y.reshape(batch, seq, d)

    return (trunk(x_prompt), trunk(x_sample))
```

```python
import functools
import math

import jax
import jax.numpy as jnp
from jax import lax
from jax.experimental import pallas as pl
from jax.experimental.pallas import tpu as pltpu

F32 = jnp.float32
BF16 = jnp.bfloat16

LANES = 128
HEAD_DIM = 128
GRID_W = 64
NA_HEADS = 12
NA_WIN_R = 8
NA_WIN_C = 16
DIFF_HEADS = 4
GQA_Q_HEADS = 12
GQA_KV_HEADS = 4
GQA_GROUP = GQA_Q_HEADS // GQA_KV_HEADS
ROPE_THETA = 10000.0
NORM_EPS = 1e-6

W_A = NA_HEADS * HEAD_DIM
W_B_QK = 2 * DIFF_HEADS * HEAD_DIM
W_B_V = DIFF_HEADS * 2 * HEAD_DIM
W_C_Q = GQA_Q_HEADS * HEAD_DIM
W_C_KV = GQA_KV_HEADS * HEAD_DIM

OFF_QA = 0
OFF_KA = OFF_QA + W_A
OFF_VA = OFF_KA + W_A
OFF_QB = OFF_VA + W_A
OFF_KB = OFF_QB + W_B_QK
OFF_VB = OFF_KB + W_B_QK
OFF_QC = OFF_VB + W_B_V
OFF_KC = OFF_QC + W_C_Q
OFF_VC = OFF_KC + W_C_KV
OFF_GA = OFF_VC + W_C_KV
REST_OFF = OFF_VA
V_OFF_A = OFF_VA - REST_OFF
V_OFF_B = OFF_VB - REST_OFF
V_OFF_C = OFF_VC - REST_OFF
G_OFF = OFF_GA - REST_OFF

V7X_VMEM_LIMIT_BYTES = 62 * 1024 * 1024

LOG2E = 1.4426950408889634
QK_SCALE_LOG2 = HEAD_DIM ** -0.5 * LOG2E
NEG_BIG = -1e30
PLAIN_SOFTMAX_LOG2_LIMIT = 64.0


def _params(*semantics):
    return pltpu.CompilerParams(dimension_semantics=semantics, vmem_limit_bytes=V7X_VMEM_LIMIT_BYTES)


def _rmsnorm_kernel(x_ref, g_ref, o_ref):
    x = x_ref[...]
    ms = jnp.mean(x * x, axis=-1, keepdims=True)
    o_ref[...] = (x * lax.rsqrt(ms + NORM_EPS) * g_ref[...]).astype(o_ref.dtype)


def _rmsnorm(x, gain, tm=512):
    t, d = x.shape
    tm = min(tm, t)
    return pl.pallas_call(
        _rmsnorm_kernel,
        grid=(t // tm,),
        in_specs=[pl.BlockSpec((tm, d), lambda i: (i, 0)), pl.BlockSpec((1, d), lambda i: (0, 0))],
        out_specs=pl.BlockSpec((tm, d), lambda i: (i, 0)),
        out_shape=jax.ShapeDtypeStruct((t, d), BF16),
        compiler_params=_params("parallel"),
        name="rmsnorm",
    )(x, gain.reshape(1, d).astype(F32))


def _store_normed_heads(x, g_ref, o_ref, rope_refs=None, rot_half=0):
    for h in range(x.shape[1] // HEAD_DIM):
        cols = slice(h * HEAD_DIM, (h + 1) * HEAD_DIM)
        xh = x[:, cols]
        ms = jnp.mean(xh * xh, axis=-1, keepdims=True)
        y = xh * lax.rsqrt(ms + NORM_EPS) * g_ref[:, cols]
        if rope_refs is not None:
            cos_ref, sin_ref = rope_refs
            y = y * cos_ref[...] + _rotate_pairs(y, rot_half) * sin_ref[...]
        o_ref[:, cols] = y.astype(o_ref.dtype)


def _matmul_kernel(*refs, kind, nk, kdim):
    a_ref, b_ref = refs[:2]
    prod = jnp.dot(a_ref[...], b_ref[...], preferred_element_type=F32)
    if kind == "residual":
        res_ref, o_ref = refs[2:]
        if nk == 1:
            o_ref[...] = prod + res_ref[...]
        else:
            o_ref[...] = prod + jnp.where(pl.program_id(2) == 0, res_ref[...], o_ref[...])
    elif kind == "residual_stats":
        res_ref, o_ref, ob_ref, ss_ref = refs[2:]
        x = prod + res_ref[...]
        o_ref[...] = x
        ob_ref[...] = x.astype(ob_ref.dtype)
        ss_ref[...] = _lane_fold(x * x)
    elif kind == "rownorm_relu2":
        ss_ref, o_ref = refs[2:]
        ms = jnp.sum(ss_ref[...], axis=-1, keepdims=True) * (1.0 / kdim)
        o_ref[...] = jnp.square(jnp.maximum(prod * lax.rsqrt(ms + NORM_EPS), 0.0)).astype(o_ref.dtype)
    elif kind == "headnorm":
        g_ref, o_ref = refs[2:]
        _store_normed_heads(prod, g_ref, o_ref)
    else:
        (o_ref,) = refs[2:]
        o_ref[...] = prod.astype(o_ref.dtype)


def _matmul(a, w, layer, *, kind="plain", residual=None, row_sumsq=None, head_gain=None,
            n=None, col_block0=0, out_dtype=BF16, tm=1024, tn=1024, tk=4096, name="matmul"):
    m, kdim = a.shape
    n = w.shape[2] if n is None else n
    tm, tn, tk = min(tm, m), min(tn, n), min(tk, kdim)
    nk = kdim // tk
    tile = pl.BlockSpec((tm, tn), lambda i, j, k: (i, j))
    in_specs = [pl.BlockSpec((tm, tk), lambda i, j, k: (i, k)),
                pl.BlockSpec((None, tk, tn), lambda i, j, k: (layer, k, col_block0 + j))]
    args = [a, w]
    out_specs, out_shape = tile, jax.ShapeDtypeStruct((m, n), out_dtype)
    if kind == "headnorm":
        in_specs.append(pl.BlockSpec((1, tn), lambda i, j, k: (0, j)))
        args.append(head_gain)
    if kind in ("residual", "residual_stats"):
        in_specs.append(tile)
        args.append(residual)
    if kind == "residual_stats":
        out_specs = (tile, tile, pl.BlockSpec((tm, LANES), lambda i, j, k: (i, j)))
        out_shape = (out_shape, jax.ShapeDtypeStruct((m, n), BF16),
                     jax.ShapeDtypeStruct((m, n // tn * LANES), F32))
    if kind == "rownorm_relu2":
        in_specs.append(pl.BlockSpec((tm, row_sumsq.shape[1]), lambda i, j, k: (i, 0)))
        args.append(row_sumsq)
    assert nk == 1 or (kind == "residual" and out_dtype == F32)
    return pl.pallas_call(
        functools.partial(_matmul_kernel, kind=kind, nk=nk, kdim=kdim),
        grid=(m // tm, n // tn, nk),
        in_specs=in_specs,
        out_specs=out_specs,
        out_shape=out_shape,
        compiler_params=_params("parallel", "parallel", "arbitrary"),
        name=name,
    )(*args)


def _rotate_pairs(y, half):
    width = y.shape[-1]
    if 2 * half == width:
        return pltpu.roll(y, half, 1)
    lane = lax.broadcasted_iota(jnp.int32, y.shape, 1)
    first = (lane % (2 * half)) < half
    return jnp.where(first, pltpu.roll(y, width - half, 1), pltpu.roll(y, half, 1))


def _headnorm_kernel(x_ref, g_ref, cos_ref, sin_ref, o_ref, *, rot_half):
    _store_normed_heads(x_ref[...].astype(F32), g_ref, o_ref, (cos_ref, sin_ref), rot_half)


def _headnorm(x, col_off, head_gain, rope, seq, tm=1024, bw=512):
    t = x.shape[0]
    width = head_gain.shape[1]
    tm = min(tm, seq)
    assert col_off % bw == 0 and width % bw == 0 and seq % tm == 0
    cb = col_off // bw
    cos, sin, rot_half = rope
    nsb = seq // tm
    tab_spec = pl.BlockSpec((tm, HEAD_DIM), lambda i, j: (i % nsb, 0))
    return pl.pallas_call(
        functools.partial(_headnorm_kernel, rot_half=rot_half),
        grid=(t // tm, width // bw),
        in_specs=[pl.BlockSpec((tm, bw), lambda i, j: (i, cb + j)), pl.BlockSpec((1, bw), lambda i, j: (0, j)),
                  tab_spec, tab_spec],
        out_specs=pl.BlockSpec((tm, bw), lambda i, j: (i, j)),
        out_shape=jax.ShapeDtypeStruct((t, width), BF16),
        compiler_params=_params("parallel", "parallel"),
        name="headnorm",
    )(x, head_gain, cos, sin)


def _rope_tables(seq):
    pos = jnp.arange(seq, dtype=jnp.int32)

    def angles(p, dr):
        inv = ROPE_THETA ** (-jnp.arange(0, dr, 2, dtype=F32) / dr)
        return p.astype(F32)[:, None] * inv[None, :]

    a1 = angles(pos, HEAD_DIM)
    cos1 = jnp.concatenate([jnp.cos(a1), jnp.cos(a1)], axis=-1)
    sin1 = jnp.concatenate([-jnp.sin(a1), jnp.sin(a1)], axis=-1)
    ar = angles(pos // GRID_W, HEAD_DIM // 2)
    ac = angles(pos % GRID_W, HEAD_DIM // 2)
    cos2 = jnp.concatenate([jnp.cos(ar), jnp.cos(ar), jnp.cos(ac), jnp.cos(ac)], axis=-1)
    sin2 = jnp.concatenate([-jnp.sin(ar), jnp.sin(ar), -jnp.sin(ac), jnp.sin(ac)], axis=-1)
    return (cos1, sin1, HEAD_DIM // 2), (cos2, sin2, HEAD_DIM // 4)


def _qk(q, k):
    return lax.dot_general(q, k, (((1,), (1,)), ((), ())), preferred_element_type=F32)


def _score_bound_ok(q_gain, k_gain, bias=None):
    bound = HEAD_DIM * QK_SCALE_LOG2 * jnp.max(jnp.abs(q_gain)) * jnp.max(jnp.abs(k_gain))
    if bias is not None:
        bound = bound + LOG2E * jnp.max(jnp.abs(bias))
    return bound <= PLAIN_SOFTMAX_LOG2_LIMIT


def _lane_fold(p):
    out = p[:, :LANES]
    for c in range(1, p.shape[1] // LANES):
        out = out + p[:, c * LANES:(c + 1) * LANES]
    return out


def _online_step(s, vj, m, l, acc):
    m_new = jnp.maximum(m, jnp.max(s, axis=-1, keepdims=True))
    alpha = jnp.exp2(m - m_new)
    p = jnp.exp2(s - m_new)
    l_new = alpha * l + jnp.sum(p, axis=-1, keepdims=True)
    acc_new = alpha * acc + jnp.dot(p.astype(vj.dtype), vj, preferred_element_type=F32)
    return m_new, l_new, acc_new


def _attend(score_fn, k_ref, v_ref, rows, dv, seq, tk, online, scratch):
    def chunk(j):
        start = pl.multiple_of(j * tk, tk)
        return k_ref[pl.ds(start, tk), :], v_ref[pl.ds(start, tk), :]

    if online:
        def body(j, carry):
            kj, vj = chunk(j)
            return _online_step(score_fn(kj), vj, *carry)

        init = (jnp.full((rows, 1), -jnp.inf, F32), jnp.zeros((rows, 1), F32), jnp.zeros((rows, dv), F32))
        _, l, acc = lax.fori_loop(0, seq // tk, body, init)
        return acc / l

    acc_ref, l_ref = scratch
    acc_ref[...] = jnp.zeros_like(acc_ref)
    l_ref[...] = jnp.zeros_like(l_ref)

    def body(j, carry):
        kj, vj = chunk(j)
        p = jnp.exp2(score_fn(kj))
        l_ref[...] += _lane_fold(p)
        acc_ref[...] = jnp.dot(p.astype(vj.dtype), vj, preferred_element_type=F32) + acc_ref[...]
        return carry

    lax.fori_loop(0, seq // tk, body, 0)
    return acc_ref[...] / jnp.sum(l_ref[...], axis=-1, keepdims=True)


def _attend_scratch(rows, dv):
    return [pltpu.VMEM((rows, dv), F32), pltpu.VMEM((rows, LANES), F32)]


def _by_softmax_kind(plain_ref, run):
    @pl.when(plain_ref[0] != 0)
    def _():
        run(False)

    @pl.when(plain_ref[0] == 0)
    def _():
        run(True)


def _plain_flag(plain_ok):
    return plain_ok.astype(jnp.int32).reshape(1)


_FLAG_SPEC = pl.BlockSpec(memory_space=pltpu.SMEM)


def _transpose_kernel(x_ref, o_ref):
    o_ref[...] = x_ref[...].T


def _transposed_columns(x, col_off, width, tm=1024):
    t = x.shape[0]
    tm = min(tm, t)
    assert col_off % width == 0 and t % tm == 0
    cb = col_off // width
    return pl.pallas_call(
        _transpose_kernel,
        grid=(t // tm,),
        in_specs=[pl.BlockSpec((tm, width), lambda i: (i, cb))],
        out_specs=pl.BlockSpec((width, tm), lambda i: (0, i)),
        out_shape=jax.ShapeDtypeStruct((width, t), x.dtype),
        compiler_params=_params("parallel"),
        name="transpose_cols",
    )(x)


def _gqa_kernel(plain_ref, q_ref, k_ref, v_ref, vt_ref, o_ref, acc_ref, l_ref, *, seq, tk):
    tq = q_ref.shape[0]
    m = GQA_GROUP * tq
    q = jnp.concatenate([q_ref[:, g * HEAD_DIM:(g + 1) * HEAD_DIM] for g in range(GQA_GROUP)], axis=0)

    def store(o):
        for g in range(GQA_GROUP):
            o_ref[:, g * HEAD_DIM:(g + 1) * HEAD_DIM] = o[g * tq:(g + 1) * tq].astype(o_ref.dtype)

    def run(online):
        if online:
            store(_attend(lambda kj: _qk(q, kj), k_ref, v_ref, m, HEAD_DIM, seq, tk, True, None))
            return
        acc_ref[...] = jnp.zeros_like(acc_ref)
        l_ref[...] = jnp.zeros_like(l_ref)

        def body(j, carry):
            start = pl.multiple_of(j * tk, tk)
            pt = jnp.exp2(_qk(k_ref[pl.ds(start, tk), :], q))
            l_ref[...] += pt.reshape(tk // 8, 8, m).sum(axis=0)
            acc_ref[...] = (jnp.dot(vt_ref[:, pl.ds(start, tk)], pt.astype(BF16), preferred_element_type=F32)
                            + acc_ref[...])
            return carry

        lax.fori_loop(0, seq // tk, body, 0)
        store((acc_ref[...] / jnp.sum(l_ref[...], axis=0, keepdims=True)).T)

    _by_softmax_kind(plain_ref, run)


def _gqa_attention(plain_ok, qk, v_all, *, batch, seq, tq=512, tk=2048):
    t = batch * seq
    tq, tk = min(tq, seq), min(tk, seq)
    nqb = seq // tq
    gw = GQA_GROUP * HEAD_DIM
    kcb = W_C_Q // HEAD_DIM
    vcb = V_OFF_C // HEAD_DIM
    q_spec = pl.BlockSpec((tq, gw), lambda b, n, i: (b * nqb + i, n))
    v_t = _transposed_columns(v_all, V_OFF_C, W_C_KV)
    return pl.pallas_call(
        functools.partial(_gqa_kernel, seq=seq, tk=tk),
        grid=(batch, GQA_KV_HEADS, nqb),
        in_specs=[_FLAG_SPEC, q_spec,
                  pl.BlockSpec((seq, HEAD_DIM), lambda b, n, i: (b, kcb + n)),
                  pl.BlockSpec((seq, HEAD_DIM), lambda b, n, i: (b, vcb + n)),
                  pl.BlockSpec((HEAD_DIM, seq), lambda b, n, i: (n, b))],
        out_specs=q_spec,
        out_shape=jax.ShapeDtypeStruct((t, W_C_Q), BF16),
        scratch_shapes=[pltpu.VMEM((HEAD_DIM, GQA_GROUP * tq), F32), pltpu.VMEM((8, GQA_GROUP * tq), F32)],
        compiler_params=_params("parallel", "parallel", "parallel"),
        name="gqa_attention",
    )(_plain_flag(plain_ok), qk, qk, v_all, v_t)


def _diff_kernel(plain_ref, q_ref, k_ref, v_ref, lq1_ref, lk1_ref, lq2_ref, lk2_ref, sg_ref, o_ref, *scratch,
                 seq, tk, lam_init):
    tq = q_ref.shape[0]
    dv = 2 * HEAD_DIM
    q1 = q_ref[:, :HEAD_DIM]
    q2 = q_ref[:, HEAD_DIM:]

    def scores(kj):
        return jnp.concatenate([_qk(q1, kj[:, :HEAD_DIM]), _qk(q2, kj[:, HEAD_DIM:])], axis=0)

    def run(online):
        o12 = _attend(scores, k_ref, v_ref, 2 * tq, dv, seq, tk, online, scratch)
        lam = (jnp.exp(jnp.sum(lq1_ref[...] * lk1_ref[...], axis=-1, keepdims=True))
               - jnp.exp(jnp.sum(lq2_ref[...] * lk2_ref[...], axis=-1, keepdims=True)) + lam_init)
        o = o12[:tq] - lam * o12[tq:]
        ms = jnp.mean(o * o, axis=-1, keepdims=True)
        o = o * lax.rsqrt(ms + NORM_EPS) * sg_ref[...] * (1.0 - lam_init)
        o_ref[...] = o.astype(o_ref.dtype)

    _by_softmax_kind(plain_ref, run)


def _diff_attention(plain_ok, qk, v_all, lam_vecs, sub_gain, *, lam_init, batch, seq, tq=512, tk=2048):
    t = batch * seq
    tq, tk = min(tq, seq), min(tk, seq)
    nqb = seq // tq
    pw = 2 * HEAD_DIM
    kcb = W_B_QK // pw
    vcb = V_OFF_B // pw
    vec_spec = pl.BlockSpec((1, HEAD_DIM), lambda b, h, i: (0, 0))
    return pl.pallas_call(
        functools.partial(_diff_kernel, seq=seq, tk=tk, lam_init=lam_init),
        grid=(batch, DIFF_HEADS, nqb),
        in_specs=[
            _FLAG_SPEC,
            pl.BlockSpec((tq, pw), lambda b, h, i: (b * nqb + i, h)),
            pl.BlockSpec((seq, pw), lambda b, h, i: (b, kcb + h)),
            pl.BlockSpec((seq, pw), lambda b, h, i: (b, vcb + h)),
            vec_spec, vec_spec, vec_spec, vec_spec,
            pl.BlockSpec((1, pw), lambda b, h, i: (0, 0)),
        ],
        out_specs=pl.BlockSpec((tq, pw), lambda b, h, i: (b * nqb + i, h)),
        out_shape=jax.ShapeDtypeStruct((t, W_B_V), BF16),
        scratch_shapes=_attend_scratch(2 * tq, pw),
        compiler_params=_params("parallel", "parallel", "parallel"),
        name="diff_attention",
    )(_plain_flag(plain_ok), qk, qk, v_all, *lam_vecs, sub_gain)


NA_Q_ROWS = 4
NA_K_ROWS = NA_Q_ROWS + NA_WIN_R
NA_BLOCK_TYPES = 3
NA_UNROLL = 4


def _na_bias_table(rpb):
    col = jnp.arange(GRID_W)
    col_start = jnp.clip(col - NA_WIN_C // 2, 0, GRID_W - NA_WIN_C)
    col_mask = (col[None, :] >= col_start[:, None]) & (col[None, :] < col_start[:, None] + NA_WIN_C)
    dc = jnp.clip(col[None, :] - col[:, None], -(NA_WIN_C - 1), NA_WIN_C - 1) + NA_WIN_C - 1
    rpb_c = rpb.astype(F32)[:, :, dc] * LOG2E
    rr = jnp.arange(NA_Q_ROWS)
    q_off = jnp.stack([rr, rr + NA_WIN_R // 2, rr + NA_K_ROWS - NA_Q_ROWS])
    w_off = jnp.stack([0 * rr, rr, 0 * rr + NA_K_ROWS - NA_WIN_R])
    i = jnp.arange(NA_K_ROWS)
    dr = i[None, None, :] - q_off[:, :, None] + NA_WIN_R - 1
    row_ok = (i[None, None, :] >= w_off[:, :, None]) & (i[None, None, :] < w_off[:, :, None] + NA_WIN_R)
    tab = rpb_c[:, jnp.clip(dr, 0, 2 * NA_WIN_R - 2)]
    ok = row_ok[None, :, :, :, None, None] & col_mask[None, None, None, None]
    tab = jnp.where(ok, tab, NEG_BIG).transpose(0, 1, 2, 4, 3, 5)
    return tab.reshape(rpb.shape[0], NA_BLOCK_TYPES, NA_Q_ROWS * GRID_W, NA_K_ROWS * GRID_W)


def _na_kernel(plain_ref, q_ref, k_ref, v_ref, bias_ref, o_ref, *, rows):
    nrb = rows // NA_Q_ROWS
    tq = NA_Q_ROWS * GRID_W

    def run(online):
        def row_block(rb, carry):
            ws = jnp.clip(rb * NA_Q_ROWS - NA_WIN_R // 2, 0, rows - NA_K_ROWS)
            start = pl.multiple_of(ws * GRID_W, GRID_W)
            qstart = pl.multiple_of(rb * tq, tq)
            kw = k_ref[pl.ds(start, NA_K_ROWS * GRID_W), :]
            vw = v_ref[pl.ds(start, NA_K_ROWS * GRID_W), :]
            block_type = jnp.where(rb == 0, 0, jnp.where(rb == nrb - 1, 2, 1))
            s = _qk(q_ref[pl.ds(qstart, tq), :], kw) + bias_ref[0, block_type]
            if online:
                s = s - jnp.max(s, axis=-1, keepdims=True)
            p = jnp.exp2(s)
            l = jnp.sum(p, axis=-1, keepdims=True)
            o = jnp.dot(p.astype(vw.dtype), vw, preferred_element_type=F32) / l
            o_ref[pl.ds(qstart, tq), :] = o.astype(o_ref.dtype)
            return carry

        lax.fori_loop(0, nrb, row_block, 0, unroll=NA_UNROLL)

    _by_softmax_kind(plain_ref, run)


def _na_attention(plain_ok, qk, v_all, bias_tab, *, batch, seq):
    t = batch * seq
    rows = seq // GRID_W
    assert rows >= NA_K_ROWS and rows % (NA_UNROLL * NA_Q_ROWS) == 0
    kcb = W_A // HEAD_DIM
    vcb = V_OFF_A // HEAD_DIM
    return pl.pallas_call(
        functools.partial(_na_kernel, rows=rows),
        grid=(batch, NA_HEADS),
        in_specs=[
            _FLAG_SPEC,
            pl.BlockSpec((seq, HEAD_DIM), lambda b, h: (b, h)),
            pl.BlockSpec((seq, HEAD_DIM), lambda b, h: (b, kcb + h)),
            pl.BlockSpec((seq, HEAD_DIM), lambda b, h: (b, vcb + h)),
            pl.BlockSpec((1, NA_BLOCK_TYPES, NA_Q_ROWS * GRID_W, NA_K_ROWS * GRID_W), lambda b, h: (h, 0, 0, 0)),
        ],
        out_specs=pl.BlockSpec((seq, HEAD_DIM), lambda b, h: (b, h)),
        out_shape=jax.ShapeDtypeStruct((t, W_A), BF16),
        compiler_params=_params("parallel", "parallel"),
        name="na_attention",
    )(_plain_flag(plain_ok), qk, qk, v_all, bias_tab)


def _merge_kernel(oa_ref, ob_ref, oc_ref, wa_ref, wb_ref, wc_ref, ga_ref, gb_ref, gc_ref, o_ref):
    def branch(o_r, w_r, g_r):
        y = jnp.dot(o_r[...], w_r[...], preferred_element_type=F32)
        return jax.nn.sigmoid(g_r[...].astype(F32)) * y

    merged = branch(oa_ref, wa_ref, ga_ref) + branch(ob_ref, wb_ref, gb_ref) + branch(oc_ref, wc_ref, gc_ref)
    o_ref[...] = merged.astype(o_ref.dtype)


def _merge(o_a, o_b, o_c, w_oa, w_ob, w_oc, layer, rest, tm=1024, tn=1024):
    t = o_a.shape[0]
    d = w_oa.shape[2]
    tm, tn = min(tm, t), min(tn, d)
    assert d % tn == 0 and G_OFF % tn == 0
    gcb = G_OFF // tn
    ncb = d // tn

    def act(width):
        return pl.BlockSpec((tm, width), lambda j, i: (i, 0))

    def wgt(width):
        return pl.BlockSpec((None, width, tn), lambda j, i: (layer, 0, j))

    def gate(branch):
        return pl.BlockSpec((tm, tn), lambda j, i: (i, gcb + branch * ncb + j))

    return pl.pallas_call(
        _merge_kernel,
        grid=(ncb, t // tm),
        in_specs=[act(W_A), act(W_B_V), act(W_C_Q), wgt(W_A), wgt(W_B_V), wgt(W_C_Q), gate(0), gate(1), gate(2)],
        out_specs=pl.BlockSpec((tm, tn), lambda j, i: (i, j)),
        out_shape=jax.ShapeDtypeStruct((t, d), BF16),
        compiler_params=_params("parallel", "parallel"),
        name="gated_merge",
    )(o_a, o_b, o_c, w_oa, w_ob, w_oc, rest, rest, rest)


def _row(v):
    return v.reshape(1, -1).astype(F32)


def _input_projection(h, w, l, seq, tables):
    rope1, rope2 = tables
    w_in = w["w_in"]
    tn = 1024
    assert REST_OFF == 2 * W_A and REST_OFF % tn == 0

    def gains(q_gain, k_gain, q_heads, k_heads):
        return jnp.concatenate([jnp.tile(q_gain.astype(F32) * QK_SCALE_LOG2, q_heads),
                                jnp.tile(k_gain.astype(F32), k_heads)]).reshape(1, -1)

    qk_a = _matmul(h, w_in, l, kind="headnorm", n=REST_OFF, tn=tn,
                   head_gain=gains(w["qn_a"][l], w["kn_a"][l], NA_HEADS, NA_HEADS), name="proj_qk_a")
    rest = _matmul(h, w_in, l, n=w_in.shape[2] - REST_OFF, tn=tn, col_block0=REST_OFF // tn, name="proj_rest")
    qk_b = _headnorm(rest, OFF_QB - REST_OFF, gains(w["qn_b"][l], w["kn_b"][l], 2 * DIFF_HEADS, 2 * DIFF_HEADS),
                     rope1, seq)
    qk_c = _headnorm(rest, OFF_QC - REST_OFF, gains(w["qn_c"][l], w["kn_c"][l], GQA_Q_HEADS, GQA_KV_HEADS),
                     rope2, seq)
    return qk_a, qk_b, qk_c, rest


def _layer(x, l, batch, seq, w, tables):
    lam_init = 0.8 - 0.6 * math.exp(-0.3 * l)
    dims = dict(batch=batch, seq=seq)
    h = _rmsnorm(x, w["norm_mix"][l])
    qk_a, qk_b, qk_c, rest = _input_projection(h, w, l, seq, tables)

    o_a = _na_attention(_score_bound_ok(w["qn_a"][l], w["kn_a"][l], w["rpb"][l]), qk_a, rest, w["na_bias"][l],
                        **dims)
    lam_vecs = tuple(_row(w[n][l]) for n in ("lam_q1", "lam_k1", "lam_q2", "lam_k2"))
    o_b = _diff_attention(_score_bound_ok(w["qn_b"][l], w["kn_b"][l]), qk_b, rest, lam_vecs, _row(w["subln_b"][l]),
                          lam_init=lam_init, **dims)
    o_c = _gqa_attention(_score_bound_ok(w["qn_c"][l], w["kn_c"][l]), qk_c, rest, **dims)

    merged = _merge(o_a, o_b, o_c, w["w_oa"], w["w_ob"], w["w_oc"], l, rest)
    x, x_bf16, x_sumsq = _matmul(merged, w["w_out"], l, kind="residual_stats", residual=x, out_dtype=F32,
                                 name="proj_out")

    u = _matmul(x_bf16, w["w_up"], l, kind="rownorm_relu2", row_sumsq=x_sumsq, name="mlp_up")
    return _matmul(u, w["w_down"], l, kind="residual", residual=x, out_dtype=F32, name="mlp_down")


_MATMUL_WEIGHTS = ("w_in", "w_oa", "w_ob", "w_oc", "w_out", "w_up", "w_down")


def kernel(x_prompt, x_sample, norm_mix, w_in, qn_a, kn_a, rpb, qn_b, kn_b, lam_q1, lam_k1, lam_q2, lam_k2,
           subln_b, qn_c, kn_c, w_oa, w_ob, w_oc, w_out, norm_mlp, w_up, w_down):
    w = dict(norm_mix=norm_mix, w_in=w_in, qn_a=qn_a, kn_a=kn_a, rpb=rpb, qn_b=qn_b, kn_b=kn_b, lam_q1=lam_q1,
             lam_k1=lam_k1, lam_q2=lam_q2, lam_k2=lam_k2, subln_b=subln_b, qn_c=qn_c, kn_c=kn_c, w_oa=w_oa,
             w_ob=w_ob, w_oc=w_oc, w_out=w_out, norm_mlp=norm_mlp, w_up=w_up, w_down=w_down)
    w["w_up"] = norm_mlp.astype(F32)[:, :, None] * w_up
    for name in _MATMUL_WEIGHTS:
        w[name] = w[name].astype(BF16)
    depth = w_in.shape[0]
    w["na_bias"] = [_na_bias_table(rpb[l]) for l in range(depth)]

    def trunk(x):
        batch, seq, d = x.shape
        tables = _rope_tables(seq)
        y = x.reshape(batch * seq, d)
        for l in range(depth):
            y = _layer(y, l, batch, seq, w, tables)
        return y.reshape(batch, seq, d)

    return (trunk(x_prompt), trunk(x_sample))
```

```python
import functools
import math

import jax
import jax.numpy as jnp
from jax import lax
from jax.experimental import pallas as pl
from jax.experimental.pallas import tpu as pltpu

F32 = jnp.float32
BF16 = jnp.bfloat16

LANES = 128
HEAD_DIM = 128
GRID_W = 64
NA_HEADS = 12
NA_WIN_R = 8
NA_WIN_C = 16
DIFF_HEADS = 4
GQA_Q_HEADS = 12
GQA_KV_HEADS = 4
GQA_GROUP = GQA_Q_HEADS // GQA_KV_HEADS
ROPE_THETA = 10000.0
NORM_EPS = 1e-6

W_A = NA_HEADS * HEAD_DIM
W_B_QK = 2 * DIFF_HEADS * HEAD_DIM
W_B_V = DIFF_HEADS * 2 * HEAD_DIM
W_C_Q = GQA_Q_HEADS * HEAD_DIM
W_C_KV = GQA_KV_HEADS * HEAD_DIM

OFF_QA = 0
OFF_KA = OFF_QA + W_A
OFF_VA = OFF_KA + W_A
OFF_QB = OFF_VA + W_A
OFF_KB = OFF_QB + W_B_QK
OFF_VB = OFF_KB + W_B_QK
OFF_QC = OFF_VB + W_B_V
OFF_KC = OFF_QC + W_C_Q
OFF_VC = OFF_KC + W_C_KV
OFF_GA = OFF_VC + W_C_KV
REST_OFF = OFF_VA
V_OFF_A = OFF_VA - REST_OFF
V_OFF_B = OFF_VB - REST_OFF
V_OFF_C = OFF_VC - REST_OFF
G_OFF = OFF_GA - REST_OFF

V7X_VMEM_LIMIT_BYTES = 62 * 1024 * 1024

LOG2E = 1.4426950408889634
QK_SCALE_LOG2 = HEAD_DIM ** -0.5 * LOG2E
NEG_BIG = -1e30
PLAIN_SOFTMAX_LOG2_LIMIT = 64.0


def _params(*semantics):
    return pltpu.CompilerParams(dimension_semantics=semantics, vmem_limit_bytes=V7X_VMEM_LIMIT_BYTES)


def _rmsnorm_kernel(x_ref, g_ref, o_ref):
    x = x_ref[...]
    ms = jnp.mean(x * x, axis=-1, keepdims=True)
    o_ref[...] = (x * lax.rsqrt(ms + NORM_EPS) * g_ref[...]).astype(o_ref.dtype)


def _rmsnorm(x, gain, tm=512):
    t, d = x.shape
    tm = min(tm, t)
    return pl.pallas_call(
        _rmsnorm_kernel,
        grid=(t // tm,),
        in_specs=[pl.BlockSpec((tm, d), lambda i: (i, 0)), pl.BlockSpec((1, d), lambda i: (0, 0))],
        out_specs=pl.BlockSpec((tm, d), lambda i: (i, 0)),
        out_shape=jax.ShapeDtypeStruct((t, d), BF16),
        compiler_params=_params("parallel"),
        name="rmsnorm",
    )(x, gain.reshape(1, d).astype(F32))


def _store_normed_heads(x, g_ref, o_ref, rope_refs=None, rot_half=0):
    for h in range(x.shape[1] // HEAD_DIM):
        cols = slice(h * HEAD_DIM, (h + 1) * HEAD_DIM)
        xh = x[:, cols]
        ms = jnp.mean(xh * xh, axis=-1, keepdims=True)
        y = xh * lax.rsqrt(ms + NORM_EPS) * g_ref[:, cols]
        if rope_refs is not None:
            cos_ref, sin_ref = rope_refs
            y = y * cos_ref[...] + _rotate_pairs(y, rot_half) * sin_ref[...]
        o_ref[:, cols] = y.astype(o_ref.dtype)


def _matmul_kernel(*refs, kind, nk, kdim):
    a_ref, b_ref = refs[:2]
    prod = jnp.dot(a_ref[...], b_ref[...], preferred_element_type=F32)
    if kind == "residual":
        res_ref, o_ref = refs[2:]
        if nk == 1:
            o_ref[...] = prod + res_ref[...]
        else:
            o_ref[...] = prod + jnp.where(pl.program_id(2) == 0, res_ref[...], o_ref[...])
    elif kind == "residual_stats":
        res_ref, o_ref, ob_ref, ss_ref = refs[2:]
        x = prod + res_ref[...]
        o_ref[...] = x
        ob_ref[...] = x.astype(ob_ref.dtype)
        ss_ref[...] = _lane_fold(x * x)
    elif kind == "rownorm_relu2":
        ss_ref, o_ref = refs[2:]
        ms = jnp.sum(ss_ref[...], axis=-1, keepdims=True) * (1.0 / kdim)
        o_ref[...] = jnp.square(jnp.maximum(prod * lax.rsqrt(ms + NORM_EPS), 0.0)).astype(o_ref.dtype)
    elif kind == "headnorm":
        g_ref, o_ref = refs[2:]
        _store_normed_heads(prod, g_ref, o_ref)
    else:
        (o_ref,) = refs[2:]
        o_ref[...] = prod.astype(o_ref.dtype)


def _matmul(a, w, layer, *, kind="plain", residual=None, row_sumsq=None, head_gain=None,
            n=None, col_block0=0, out_dtype=BF16, tm=1024, tn=1024, tk=4096, name="matmul"):
    m, kdim = a.shape
    n = w.shape[2] if n is None else n
    tm, tn, tk = min(tm, m), min(tn, n), min(tk, kdim)
    nk = kdim // tk
    tile = pl.BlockSpec((tm, tn), lambda i, j, k: (i, j))
    in_specs = [pl.BlockSpec((tm, tk), lambda i, j, k: (i, k)),
                pl.BlockSpec((None, tk, tn), lambda i, j, k: (layer, k, col_block0 + j))]
    args = [a, w]
    out_specs, out_shape = tile, jax.ShapeDtypeStruct((m, n), out_dtype)
    if kind == "headnorm":
        in_specs.append(pl.BlockSpec((1, tn), lambda i, j, k: (0, j)))
        args.append(head_gain)
    if kind in ("residual", "residual_stats"):
        in_specs.append(tile)
        args.append(residual)
    if kind == "residual_stats":
        out_specs = (tile, tile, pl.BlockSpec((tm, LANES), lambda i, j, k: (i, j)))
        out_shape = (out_shape, jax.ShapeDtypeStruct((m, n), BF16),
                     jax.ShapeDtypeStruct((m, n // tn * LANES), F32))
    if kind == "rownorm_relu2":
        in_specs.append(pl.BlockSpec((tm, row_sumsq.shape[1]), lambda i, j, k: (i, 0)))
        args.append(row_sumsq)
    assert nk == 1 or (kind == "residual" and out_dtype == F32)
    return pl.pallas_call(
        functools.partial(_matmul_kernel, kind=kind, nk=nk, kdim=kdim),
        grid=(m // tm, n // tn, nk),
        in_specs=in_specs,
        out_specs=out_specs,
        out_shape=out_shape,
        compiler_params=_params("parallel", "parallel", "arbitrary"),
        name=name,
    )(*args)


def _rotate_pairs(y, half):
    width = y.shape[-1]
    if 2 * half == width:
        return pltpu.roll(y, half, 1)
    lane = lax.broadcasted_iota(jnp.int32, y.shape, 1)
    first = (lane % (2 * half)) < half
    return jnp.where(first, pltpu.roll(y, width - half, 1), pltpu.roll(y, half, 1))


def _headnorm_kernel(x_ref, g_ref, cos_ref, sin_ref, o_ref, *, rot_half):
    _store_normed_heads(x_ref[...].astype(F32), g_ref, o_ref, (cos_ref, sin_ref), rot_half)


def _headnorm(x, col_off, head_gain, rope, seq, tm=1024, bw=512):
    t = x.shape[0]
    width = head_gain.shape[1]
    tm = min(tm, seq)
    assert col_off % bw == 0 and width % bw == 0 and seq % tm == 0
    cb = col_off // bw
    cos, sin, rot_half = rope
    nsb = seq // tm
    tab_spec = pl.BlockSpec((tm, HEAD_DIM), lambda i, j: (i % nsb, 0))
    return pl.pallas_call(
        functools.partial(_headnorm_kernel, rot_half=rot_half),
        grid=(t // tm, width // bw),
        in_specs=[pl.BlockSpec((tm, bw), lambda i, j: (i, cb + j)), pl.BlockSpec((1, bw), lambda i, j: (0, j)),
                  tab_spec, tab_spec],
        out_specs=pl.BlockSpec((tm, bw), lambda i, j: (i, j)),
        out_shape=jax.ShapeDtypeStruct((t, width), BF16),
        compiler_params=_params("parallel", "parallel"),
        name="headnorm",
    )(x, head_gain, cos, sin)


def _rope_tables(seq):
    pos = jnp.arange(seq, dtype=jnp.int32)

    def angles(p, dr):
        inv = ROPE_THETA ** (-jnp.arange(0, dr, 2, dtype=F32) / dr)
        return p.astype(F32)[:, None] * inv[None, :]

    a1 = angles(pos, HEAD_DIM)
    cos1 = jnp.concatenate([jnp.cos(a1), jnp.cos(a1)], axis=-1)
    sin1 = jnp.concatenate([-jnp.sin(a1), jnp.sin(a1)], axis=-1)
    ar = angles(pos // GRID_W, HEAD_DIM // 2)
    ac = angles(pos % GRID_W, HEAD_DIM // 2)
    cos2 = jnp.concatenate([jnp.cos(ar), jnp.cos(ar), jnp.cos(ac), jnp.cos(ac)], axis=-1)
    sin2 = jnp.concatenate([-jnp.sin(ar), jnp.sin(ar), -jnp.sin(ac), jnp.sin(ac)], axis=-1)
    return (cos1, sin1, HEAD_DIM // 2), (cos2, sin2, HEAD_DIM // 4)


def _qk(q, k):
    return lax.dot_general(q, k, (((1,), (1,)), ((), ())), preferred_element_type=F32)


def _score_bound_ok(q_gain, k_gain, bias=None):
    bound = HEAD_DIM * QK_SCALE_LOG2 * jnp.max(jnp.abs(q_gain)) * jnp.max(jnp.abs(k_gain))
    if bias is not None:
        bound = bound + LOG2E * jnp.max(jnp.abs(bias))
    return bound <= PLAIN_SOFTMAX_LOG2_LIMIT


def _lane_fold(p):
    out = p[:, :LANES]
    for c in range(1, p.shape[1] // LANES):
        out = out + p[:, c * LANES:(c + 1) * LANES]
    return out


def _online_step(s, vj, m, l, acc):
    m_new = jnp.maximum(m, jnp.max(s, axis=-1, keepdims=True))
    alpha = jnp.exp2(m - m_new)
    p = jnp.exp2(s - m_new)
    l_new = alpha * l + jnp.sum(p, axis=-1, keepdims=True)
    acc_new = alpha * acc + jnp.dot(p.astype(vj.dtype), vj, preferred_element_type=F32)
    return m_new, l_new, acc_new


def _attend(score_fn, k_ref, v_ref, rows, dv, seq, tk, online, scratch):
    def chunk(j):
        start = pl.multiple_of(j * tk, tk)
        return k_ref[pl.ds(start, tk), :], v_ref[pl.ds(start, tk), :]

    if online:
        def body(j, carry):
            kj, vj = chunk(j)
            return _online_step(score_fn(kj), vj, *carry)

        init = (jnp.full((rows, 1), -jnp.inf, F32), jnp.zeros((rows, 1), F32), jnp.zeros((rows, dv), F32))
        _, l, acc = lax.fori_loop(0, seq // tk, body, init)
        return acc / l

    acc_ref, l_ref = scratch
    acc_ref[...] = jnp.zeros_like(acc_ref)
    l_ref[...] = jnp.zeros_like(l_ref)

    def body(j, carry):
        kj, vj = chunk(j)
        p = jnp.exp2(score_fn(kj))
        l_ref[...] += _lane_fold(p)
        acc_ref[...] = jnp.dot(p.astype(vj.dtype), vj, preferred_element_type=F32) + acc_ref[...]
        return carry

    lax.fori_loop(0, seq // tk, body, 0)
    return acc_ref[...] / jnp.sum(l_ref[...], axis=-1, keepdims=True)


def _attend_scratch(rows, dv):
    return [pltpu.VMEM((rows, dv), F32), pltpu.VMEM((rows, LANES), F32)]


def _by_softmax_kind(plain_ref, run):
    @pl.when(plain_ref[0] != 0)
    def _():
        run(False)

    @pl.when(plain_ref[0] == 0)
    def _():
        run(True)


def _plain_flag(plain_ok):
    return plain_ok.astype(jnp.int32).reshape(1)


_FLAG_SPEC = pl.BlockSpec(memory_space=pltpu.SMEM)


def _transpose_kernel(x_ref, o_ref):
    o_ref[...] = x_ref[...].T


def _transposed_columns(x, col_off, width, tm=1024):
    t = x.shape[0]
    tm = min(tm, t)
    assert col_off % width == 0 and t % tm == 0
    cb = col_off // width
    return pl.pallas_call(
        _transpose_kernel,
        grid=(t // tm,),
        in_specs=[pl.BlockSpec((tm, width), lambda i: (i, cb))],
        out_specs=pl.BlockSpec((width, tm), lambda i: (0, i)),
        out_shape=jax.ShapeDtypeStruct((width, t), x.dtype),
        compiler_params=_params("parallel"),
        name="transpose_cols",
    )(x)


def _gqa_kernel(plain_ref, q_ref, k_ref, v_ref, vt_ref, o_ref, acc_ref, l_ref, *, seq, tk):
    tq = q_ref.shape[0]
    m = GQA_GROUP * tq
    q = jnp.concatenate([q_ref[:, g * HEAD_DIM:(g + 1) * HEAD_DIM] for g in range(GQA_GROUP)], axis=0)

    def store(o):
        for g in range(GQA_GROUP):
            o_ref[:, g * HEAD_DIM:(g + 1) * HEAD_DIM] = o[g * tq:(g + 1) * tq].astype(o_ref.dtype)

    def run(online):
        if online:
            store(_attend(lambda kj: _qk(q, kj), k_ref, v_ref, m, HEAD_DIM, seq, tk, True, None))
            return
        acc_ref[...] = jnp.zeros_like(acc_ref)
        l_ref[...] = jnp.zeros_like(l_ref)

        def body(j, carry):
            start = pl.multiple_of(j * tk, tk)
            pt = jnp.exp2(_qk(k_ref[pl.ds(start, tk), :], q))
            l_ref[...] += pt.reshape(tk // 8, 8, m).sum(axis=0)
            acc_ref[...] = (jnp.dot(vt_ref[:, pl.ds(start, tk)], pt.astype(BF16), preferred_element_type=F32)
                            + acc_ref[...])
            return carry

        lax.fori_loop(0, seq // tk, body, 0)
        store((acc_ref[...] / jnp.sum(l_ref[...], axis=0, keepdims=True)).T)

    _by_softmax_kind(plain_ref, run)


def _gqa_attention(plain_ok, qk, v_all, *, batch, seq, tq=512, tk=2048):
    t = batch * seq
    tq, tk = min(tq, seq), min(tk, seq)
    nqb = seq // tq
    gw = GQA_GROUP * HEAD_DIM
    kcb = W_C_Q // HEAD_DIM
    vcb = V_OFF_C // HEAD_DIM
    q_spec = pl.BlockSpec((tq, gw), lambda b, n, i: (b * nqb + i, n))
    v_t = _transposed_columns(v_all, V_OFF_C, W_C_KV)
    return pl.pallas_call(
        functools.partial(_gqa_kernel, seq=seq, tk=tk),
        grid=(batch, GQA_KV_HEADS, nqb),
        in_specs=[_FLAG_SPEC, q_spec,
                  pl.BlockSpec((seq, HEAD_DIM), lambda b, n, i: (b, kcb + n)),
                  pl.BlockSpec((seq, HEAD_DIM), lambda b, n, i: (b, vcb + n)),
                  pl.BlockSpec((HEAD_DIM, seq), lambda b, n, i: (n, b))],
        out_specs=q_spec,
        out_shape=jax.ShapeDtypeStruct((t, W_C_Q), BF16),
        scratch_shapes=[pltpu.VMEM((HEAD_DIM, GQA_GROUP * tq), F32), pltpu.VMEM((8, GQA_GROUP * tq), F32)],
        compiler_params=_params("parallel", "parallel", "parallel"),
        name="gqa_attention",
    )(_plain_flag(plain_ok), qk, qk, v_all, v_t)


def _diff_kernel(plain_ref, q_ref, k_ref, v_ref, lq1_ref, lk1_ref, lq2_ref, lk2_ref, sg_ref, o_ref, *scratch,
                 seq, tk, lam_init):
    tq = q_ref.shape[0]
    dv = 2 * HEAD_DIM
    q1 = q_ref[:, :HEAD_DIM]
    q2 = q_ref[:, HEAD_DIM:]

    def scores(kj):
        return jnp.concatenate([_qk(q1, kj[:, :HEAD_DIM]), _qk(q2, kj[:, HEAD_DIM:])], axis=0)

    def run(online):
        o12 = _attend(scores, k_ref, v_ref, 2 * tq, dv, seq, tk, online, scratch)
        lam = (jnp.exp(jnp.sum(lq1_ref[...] * lk1_ref[...], axis=-1, keepdims=True))
               - jnp.exp(jnp.sum(lq2_ref[...] * lk2_ref[...], axis=-1, keepdims=True)) + lam_init)
        o = o12[:tq] - lam * o12[tq:]
        ms = jnp.mean(o * o, axis=-1, keepdims=True)
        o = o * lax.rsqrt(ms + NORM_EPS) * sg_ref[...] * (1.0 - lam_init)
        o_ref[...] = o.astype(o_ref.dtype)

    _by_softmax_kind(plain_ref, run)


def _diff_attention(plain_ok, qk, v_all, lam_vecs, sub_gain, *, lam_init, batch, seq, tq=512, tk=2048):
    t = batch * seq
    tq, tk = min(tq, seq), min(tk, seq)
    nqb = seq // tq
    pw = 2 * HEAD_DIM
    kcb = W_B_QK // pw
    vcb = V_OFF_B // pw
    vec_spec = pl.BlockSpec((1, HEAD_DIM), lambda b, h, i: (0, 0))
    return pl.pallas_call(
        functools.partial(_diff_kernel, seq=seq, tk=tk, lam_init=lam_init),
        grid=(batch, DIFF_HEADS, nqb),
        in_specs=[
            _FLAG_SPEC,
            pl.BlockSpec((tq, pw), lambda b, h, i: (b * nqb + i, h)),
            pl.BlockSpec((seq, pw), lambda b, h, i: (b, kcb + h)),
            pl.BlockSpec((seq, pw), lambda b, h, i: (b, vcb + h)),
            vec_spec, vec_spec, vec_spec, vec_spec,
            pl.BlockSpec((1, pw), lambda b, h, i: (0, 0)),
        ],
        out_specs=pl.BlockSpec((tq, pw), lambda b, h, i: (b * nqb + i, h)),
        out_shape=jax.ShapeDtypeStruct((t, W_B_V), BF16),
        scratch_shapes=_attend_scratch(2 * tq, pw),
        compiler_params=_params("parallel", "parallel", "parallel"),
        name="diff_attention",
    )(_plain_flag(plain_ok), qk, qk, v_all, *lam_vecs, sub_gain)


NA_Q_ROWS = 4
NA_K_ROWS = NA_Q_ROWS + NA_WIN_R
NA_BLOCK_TYPES = 3
NA_UNROLL = 4


def _na_bias_table(rpb):
    m = jnp.arange(LANES)
    delta = jnp.where(m < GRID_W, m, m - LANES)
    by_delta = rpb.astype(F32)[:, :, jnp.clip(delta, -(NA_WIN_C - 1), NA_WIN_C - 1) + NA_WIN_C - 1] * LOG2E
    heads = rpb.shape[0]
    return pl.pallas_call(
        _na_bias_kernel,
        grid=(heads,),
        in_specs=[pl.BlockSpec((1, 2 * NA_WIN_R - 1, LANES), lambda h: (h, 0, 0))],
        out_specs=pl.BlockSpec((1, NA_BLOCK_TYPES, NA_Q_ROWS * GRID_W, NA_K_ROWS * GRID_W), lambda h: (h, 0, 0, 0)),
        out_shape=jax.ShapeDtypeStruct((heads, NA_BLOCK_TYPES, NA_Q_ROWS * GRID_W, NA_K_ROWS * GRID_W), F32),
        compiler_params=_params("parallel"),
        name="na_bias_table",
    )(by_delta)


def _na_bias_kernel(by_delta_ref, o_ref):
    q = lax.broadcasted_iota(jnp.int32, (GRID_W, GRID_W), 0)
    k = lax.broadcasted_iota(jnp.int32, (GRID_W, GRID_W), 1)
    col_start = jnp.clip(q - NA_WIN_C // 2, 0, GRID_W - NA_WIN_C)
    in_window = (k >= col_start) & (k < col_start + NA_WIN_C)
    masked = jnp.full((GRID_W, GRID_W), NEG_BIG, F32)

    def block(d):
        rows = jnp.broadcast_to(by_delta_ref[0, d:d + 1, :], (GRID_W, LANES))
        toeplitz = pltpu.roll(rows, 0, 1, stride=1, stride_axis=0)
        return jnp.where(in_window, toeplitz[:, :GRID_W], masked)

    by_offset = [block(d) for d in range(2 * NA_WIN_R - 1)]
    for block_type in range(NA_BLOCK_TYPES):
        for rr in range(NA_Q_ROWS):
            q_off, w_off = ((rr, 0), (rr + NA_WIN_R // 2, rr),
                            (rr + NA_K_ROWS - NA_Q_ROWS, NA_K_ROWS - NA_WIN_R))[block_type]
            strip = [by_offset[i - q_off + NA_WIN_R - 1] if w_off <= i < w_off + NA_WIN_R else masked
                     for i in range(NA_K_ROWS)]
            o_ref[0, block_type, rr * GRID_W:(rr + 1) * GRID_W, :] = jnp.concatenate(strip, axis=1)


def _na_kernel(plain_ref, q_ref, k_ref, v_ref, bias_ref, o_ref, *, rows):
    nrb = rows // NA_Q_ROWS
    tq = NA_Q_ROWS * GRID_W

    def run(online):
        def row_block(rb, carry):
            ws = jnp.clip(rb * NA_Q_ROWS - NA_WIN_R // 2, 0, rows - NA_K_ROWS)
            start = pl.multiple_of(ws * GRID_W, GRID_W)
            qstart = pl.multiple_of(rb * tq, tq)
            kw = k_ref[pl.ds(start, NA_K_ROWS * GRID_W), :]
            vw = v_ref[pl.ds(start, NA_K_ROWS * GRID_W), :]
            block_type = jnp.where(rb == 0, 0, jnp.where(rb == nrb - 1, 2, 1))
            s = _qk(q_ref[pl.ds(qstart, tq), :], kw) + bias_ref[0, block_type]
            if online:
                s = s - jnp.max(s, axis=-1, keepdims=True)
            p = jnp.exp2(s)
            l = jnp.sum(p, axis=-1, keepdims=True)
            o = jnp.dot(p.astype(vw.dtype), vw, preferred_element_type=F32) / l
            o_ref[pl.ds(qstart, tq), :] = o.astype(o_ref.dtype)
            return carry

        lax.fori_loop(0, nrb, row_block, 0, unroll=NA_UNROLL)

    _by_softmax_kind(plain_ref, run)


def _na_attention(plain_ok, qk, v_all, bias_tab, *, batch, seq):
    t = batch * seq
    rows = seq // GRID_W
    assert rows >= NA_K_ROWS and rows % (NA_UNROLL * NA_Q_ROWS) == 0
    kcb = W_A // HEAD_DIM
    vcb = V_OFF_A // HEAD_DIM
    return pl.pallas_call(
        functools.partial(_na_kernel, rows=rows),
        grid=(batch, NA_HEADS),
        in_specs=[
            _FLAG_SPEC,
            pl.BlockSpec((seq, HEAD_DIM), lambda b, h: (b, h)),
            pl.BlockSpec((seq, HEAD_DIM), lambda b, h: (b, kcb + h)),
            pl.BlockSpec((seq, HEAD_DIM), lambda b, h: (b, vcb + h)),
            pl.BlockSpec((1, NA_BLOCK_TYPES, NA_Q_ROWS * GRID_W, NA_K_ROWS * GRID_W), lambda b, h: (h, 0, 0, 0)),
        ],
        out_specs=pl.BlockSpec((seq, HEAD_DIM), lambda b, h: (b, h)),
        out_shape=jax.ShapeDtypeStruct((t, W_A), BF16),
        compiler_params=_params("parallel", "parallel"),
        name="na_attention",
    )(_plain_flag(plain_ok), qk, qk, v_all, bias_tab)


def _merge_kernel(oa_ref, ob_ref, oc_ref, wa_ref, wb_ref, wc_ref, ga_ref, gb_ref, gc_ref, o_ref):
    def branch(o_r, w_r, g_r):
        y = jnp.dot(o_r[...], w_r[...], preferred_element_type=F32)
        return jax.nn.sigmoid(g_r[...].astype(F32)) * y

    merged = branch(oa_ref, wa_ref, ga_ref) + branch(ob_ref, wb_ref, gb_ref) + branch(oc_ref, wc_ref, gc_ref)
    o_ref[...] = merged.astype(o_ref.dtype)


def _merge(o_a, o_b, o_c, w_oa, w_ob, w_oc, layer, rest, tm=1024, tn=1024):
    t = o_a.shape[0]
    d = w_oa.shape[2]
    tm, tn = min(tm, t), min(tn, d)
    assert d % tn == 0 and G_OFF % tn == 0
    gcb = G_OFF // tn
    ncb = d // tn

    def act(width):
        return pl.BlockSpec((tm, width), lambda j, i: (i, 0))

    def wgt(width):
        return pl.BlockSpec((None, width, tn), lambda j, i: (layer, 0, j))

    def gate(branch):
        return pl.BlockSpec((tm, tn), lambda j, i: (i, gcb + branch * ncb + j))

    return pl.pallas_call(
        _merge_kernel,
        grid=(ncb, t // tm),
        in_specs=[act(W_A), act(W_B_V), act(W_C_Q), wgt(W_A), wgt(W_B_V), wgt(W_C_Q), gate(0), gate(1), gate(2)],
        out_specs=pl.BlockSpec((tm, tn), lambda j, i: (i, j)),
        out_shape=jax.ShapeDtypeStruct((t, d), BF16),
        compiler_params=_params("parallel", "parallel"),
        name="gated_merge",
    )(o_a, o_b, o_c, w_oa, w_ob, w_oc, rest, rest, rest)


def _row(v):
    return v.reshape(1, -1).astype(F32)


def _input_projection(h, w, l, seq, tables):
    rope1, rope2 = tables
    w_in = w["w_in"]
    tn = 1024
    assert REST_OFF == 2 * W_A and REST_OFF % tn == 0

    def gains(q_gain, k_gain, q_heads, k_heads):
        return jnp.concatenate([jnp.tile(q_gain.astype(F32) * QK_SCALE_LOG2, q_heads),
                                jnp.tile(k_gain.astype(F32), k_heads)]).reshape(1, -1)

    qk_a = _matmul(h, w_in, l, kind="headnorm", n=REST_OFF, tn=tn,
                   head_gain=gains(w["qn_a"][l], w["kn_a"][l], NA_HEADS, NA_HEADS), name="proj_qk_a")
    rest = _matmul(h, w_in, l, n=w_in.shape[2] - REST_OFF, tn=tn, col_block0=REST_OFF // tn, name="proj_rest")
    qk_b = _headnorm(rest, OFF_QB - REST_OFF, gains(w["qn_b"][l], w["kn_b"][l], 2 * DIFF_HEADS, 2 * DIFF_HEADS),
                     rope1, seq)
    qk_c = _headnorm(rest, OFF_QC - REST_OFF, gains(w["qn_c"][l], w["kn_c"][l], GQA_Q_HEADS, GQA_KV_HEADS),
                     rope2, seq)
    return qk_a, qk_b, qk_c, rest


def _layer(x, l, batch, seq, w, tables):
    lam_init = 0.8 - 0.6 * math.exp(-0.3 * l)
    dims = dict(batch=batch, seq=seq)
    h = _rmsnorm(x, w["norm_mix"][l])
    qk_a, qk_b, qk_c, rest = _input_projection(h, w, l, seq, tables)

    o_a = _na_attention(_score_bound_ok(w["qn_a"][l], w["kn_a"][l], w["rpb"][l]), qk_a, rest, w["na_bias"][l],
                        **dims)
    lam_vecs = tuple(_row(w[n][l]) for n in ("lam_q1", "lam_k1", "lam_q2", "lam_k2"))
    o_b = _diff_attention(_score_bound_ok(w["qn_b"][l], w["kn_b"][l]), qk_b, rest, lam_vecs, _row(w["subln_b"][l]),
                          lam_init=lam_init, **dims)
    o_c = _gqa_attention(_score_bound_ok(w["qn_c"][l], w["kn_c"][l]), qk_c, rest, **dims)

    merged = _merge(o_a, o_b, o_c, w["w_oa"], w["w_ob"], w["w_oc"], l, rest)
    x, x_bf16, x_sumsq = _matmul(merged, w["w_out"], l, kind="residual_stats", residual=x, out_dtype=F32,
                                 name="proj_out")

    u = _matmul(x_bf16, w["w_up"], l, kind="rownorm_relu2", row_sumsq=x_sumsq, name="mlp_up")
    return _matmul(u, w["w_down"], l, kind="residual", residual=x, out_dtype=F32, name="mlp_down")


_MATMUL_WEIGHTS = ("w_in", "w_oa", "w_ob", "w_oc", "w_out", "w_up", "w_down")


def kernel(x_prompt, x_sample, norm_mix, w_in, qn_a, kn_a, rpb, qn_b, kn_b, lam_q1, lam_k1, lam_q2, lam_k2,
           subln_b, qn_c, kn_c, w_oa, w_ob, w_oc, w_out, norm_mlp, w_up, w_down):
    w = dict(norm_mix=norm_mix, w_in=w_in, qn_a=qn_a, kn_a=kn_a, rpb=rpb, qn_b=qn_b, kn_b=kn_b, lam_q1=lam_q1,
             lam_k1=lam_k1, lam_q2=lam_q2, lam_k2=lam_k2, subln_b=subln_b, qn_c=qn_c, kn_c=kn_c, w_oa=w_oa,
             w_ob=w_ob, w_oc=w_oc, w_out=w_out, norm_mlp=norm_mlp, w_up=w_up, w_down=w_down)
    w["w_up"] = norm_mlp.astype(F32)[:, :, None] * w_up
    for name in _MATMUL_WEIGHTS:
        w[name] = w[name].astype(BF16)
    depth = w_in.shape[0]
    w["na_bias"] = [_na_bias_table(rpb[l]) for l in range(depth)]

    def trunk(x):
        batch, seq, d = x.shape
        tables = _rope_tables(seq)
        y = x.reshape(batch * seq, d)
        for l in range(depth):
            y = _layer(y, l, batch, seq, w, tables)
        return y.reshape(batch, seq, d)

    return (trunk(x_prompt), trunk(x_sample))
```

```python
import functools
import math

import jax
import jax.numpy as jnp
from jax import lax
from jax.experimental import pallas as pl
from jax.experimental.pallas import tpu as pltpu

F32 = jnp.float32
BF16 = jnp.bfloat16

LANES = 128
HEAD_DIM = 128
GRID_W = 64
NA_HEADS = 12
NA_WIN_R = 8
NA_WIN_C = 16
DIFF_HEADS = 4
GQA_Q_HEADS = 12
GQA_KV_HEADS = 4
GQA_GROUP = GQA_Q_HEADS // GQA_KV_HEADS
ROPE_THETA = 10000.0
NORM_EPS = 1e-6

W_A = NA_HEADS * HEAD_DIM
W_B_QK = 2 * DIFF_HEADS * HEAD_DIM
W_B_V = DIFF_HEADS * 2 * HEAD_DIM
W_C_Q = GQA_Q_HEADS * HEAD_DIM
W_C_KV = GQA_KV_HEADS * HEAD_DIM

OFF_QA = 0
OFF_KA = OFF_QA + W_A
OFF_VA = OFF_KA + W_A
OFF_QB = OFF_VA + W_A
OFF_KB = OFF_QB + W_B_QK
OFF_VB = OFF_KB + W_B_QK
OFF_QC = OFF_VB + W_B_V
OFF_KC = OFF_QC + W_C_Q
OFF_VC = OFF_KC + W_C_KV
OFF_GA = OFF_VC + W_C_KV
REST_OFF = OFF_VA
V_OFF_A = OFF_VA - REST_OFF
V_OFF_B = OFF_VB - REST_OFF
V_OFF_C = OFF_VC - REST_OFF
G_OFF = OFF_GA - REST_OFF

V7X_VMEM_LIMIT_BYTES = 62 * 1024 * 1024

LOG2E = 1.4426950408889634
QK_SCALE_LOG2 = HEAD_DIM ** -0.5 * LOG2E
NEG_BIG = -1e30
PLAIN_SOFTMAX_LOG2_LIMIT = 64.0


def _params(*semantics):
    return pltpu.CompilerParams(dimension_semantics=semantics, vmem_limit_bytes=V7X_VMEM_LIMIT_BYTES)


def _rmsnorm_kernel(x_ref, g_ref, o_ref):
    x = x_ref[...]
    ms = jnp.mean(x * x, axis=-1, keepdims=True)
    o_ref[...] = (x * lax.rsqrt(ms + NORM_EPS) * g_ref[...]).astype(o_ref.dtype)


def _rmsnorm(x, gain, tm=512):
    t, d = x.shape
    tm = min(tm, t)
    return pl.pallas_call(
        _rmsnorm_kernel,
        grid=(t // tm,),
        in_specs=[pl.BlockSpec((tm, d), lambda i: (i, 0)), pl.BlockSpec((1, d), lambda i: (0, 0))],
        out_specs=pl.BlockSpec((tm, d), lambda i: (i, 0)),
        out_shape=jax.ShapeDtypeStruct((t, d), BF16),
        compiler_params=_params("parallel"),
        name="rmsnorm",
    )(x, gain.reshape(1, d).astype(F32))


def _store_normed_heads(x, g_ref, o_ref, rope_refs=None, rot_half=0, mean_on_mxu=False):
    averager = jnp.full((HEAD_DIM, HEAD_DIM), 1.0 / HEAD_DIM, BF16)
    for h in range(x.shape[1] // HEAD_DIM):
        cols = slice(h * HEAD_DIM, (h + 1) * HEAD_DIM)
        xh = x[:, cols]
        if mean_on_mxu:
            ms = jnp.dot((xh * xh).astype(BF16), averager, preferred_element_type=F32)
        else:
            ms = jnp.mean(xh * xh, axis=-1, keepdims=True)
        y = xh * lax.rsqrt(ms + NORM_EPS) * g_ref[:, cols]
        if rope_refs is not None:
            cos_ref, sin_ref = rope_refs
            y = y * cos_ref[...] + _rotate_pairs(y, rot_half) * sin_ref[...]
        o_ref[:, cols] = y.astype(o_ref.dtype)


def _matmul_kernel(*refs, kind, nk, kdim):
    a_ref, b_ref = refs[:2]
    prod = jnp.dot(a_ref[...], b_ref[...], preferred_element_type=F32)
    if kind == "residual":
        res_ref, o_ref = refs[2:]
        if nk == 1:
            o_ref[...] = prod + res_ref[...]
        else:
            o_ref[...] = prod + jnp.where(pl.program_id(2) == 0, res_ref[...], o_ref[...])
    elif kind == "residual_stats":
        res_ref, o_ref, ob_ref, ss_ref = refs[2:]
        x = prod + res_ref[...]
        o_ref[...] = x
        ob_ref[...] = x.astype(ob_ref.dtype)
        ss_ref[...] = _lane_fold(x * x)
    elif kind == "rownorm_relu2":
        ss_ref, o_ref = refs[2:]
        ms = jnp.sum(ss_ref[...], axis=-1, keepdims=True) * (1.0 / kdim)
        o_ref[...] = jnp.square(jnp.maximum(prod * lax.rsqrt(ms + NORM_EPS), 0.0)).astype(o_ref.dtype)
    elif kind == "headnorm":
        g_ref, o_ref = refs[2:]
        _store_normed_heads(prod, g_ref, o_ref)
    else:
        (o_ref,) = refs[2:]
        o_ref[...] = prod.astype(o_ref.dtype)


def _matmul(a, w, layer, *, kind="plain", residual=None, row_sumsq=None, head_gain=None,
            n=None, col_block0=0, out_dtype=BF16, tm=1024, tn=1024, tk=4096, name="matmul"):
    m, kdim = a.shape
    n = w.shape[2] if n is None else n
    tm, tn, tk = min(tm, m), min(tn, n), min(tk, kdim)
    nk = kdim // tk
    tile = pl.BlockSpec((tm, tn), lambda i, j, k: (i, j))
    in_specs = [pl.BlockSpec((tm, tk), lambda i, j, k: (i, k)),
                pl.BlockSpec((None, tk, tn), lambda i, j, k: (layer, k, col_block0 + j))]
    args = [a, w]
    out_specs, out_shape = tile, jax.ShapeDtypeStruct((m, n), out_dtype)
    if kind == "headnorm":
        in_specs.append(pl.BlockSpec((1, tn), lambda i, j, k: (0, j)))
        args.append(head_gain)
    if kind in ("residual", "residual_stats"):
        in_specs.append(tile)
        args.append(residual)
    if kind == "residual_stats":
        out_specs = (tile, tile, pl.BlockSpec((tm, LANES), lambda i, j, k: (i, j)))
        out_shape = (out_shape, jax.ShapeDtypeStruct((m, n), BF16),
                     jax.ShapeDtypeStruct((m, n // tn * LANES), F32))
    if kind == "rownorm_relu2":
        in_specs.append(pl.BlockSpec((tm, row_sumsq.shape[1]), lambda i, j, k: (i, 0)))
        args.append(row_sumsq)
    assert nk == 1 or (kind == "residual" and out_dtype == F32)
    return pl.pallas_call(
        functools.partial(_matmul_kernel, kind=kind, nk=nk, kdim=kdim),
        grid=(m // tm, n // tn, nk),
        in_specs=in_specs,
        out_specs=out_specs,
        out_shape=out_shape,
        compiler_params=_params("parallel", "parallel", "arbitrary"),
        name=name,
    )(*args)


def _rotate_pairs(y, half):
    width = y.shape[-1]
    if 2 * half == width:
        return pltpu.roll(y, half, 1)
    lane = lax.broadcasted_iota(jnp.int32, y.shape, 1)
    first = (lane % (2 * half)) < half
    return jnp.where(first, pltpu.roll(y, width - half, 1), pltpu.roll(y, half, 1))


def _headnorm_kernel(x_ref, g_ref, cos_ref, sin_ref, o_ref, *, rot_half):
    _store_normed_heads(x_ref[...].astype(F32), g_ref, o_ref, (cos_ref, sin_ref), rot_half, mean_on_mxu=True)


def _headnorm(x, col_off, head_gain, rope, seq, tm=1024, bw=512):
    t = x.shape[0]
    width = head_gain.shape[1]
    tm = min(tm, seq)
    assert col_off % bw == 0 and width % bw == 0 and seq % tm == 0
    cb = col_off // bw
    cos, sin, rot_half = rope
    nsb = seq // tm
    tab_spec = pl.BlockSpec((tm, HEAD_DIM), lambda i, j: (i % nsb, 0))
    return pl.pallas_call(
        functools.partial(_headnorm_kernel, rot_half=rot_half),
        grid=(t // tm, width // bw),
        in_specs=[pl.BlockSpec((tm, bw), lambda i, j: (i, cb + j)), pl.BlockSpec((1, bw), lambda i, j: (0, j)),
                  tab_spec, tab_spec],
        out_specs=pl.BlockSpec((tm, bw), lambda i, j: (i, j)),
        out_shape=jax.ShapeDtypeStruct((t, width), BF16),
        compiler_params=_params("parallel", "parallel"),
        name="headnorm",
    )(x, head_gain, cos, sin)


def _rope_tables(seq):
    pos = jnp.arange(seq, dtype=jnp.int32)

    def angles(p, dr):
        inv = ROPE_THETA ** (-jnp.arange(0, dr, 2, dtype=F32) / dr)
        return p.astype(F32)[:, None] * inv[None, :]

    a1 = angles(pos, HEAD_DIM)
    cos1 = jnp.concatenate([jnp.cos(a1), jnp.cos(a1)], axis=-1)
    sin1 = jnp.concatenate([-jnp.sin(a1), jnp.sin(a1)], axis=-1)
    ar = angles(pos // GRID_W, HEAD_DIM // 2)
    ac = angles(pos % GRID_W, HEAD_DIM // 2)
    cos2 = jnp.concatenate([jnp.cos(ar), jnp.cos(ar), jnp.cos(ac), jnp.cos(ac)], axis=-1)
    sin2 = jnp.concatenate([-jnp.sin(ar), jnp.sin(ar), -jnp.sin(ac), jnp.sin(ac)], axis=-1)
    return (cos1, sin1, HEAD_DIM // 2), (cos2, sin2, HEAD_DIM // 4)


def _qk(q, k):
    return lax.dot_general(q, k, (((1,), (1,)), ((), ())), preferred_element_type=F32)


def _score_bound_ok(q_gain, k_gain, bias=None):
    bound = HEAD_DIM * QK_SCALE_LOG2 * jnp.max(jnp.abs(q_gain)) * jnp.max(jnp.abs(k_gain))
    if bias is not None:
        bound = bound + LOG2E * jnp.max(jnp.abs(bias))
    return bound <= PLAIN_SOFTMAX_LOG2_LIMIT


def _lane_fold(p):
    out = p[:, :LANES]
    for c in range(1, p.shape[1] // LANES):
        out = out + p[:, c * LANES:(c + 1) * LANES]
    return out


def _online_step(s, vj, m, l, acc):
    m_new = jnp.maximum(m, jnp.max(s, axis=-1, keepdims=True))
    alpha = jnp.exp2(m - m_new)
    p = jnp.exp2(s - m_new)
    l_new = alpha * l + jnp.sum(p, axis=-1, keepdims=True)
    acc_new = alpha * acc + jnp.dot(p.astype(vj.dtype), vj, preferred_element_type=F32)
    return m_new, l_new, acc_new


def _attend(score_fn, k_ref, v_ref, rows, dv, seq, tk, online, scratch):
    def chunk(j):
        start = pl.multiple_of(j * tk, tk)
        return k_ref[pl.ds(start, tk), :], v_ref[pl.ds(start, tk), :]

    if online:
        def body(j, carry):
            kj, vj = chunk(j)
            return _online_step(score_fn(kj), vj, *carry)

        init = (jnp.full((rows, 1), -jnp.inf, F32), jnp.zeros((rows, 1), F32), jnp.zeros((rows, dv), F32))
        _, l, acc = lax.fori_loop(0, seq // tk, body, init)
        return acc / l

    acc_ref, l_ref = scratch
    acc_ref[...] = jnp.zeros_like(acc_ref)
    l_ref[...] = jnp.zeros_like(l_ref)

    def body(j, carry):
        kj, vj = chunk(j)
        p = jnp.exp2(score_fn(kj))
        l_ref[...] += _lane_fold(p)
        acc_ref[...] = jnp.dot(p.astype(vj.dtype), vj, preferred_element_type=F32) + acc_ref[...]
        return carry

    lax.fori_loop(0, seq // tk, body, 0)
    return acc_ref[...] / jnp.sum(l_ref[...], axis=-1, keepdims=True)


def _attend_scratch(rows, dv):
    return [pltpu.VMEM((rows, dv), F32), pltpu.VMEM((rows, LANES), F32)]


def _by_softmax_kind(plain_ref, run):
    @pl.when(plain_ref[0] != 0)
    def _():
        run(False)

    @pl.when(plain_ref[0] == 0)
    def _():
        run(True)


def _plain_flag(plain_ok):
    return plain_ok.astype(jnp.int32).reshape(1)


_FLAG_SPEC = pl.BlockSpec(memory_space=pltpu.SMEM)


def _transpose_kernel(x_ref, o_ref):
    o_ref[...] = x_ref[...].T


def _transposed_columns(x, col_off, width, tm=1024):
    t = x.shape[0]
    tm = min(tm, t)
    assert col_off % width == 0 and t % tm == 0
    cb = col_off // width
    return pl.pallas_call(
        _transpose_kernel,
        grid=(t // tm,),
        in_specs=[pl.BlockSpec((tm, width), lambda i: (i, cb))],
        out_specs=pl.BlockSpec((width, tm), lambda i: (0, i)),
        out_shape=jax.ShapeDtypeStruct((width, t), x.dtype),
        compiler_params=_params("parallel"),
        name="transpose_cols",
    )(x)


def _gqa_kernel(plain_ref, q_ref, k_ref, v_ref, vt_ref, o_ref, acc_ref, l_ref, *, seq, tk):
    tq = q_ref.shape[0]
    m = GQA_GROUP * tq
    q = jnp.concatenate([q_ref[:, g * HEAD_DIM:(g + 1) * HEAD_DIM] for g in range(GQA_GROUP)], axis=0)

    def store(o):
        for g in range(GQA_GROUP):
            o_ref[:, g * HEAD_DIM:(g + 1) * HEAD_DIM] = o[g * tq:(g + 1) * tq].astype(o_ref.dtype)

    def run(online):
        if online:
            store(_attend(lambda kj: _qk(q, kj), k_ref, v_ref, m, HEAD_DIM, seq, tk, True, None))
            return
        acc_ref[...] = jnp.zeros_like(acc_ref)
        l_ref[...] = jnp.zeros_like(l_ref)

        def body(j, carry):
            start = pl.multiple_of(j * tk, tk)
            pt = jnp.exp2(_qk(k_ref[pl.ds(start, tk), :], q))
            l_ref[...] += pt.reshape(tk // 8, 8, m).sum(axis=0)
            acc_ref[...] = (jnp.dot(vt_ref[:, pl.ds(start, tk)], pt.astype(BF16), preferred_element_type=F32)
                            + acc_ref[...])
            return carry

        lax.fori_loop(0, seq // tk, body, 0)
        store((acc_ref[...] / jnp.sum(l_ref[...], axis=0, keepdims=True)).T)

    _by_softmax_kind(plain_ref, run)


def _gqa_attention(plain_ok, qk, v_all, *, batch, seq, tq=512, tk=2048):
    t = batch * seq
    tq, tk = min(tq, seq), min(tk, seq)
    nqb = seq // tq
    gw = GQA_GROUP * HEAD_DIM
    kcb = W_C_Q // HEAD_DIM
    vcb = V_OFF_C // HEAD_DIM
    q_spec = pl.BlockSpec((tq, gw), lambda b, n, i: (b * nqb + i, n))
    v_t = _transposed_columns(v_all, V_OFF_C, W_C_KV)
    return pl.pallas_call(
        functools.partial(_gqa_kernel, seq=seq, tk=tk),
        grid=(batch, GQA_KV_HEADS, nqb),
        in_specs=[_FLAG_SPEC, q_spec,
                  pl.BlockSpec((seq, HEAD_DIM), lambda b, n, i: (b, kcb + n)),
                  pl.BlockSpec((seq, HEAD_DIM), lambda b, n, i: (b, vcb + n)),
                  pl.BlockSpec((HEAD_DIM, seq), lambda b, n, i: (n, b))],
        out_specs=q_spec,
        out_shape=jax.ShapeDtypeStruct((t, W_C_Q), BF16),
        scratch_shapes=[pltpu.VMEM((HEAD_DIM, GQA_GROUP * tq), F32), pltpu.VMEM((8, GQA_GROUP * tq), F32)],
        compiler_params=_params("parallel", "parallel", "parallel"),
        name="gqa_attention",
    )(_plain_flag(plain_ok), qk, qk, v_all, v_t)


def _diff_kernel(plain_ref, q_ref, k_ref, v_ref, lq1_ref, lk1_ref, lq2_ref, lk2_ref, sg_ref, o_ref, *scratch,
                 seq, tk, lam_init):
    tq = q_ref.shape[0]
    dv = 2 * HEAD_DIM
    q1 = q_ref[:, :HEAD_DIM]
    q2 = q_ref[:, HEAD_DIM:]

    def scores(kj):
        return jnp.concatenate([_qk(q1, kj[:, :HEAD_DIM]), _qk(q2, kj[:, HEAD_DIM:])], axis=0)

    def run(online):
        o12 = _attend(scores, k_ref, v_ref, 2 * tq, dv, seq, tk, online, scratch)
        lam = (jnp.exp(jnp.sum(lq1_ref[...] * lk1_ref[...], axis=-1, keepdims=True))
               - jnp.exp(jnp.sum(lq2_ref[...] * lk2_ref[...], axis=-1, keepdims=True)) + lam_init)
        o = o12[:tq] - lam * o12[tq:]
        ms = jnp.mean(o * o, axis=-1, keepdims=True)
        o = o * lax.rsqrt(ms + NORM_EPS) * sg_ref[...] * (1.0 - lam_init)
        o_ref[...] = o.astype(o_ref.dtype)

    _by_softmax_kind(plain_ref, run)


def _diff_attention(plain_ok, qk, v_all, lam_vecs, sub_gain, *, lam_init, batch, seq, tq=512, tk=2048):
    t = batch * seq
    tq, tk = min(tq, seq), min(tk, seq)
    nqb = seq // tq
    pw = 2 * HEAD_DIM
    kcb = W_B_QK // pw
    vcb = V_OFF_B // pw
    vec_spec = pl.BlockSpec((1, HEAD_DIM), lambda b, h, i: (0, 0))
    return pl.pallas_call(
        functools.partial(_diff_kernel, seq=seq, tk=tk, lam_init=lam_init),
        grid=(batch, DIFF_HEADS, nqb),
        in_specs=[
            _FLAG_SPEC,
            pl.BlockSpec((tq, pw), lambda b, h, i: (b * nqb + i, h)),
            pl.BlockSpec((seq, pw), lambda b, h, i: (b, kcb + h)),
            pl.BlockSpec((seq, pw), lambda b, h, i: (b, vcb + h)),
            vec_spec, vec_spec, vec_spec, vec_spec,
            pl.BlockSpec((1, pw), lambda b, h, i: (0, 0)),
        ],
        out_specs=pl.BlockSpec((tq, pw), lambda b, h, i: (b * nqb + i, h)),
        out_shape=jax.ShapeDtypeStruct((t, W_B_V), BF16),
        scratch_shapes=_attend_scratch(2 * tq, pw),
        compiler_params=_params("parallel", "parallel", "parallel"),
        name="diff_attention",
    )(_plain_flag(plain_ok), qk, qk, v_all, *lam_vecs, sub_gain)


NA_Q_ROWS = 4
NA_K_ROWS = NA_Q_ROWS + NA_WIN_R
NA_BLOCK_TYPES = 3
NA_UNROLL = 4


def _na_bias_table(rpb):
    m = jnp.arange(LANES)
    delta = jnp.where(m < GRID_W, m, m - LANES)
    by_delta = rpb.astype(F32)[:, :, jnp.clip(delta, -(NA_WIN_C - 1), NA_WIN_C - 1) + NA_WIN_C - 1] * LOG2E
    heads = rpb.shape[0]
    return pl.pallas_call(
        _na_bias_kernel,
        grid=(heads,),
        in_specs=[pl.BlockSpec((1, 2 * NA_WIN_R - 1, LANES), lambda h: (h, 0, 0))],
        out_specs=pl.BlockSpec((1, NA_BLOCK_TYPES, NA_Q_ROWS * GRID_W, NA_K_ROWS * GRID_W), lambda h: (h, 0, 0, 0)),
        out_shape=jax.ShapeDtypeStruct((heads, NA_BLOCK_TYPES, NA_Q_ROWS * GRID_W, NA_K_ROWS * GRID_W), F32),
        compiler_params=_params("parallel"),
        name="na_bias_table",
    )(by_delta)


def _na_bias_kernel(by_delta_ref, o_ref):
    q = lax.broadcasted_iota(jnp.int32, (GRID_W, GRID_W), 0)
    k = lax.broadcasted_iota(jnp.int32, (GRID_W, GRID_W), 1)
    col_start = jnp.clip(q - NA_WIN_C // 2, 0, GRID_W - NA_WIN_C)
    in_window = (k >= col_start) & (k < col_start + NA_WIN_C)
    masked = jnp.full((GRID_W, GRID_W), NEG_BIG, F32)

    def block(d):
        rows = jnp.broadcast_to(by_delta_ref[0, d:d + 1, :], (GRID_W, LANES))
        toeplitz = pltpu.roll(rows, 0, 1, stride=1, stride_axis=0)
        return jnp.where(in_window, toeplitz[:, :GRID_W], masked)

    by_offset = [block(d) for d in range(2 * NA_WIN_R - 1)]
    for block_type in range(NA_BLOCK_TYPES):
        for rr in range(NA_Q_ROWS):
            q_off, w_off = ((rr, 0), (rr + NA_WIN_R // 2, rr),
                            (rr + NA_K_ROWS - NA_Q_ROWS, NA_K_ROWS - NA_WIN_R))[block_type]
            strip = [by_offset[i - q_off + NA_WIN_R - 1] if w_off <= i < w_off + NA_WIN_R else masked
                     for i in range(NA_K_ROWS)]
            o_ref[0, block_type, rr * GRID_W:(rr + 1) * GRID_W, :] = jnp.concatenate(strip, axis=1)


def _na_kernel(plain_ref, q_ref, k_ref, v_ref, bias_ref, o_ref, *, rows):
    nrb = rows // NA_Q_ROWS
    tq = NA_Q_ROWS * GRID_W

    def run(online):
        def row_block(rb, carry):
            ws = jnp.clip(rb * NA_Q_ROWS - NA_WIN_R // 2, 0, rows - NA_K_ROWS)
            start = pl.multiple_of(ws * GRID_W, GRID_W)
            qstart = pl.multiple_of(rb * tq, tq)
            kw = k_ref[pl.ds(start, NA_K_ROWS * GRID_W), :]
            vw = v_ref[pl.ds(start, NA_K_ROWS * GRID_W), :]
            block_type = jnp.where(rb == 0, 0, jnp.where(rb == nrb - 1, 2, 1))
            s = _qk(q_ref[pl.ds(qstart, tq), :], kw) + bias_ref[0, block_type]
            if online:
                s = s - jnp.max(s, axis=-1, keepdims=True)
            p = jnp.exp2(s)
            l = jnp.sum(p, axis=-1, keepdims=True)
            o = jnp.dot(p.astype(vw.dtype), vw, preferred_element_type=F32) / l
            o_ref[pl.ds(qstart, tq), :] = o.astype(o_ref.dtype)
            return carry

        lax.fori_loop(0, nrb, row_block, 0, unroll=NA_UNROLL)

    _by_softmax_kind(plain_ref, run)


def _na_attention(plain_ok, qk, v_all, bias_tab, *, batch, seq):
    t = batch * seq
    rows = seq // GRID_W
    assert rows >= NA_K_ROWS and rows % (NA_UNROLL * NA_Q_ROWS) == 0
    kcb = W_A // HEAD_DIM
    vcb = V_OFF_A // HEAD_DIM
    return pl.pallas_call(
        functools.partial(_na_kernel, rows=rows),
        grid=(batch, NA_HEADS),
        in_specs=[
            _FLAG_SPEC,
            pl.BlockSpec((seq, HEAD_DIM), lambda b, h: (b, h)),
            pl.BlockSpec((seq, HEAD_DIM), lambda b, h: (b, kcb + h)),
            pl.BlockSpec((seq, HEAD_DIM), lambda b, h: (b, vcb + h)),
            pl.BlockSpec((1, NA_BLOCK_TYPES, NA_Q_ROWS * GRID_W, NA_K_ROWS * GRID_W), lambda b, h: (h, 0, 0, 0)),
        ],
        out_specs=pl.BlockSpec((seq, HEAD_DIM), lambda b, h: (b, h)),
        out_shape=jax.ShapeDtypeStruct((t, W_A), BF16),
        compiler_params=_params("parallel", "parallel"),
        name="na_attention",
    )(_plain_flag(plain_ok), qk, qk, v_all, bias_tab)


def _merge_kernel(oa_ref, ob_ref, oc_ref, wa_ref, wb_ref, wc_ref, ga_ref, gb_ref, gc_ref, o_ref):
    def branch(o_r, w_r, g_r):
        y = jnp.dot(o_r[...], w_r[...], preferred_element_type=F32)
        return jax.nn.sigmoid(g_r[...].astype(F32)) * y

    merged = branch(oa_ref, wa_ref, ga_ref) + branch(ob_ref, wb_ref, gb_ref) + branch(oc_ref, wc_ref, gc_ref)
    o_ref[...] = merged.astype(o_ref.dtype)


def _merge(o_a, o_b, o_c, w_oa, w_ob, w_oc, layer, rest, tm=1024, tn=1024):
    t = o_a.shape[0]
    d = w_oa.shape[2]
    tm, tn = min(tm, t), min(tn, d)
    assert d % tn == 0 and G_OFF % tn == 0
    gcb = G_OFF // tn
    ncb = d // tn

    def act(width):
        return pl.BlockSpec((tm, width), lambda j, i: (i, 0))

    def wgt(width):
        return pl.BlockSpec((None, width, tn), lambda j, i: (layer, 0, j))

    def gate(branch):
        return pl.BlockSpec((tm, tn), lambda j, i: (i, gcb + branch * ncb + j))

    return pl.pallas_call(
        _merge_kernel,
        grid=(ncb, t // tm),
        in_specs=[act(W_A), act(W_B_V), act(W_C_Q), wgt(W_A), wgt(W_B_V), wgt(W_C_Q), gate(0), gate(1), gate(2)],
        out_specs=pl.BlockSpec((tm, tn), lambda j, i: (i, j)),
        out_shape=jax.ShapeDtypeStruct((t, d), BF16),
        compiler_params=_params("parallel", "parallel"),
        name="gated_merge",
    )(o_a, o_b, o_c, w_oa, w_ob, w_oc, rest, rest, rest)


def _row(v):
    return v.reshape(1, -1).astype(F32)


def _input_projection(h, w, l, seq, tables):
    rope1, rope2 = tables
    w_in = w["w_in"]
    tn = 1024
    assert REST_OFF == 2 * W_A and REST_OFF % tn == 0

    def gains(q_gain, k_gain, q_heads, k_heads):
        return jnp.concatenate([jnp.tile(q_gain.astype(F32) * QK_SCALE_LOG2, q_heads),
                                jnp.tile(k_gain.astype(F32), k_heads)]).reshape(1, -1)

    qk_a = _matmul(h, w_in, l, kind="headnorm", n=REST_OFF, tn=tn,
                   head_gain=gains(w["qn_a"][l], w["kn_a"][l], NA_HEADS, NA_HEADS), name="proj_qk_a")
    rest = _matmul(h, w_in, l, n=w_in.shape[2] - REST_OFF, tn=tn, col_block0=REST_OFF // tn, name="proj_rest")
    qk_b = _headnorm(rest, OFF_QB - REST_OFF, gains(w["qn_b"][l], w["kn_b"][l], 2 * DIFF_HEADS, 2 * DIFF_HEADS),
                     rope1, seq)
    qk_c = _headnorm(rest, OFF_QC - REST_OFF, gains(w["qn_c"][l], w["kn_c"][l], GQA_Q_HEADS, GQA_KV_HEADS),
                     rope2, seq)
    return qk_a, qk_b, qk_c, rest


def _layer(x, l, batch, seq, w, tables):
    lam_init = 0.8 - 0.6 * math.exp(-0.3 * l)
    dims = dict(batch=batch, seq=seq)
    h = _rmsnorm(x, w["norm_mix"][l])
    qk_a, qk_b, qk_c, rest = _input_projection(h, w, l, seq, tables)

    o_a = _na_attention(_score_bound_ok(w["qn_a"][l], w["kn_a"][l], w["rpb"][l]), qk_a, rest, w["na_bias"][l],
                        **dims)
    lam_vecs = tuple(_row(w[n][l]) for n in ("lam_q1", "lam_k1", "lam_q2", "lam_k2"))
    o_b = _diff_attention(_score_bound_ok(w["qn_b"][l], w["kn_b"][l]), qk_b, rest, lam_vecs, _row(w["subln_b"][l]),
                          lam_init=lam_init, **dims)
    o_c = _gqa_attention(_score_bound_ok(w["qn_c"][l], w["kn_c"][l]), qk_c, rest, **dims)

    merged = _merge(o_a, o_b, o_c, w["w_oa"], w["w_ob"], w["w_oc"], l, rest)
    x, x_bf16, x_sumsq = _matmul(merged, w["w_out"], l, kind="residual_stats", residual=x, out_dtype=F32,
                                 name="proj_out")

    u = _matmul(x_bf16, w["w_up"], l, kind="rownorm_relu2", row_sumsq=x_sumsq, name="mlp_up")
    return _matmul(u, w["w_down"], l, kind="residual", residual=x, out_dtype=F32, name="mlp_down")


_MATMUL_WEIGHTS = ("w_in", "w_oa", "w_ob", "w_oc", "w_out", "w_up", "w_down")


def kernel(x_prompt, x_sample, norm_mix, w_in, qn_a, kn_a, rpb, qn_b, kn_b, lam_q1, lam_k1, lam_q2, lam_k2,
           subln_b, qn_c, kn_c, w_oa, w_ob, w_oc, w_out, norm_mlp, w_up, w_down):
    w = dict(norm_mix=norm_mix, w_in=w_in, qn_a=qn_a, kn_a=kn_a, rpb=rpb, qn_b=qn_b, kn_b=kn_b, lam_q1=lam_q1,
             lam_k1=lam_k1, lam_q2=lam_q2, lam_k2=lam_k2, subln_b=subln_b, qn_c=qn_c, kn_c=kn_c, w_oa=w_oa,
             w_ob=w_ob, w_oc=w_oc, w_out=w_out, norm_mlp=norm_mlp, w_up=w_up, w_down=w_down)
    w["w_up"] = norm_mlp.astype(F32)[:, :, None] * w_up
    for name in _MATMUL_WEIGHTS:
        w[name] = w[name].astype(BF16)
    depth = w_in.shape[0]
    w["na_bias"] = [_na_bias_table(rpb[l]) for l in range(depth)]

    def trunk(x):
        batch, seq, d = x.shape
        tables = _rope_tables(seq)
        y = x.reshape(batch * seq, d)
        for l in range(depth):
            y = _layer(y, l, batch, seq, w, tables)
        return y.reshape(batch, seq, d)

    return (trunk(x_prompt), trunk(x_sample))
```

```python
import functools
import math

import jax
import jax.numpy as jnp
from jax import lax
from jax.experimental import pallas as pl
from jax.experimental.pallas import tpu as pltpu

F32 = jnp.float32
BF16 = jnp.bfloat16

LANES = 128
HEAD_DIM = 128
GRID_W = 64
NA_HEADS = 12
NA_WIN_R = 8
NA_WIN_C = 16
DIFF_HEADS = 4
GQA_Q_HEADS = 12
GQA_KV_HEADS = 4
GQA_GROUP = GQA_Q_HEADS // GQA_KV_HEADS
ROPE_THETA = 10000.0
NORM_EPS = 1e-6

W_A = NA_HEADS * HEAD_DIM
W_B_QK = 2 * DIFF_HEADS * HEAD_DIM
W_B_V = DIFF_HEADS * 2 * HEAD_DIM
W_C_Q = GQA_Q_HEADS * HEAD_DIM
W_C_KV = GQA_KV_HEADS * HEAD_DIM

OFF_QA = 0
OFF_KA = OFF_QA + W_A
OFF_VA = OFF_KA + W_A
OFF_QB = OFF_VA + W_A
OFF_KB = OFF_QB + W_B_QK
OFF_VB = OFF_KB + W_B_QK
OFF_QC = OFF_VB + W_B_V
OFF_KC = OFF_QC + W_C_Q
OFF_VC = OFF_KC + W_C_KV
OFF_GA = OFF_VC + W_C_KV
REST_OFF = OFF_VA
V_OFF_A = OFF_VA - REST_OFF
V_OFF_B = OFF_VB - REST_OFF
V_OFF_C = OFF_VC - REST_OFF
G_OFF = OFF_GA - REST_OFF

V7X_VMEM_LIMIT_BYTES = 62 * 1024 * 1024

LOG2E = 1.4426950408889634
QK_SCALE_LOG2 = HEAD_DIM ** -0.5 * LOG2E
NEG_BIG = -1e30
PLAIN_SOFTMAX_LOG2_LIMIT = 64.0


def _params(*semantics):
    return pltpu.CompilerParams(dimension_semantics=semantics, vmem_limit_bytes=V7X_VMEM_LIMIT_BYTES)


def _rmsnorm_kernel(x_ref, g_ref, o_ref):
    x = x_ref[...]
    ms = jnp.mean(x * x, axis=-1, keepdims=True)
    o_ref[...] = (x * lax.rsqrt(ms + NORM_EPS) * g_ref[...]).astype(o_ref.dtype)


def _rmsnorm(x, gain, tm=512):
    t, d = x.shape
    tm = min(tm, t)
    return pl.pallas_call(
        _rmsnorm_kernel,
        grid=(t // tm,),
        in_specs=[pl.BlockSpec((tm, d), lambda i: (i, 0)), pl.BlockSpec((1, d), lambda i: (0, 0))],
        out_specs=pl.BlockSpec((tm, d), lambda i: (i, 0)),
        out_shape=jax.ShapeDtypeStruct((t, d), BF16),
        compiler_params=_params("parallel"),
        name="rmsnorm",
    )(x, gain.reshape(1, d).astype(F32))


def _store_normed_heads(x, g_ref, o_ref, rope_refs=None, rot_half=0, mean_on_mxu=False):
    averager = jnp.full((HEAD_DIM, HEAD_DIM), 1.0 / HEAD_DIM, BF16)
    for h in range(x.shape[1] // HEAD_DIM):
        cols = slice(h * HEAD_DIM, (h + 1) * HEAD_DIM)
        xh = x[:, cols]
        if mean_on_mxu:
            ms = jnp.dot((xh * xh).astype(BF16), averager, preferred_element_type=F32)
        else:
            ms = jnp.mean(xh * xh, axis=-1, keepdims=True)
        y = xh * lax.rsqrt(ms + NORM_EPS) * g_ref[:, cols]
        if rope_refs is not None:
            cos_ref, sin_ref = rope_refs
            y = y * cos_ref[...] + _rotate_pairs(y, rot_half) * sin_ref[...]
        o_ref[:, cols] = y.astype(o_ref.dtype)


def _matmul_kernel(*refs, kind, nk, kdim, n_in, rider_gains):
    ins, outs = refs[:n_in], refs[n_in:]
    for has_gain, o_ref in zip(reversed(rider_gains), reversed(outs)):
        if has_gain:
            src, ins = ins[-2][...] * ins[-1][...], ins[:-2]
        else:
            src, ins = ins[-1][...], ins[:-1]
        o_ref[...] = src.astype(o_ref.dtype)
    refs = ins + outs[:len(outs) - len(rider_gains)]
    a_ref, b_ref = refs[:2]
    prod = jnp.dot(a_ref[...], b_ref[...], preferred_element_type=F32)
    if kind == "residual":
        res_ref, o_ref = refs[2:]
        if nk == 1:
            o_ref[...] = prod + res_ref[...]
        else:
            o_ref[...] = prod + jnp.where(pl.program_id(2) == 0, res_ref[...], o_ref[...])
    elif kind == "residual_stats":
        res_ref, o_ref, ob_ref, ss_ref = refs[2:]
        x = prod + res_ref[...]
        o_ref[...] = x
        ob_ref[...] = x.astype(ob_ref.dtype)
        ss_ref[...] = _lane_fold(x * x)
    elif kind == "rownorm_relu2":
        ss_ref, o_ref = refs[2:]
        ms = jnp.sum(ss_ref[...], axis=-1, keepdims=True) * (1.0 / kdim)
        o_ref[...] = jnp.square(jnp.maximum(prod * lax.rsqrt(ms + NORM_EPS), 0.0)).astype(o_ref.dtype)
    elif kind == "headnorm":
        g_ref, o_ref = refs[2:]
        _store_normed_heads(prod, g_ref, o_ref)
    else:
        (o_ref,) = refs[2:]
        o_ref[...] = prod.astype(o_ref.dtype)


def _rider_rows(total_rows, steps):
    return next(r for r in range(16, total_rows + 1, 16) if total_rows % r == 0 and total_rows // r <= steps)


def _matmul(a, w, layer, *, kind="plain", residual=None, row_sumsq=None, head_gain=None, riders=(),
            n=None, col_block0=0, out_dtype=BF16, tm=1024, tn=1024, tk=4096, name="matmul"):
    m, kdim = a.shape
    n = w.shape[2] if n is None else n
    tm, tn, tk = min(tm, m), min(tn, n), min(tk, kdim)
    nk = kdim // tk
    tile = pl.BlockSpec((tm, tn), lambda i, j, k: (i, j))
    in_specs = [pl.BlockSpec((tm, tk), lambda i, j, k: (i, k)),
                pl.BlockSpec((None, tk, tn), lambda i, j, k: (layer, k, col_block0 + j))]
    args = [a, w]
    out_specs, out_shape = tile, jax.ShapeDtypeStruct((m, n), out_dtype)
    if kind == "headnorm":
        in_specs.append(pl.BlockSpec((1, tn), lambda i, j, k: (0, j)))
        args.append(head_gain)
    if kind in ("residual", "residual_stats"):
        in_specs.append(tile)
        args.append(residual)
    if kind == "residual_stats":
        out_specs = (tile, tile, pl.BlockSpec((tm, LANES), lambda i, j, k: (i, j)))
        out_shape = (out_shape, jax.ShapeDtypeStruct((m, n), BF16),
                     jax.ShapeDtypeStruct((m, n // tn * LANES), F32))
    if kind == "rownorm_relu2":
        in_specs.append(pl.BlockSpec((tm, row_sumsq.shape[1]), lambda i, j, k: (i, 0)))
        args.append(row_sumsq)
    assert nk == 1 or (kind == "residual" and out_dtype == F32)
    if riders:
        assert nk == 1
        nj = n // tn
        out_specs, out_shape = [out_specs], [out_shape]
        for src, src_layer, gain in riders:
            _, rows_total, cols = src.shape
            rows = _rider_rows(rows_total, (m // tm) * nj)
            last = rows_total // rows - 1

            def slab(i, j, k, last=last):
                return jnp.minimum(i * nj + j, last)

            in_specs.append(pl.BlockSpec((None, rows, cols), lambda i, j, k, s=slab, l=src_layer: (l, s(i, j, k), 0)))
            args.append(src)
            if gain is not None:
                in_specs.append(pl.BlockSpec((rows, 1), lambda i, j, k, s=slab: (s(i, j, k), 0)))
                args.append(gain.reshape(rows_total, 1).astype(F32))
            out_specs.append(pl.BlockSpec((rows, cols), lambda i, j, k, s=slab: (s(i, j, k), 0)))
            out_shape.append(jax.ShapeDtypeStruct((rows_total, cols), BF16))
    return pl.pallas_call(
        functools.partial(_matmul_kernel, kind=kind, nk=nk, kdim=kdim, n_in=len(args),
                          rider_gains=tuple(gain is not None for _, _, gain in riders)),
        grid=(m // tm, n // tn, nk),
        in_specs=in_specs,
        out_specs=out_specs,
        out_shape=out_shape,
        compiler_params=_params(*(("arbitrary",) * 3 if riders else ("parallel", "parallel", "arbitrary"))),
        name=name,
    )(*args)


def _rotate_pairs(y, half):
    width = y.shape[-1]
    if 2 * half == width:
        return pltpu.roll(y, half, 1)
    lane = lax.broadcasted_iota(jnp.int32, y.shape, 1)
    first = (lane % (2 * half)) < half
    return jnp.where(first, pltpu.roll(y, width - half, 1), pltpu.roll(y, half, 1))


def _headnorm_kernel(x_ref, g_ref, cos_ref, sin_ref, o_ref, *, rot_half):
    _store_normed_heads(x_ref[...].astype(F32), g_ref, o_ref, (cos_ref, sin_ref), rot_half, mean_on_mxu=True)


def _headnorm(x, col_off, head_gain, rope, seq, tm=1024, bw=512):
    t = x.shape[0]
    width = head_gain.shape[1]
    tm = min(tm, seq)
    assert col_off % bw == 0 and width % bw == 0 and seq % tm == 0
    cb = col_off // bw
    cos, sin, rot_half = rope
    nsb = seq // tm
    tab_spec = pl.BlockSpec((tm, HEAD_DIM), lambda i, j: (i % nsb, 0))
    return pl.pallas_call(
        functools.partial(_headnorm_kernel, rot_half=rot_half),
        grid=(t // tm, width // bw),
        in_specs=[pl.BlockSpec((tm, bw), lambda i, j: (i, cb + j)), pl.BlockSpec((1, bw), lambda i, j: (0, j)),
                  tab_spec, tab_spec],
        out_specs=pl.BlockSpec((tm, bw), lambda i, j: (i, j)),
        out_shape=jax.ShapeDtypeStruct((t, width), BF16),
        compiler_params=_params("parallel", "parallel"),
        name="headnorm",
    )(x, head_gain, cos, sin)


def _rope_tables(seq):
    pos = jnp.arange(seq, dtype=jnp.int32)

    def angles(p, dr):
        inv = ROPE_THETA ** (-jnp.arange(0, dr, 2, dtype=F32) / dr)
        return p.astype(F32)[:, None] * inv[None, :]

    a1 = angles(pos, HEAD_DIM)
    cos1 = jnp.concatenate([jnp.cos(a1), jnp.cos(a1)], axis=-1)
    sin1 = jnp.concatenate([-jnp.sin(a1), jnp.sin(a1)], axis=-1)
    ar = angles(pos // GRID_W, HEAD_DIM // 2)
    ac = angles(pos % GRID_W, HEAD_DIM // 2)
    cos2 = jnp.concatenate([jnp.cos(ar), jnp.cos(ar), jnp.cos(ac), jnp.cos(ac)], axis=-1)
    sin2 = jnp.concatenate([-jnp.sin(ar), jnp.sin(ar), -jnp.sin(ac), jnp.sin(ac)], axis=-1)
    return (cos1, sin1, HEAD_DIM // 2), (cos2, sin2, HEAD_DIM // 4)


def _qk(q, k):
    return lax.dot_general(q, k, (((1,), (1,)), ((), ())), preferred_element_type=F32)


def _score_bound_ok(q_gain, k_gain, bias=None):
    bound = HEAD_DIM * QK_SCALE_LOG2 * jnp.max(jnp.abs(q_gain)) * jnp.max(jnp.abs(k_gain))
    if bias is not None:
        bound = bound + LOG2E * jnp.max(jnp.abs(bias))
    return bound <= PLAIN_SOFTMAX_LOG2_LIMIT


def _lane_fold(p):
    out = p[:, :LANES]
    for c in range(1, p.shape[1] // LANES):
        out = out + p[:, c * LANES:(c + 1) * LANES]
    return out


def _online_step(s, vj, m, l, acc):
    m_new = jnp.maximum(m, jnp.max(s, axis=-1, keepdims=True))
    alpha = jnp.exp2(m - m_new)
    p = jnp.exp2(s - m_new)
    l_new = alpha * l + jnp.sum(p, axis=-1, keepdims=True)
    acc_new = alpha * acc + jnp.dot(p.astype(vj.dtype), vj, preferred_element_type=F32)
    return m_new, l_new, acc_new


def _attend(score_fn, k_ref, v_ref, rows, dv, seq, tk, online, scratch):
    def chunk(j):
        start = pl.multiple_of(j * tk, tk)
        return k_ref[pl.ds(start, tk), :], v_ref[pl.ds(start, tk), :]

    if online:
        def body(j, carry):
            kj, vj = chunk(j)
            return _online_step(score_fn(kj), vj, *carry)

        init = (jnp.full((rows, 1), -jnp.inf, F32), jnp.zeros((rows, 1), F32), jnp.zeros((rows, dv), F32))
        _, l, acc = lax.fori_loop(0, seq // tk, body, init)
        return acc / l

    acc_ref, l_ref = scratch
    acc_ref[...] = jnp.zeros_like(acc_ref)
    l_ref[...] = jnp.zeros_like(l_ref)

    def body(j, carry):
        kj, vj = chunk(j)
        p = jnp.exp2(score_fn(kj))
        l_ref[...] += _lane_fold(p)
        acc_ref[...] = jnp.dot(p.astype(vj.dtype), vj, preferred_element_type=F32) + acc_ref[...]
        return carry

    lax.fori_loop(0, seq // tk, body, 0)
    return acc_ref[...] / jnp.sum(l_ref[...], axis=-1, keepdims=True)


def _attend_scratch(rows, dv):
    return [pltpu.VMEM((rows, dv), F32), pltpu.VMEM((rows, LANES), F32)]


def _by_softmax_kind(plain_ref, run):
    @pl.when(plain_ref[0] != 0)
    def _():
        run(False)

    @pl.when(plain_ref[0] == 0)
    def _():
        run(True)


def _plain_flag(plain_ok):
    return plain_ok.astype(jnp.int32).reshape(1)


_FLAG_SPEC = pl.BlockSpec(memory_space=pltpu.SMEM)


def _transpose_kernel(x_ref, o_ref):
    o_ref[...] = x_ref[...].T


def _transposed_columns(x, col_off, width, tm=1024):
    t = x.shape[0]
    tm = min(tm, t)
    assert col_off % width == 0 and t % tm == 0
    cb = col_off // width
    return pl.pallas_call(
        _transpose_kernel,
        grid=(t // tm,),
        in_specs=[pl.BlockSpec((tm, width), lambda i: (i, cb))],
        out_specs=pl.BlockSpec((width, tm), lambda i: (0, i)),
        out_shape=jax.ShapeDtypeStruct((width, t), x.dtype),
        compiler_params=_params("parallel"),
        name="transpose_cols",
    )(x)


def _gqa_kernel(plain_ref, q_ref, k_ref, v_ref, vt_ref, o_ref, acc_ref, l_ref, *, seq, tk):
    tq = q_ref.shape[0]
    m = GQA_GROUP * tq
    q = jnp.concatenate([q_ref[:, g * HEAD_DIM:(g + 1) * HEAD_DIM] for g in range(GQA_GROUP)], axis=0)

    def store(o):
        for g in range(GQA_GROUP):
            o_ref[:, g * HEAD_DIM:(g + 1) * HEAD_DIM] = o[g * tq:(g + 1) * tq].astype(o_ref.dtype)

    def run(online):
        if online:
            store(_attend(lambda kj: _qk(q, kj), k_ref, v_ref, m, HEAD_DIM, seq, tk, True, None))
            return
        acc_ref[...] = jnp.zeros_like(acc_ref)
        l_ref[...] = jnp.zeros_like(l_ref)

        def body(j, carry):
            start = pl.multiple_of(j * tk, tk)
            pt = jnp.exp2(_qk(k_ref[pl.ds(start, tk), :], q))
            l_ref[...] += pt.reshape(tk // 8, 8, m).sum(axis=0)
            acc_ref[...] = (jnp.dot(vt_ref[:, pl.ds(start, tk)], pt.astype(BF16), preferred_element_type=F32)
                            + acc_ref[...])
            return carry

        lax.fori_loop(0, seq // tk, body, 0)
        store((acc_ref[...] / jnp.sum(l_ref[...], axis=0, keepdims=True)).T)

    _by_softmax_kind(plain_ref, run)


def _gqa_attention(plain_ok, qk, v_all, *, batch, seq, tq=512, tk=2048):
    t = batch * seq
    tq, tk = min(tq, seq), min(tk, seq)
    nqb = seq // tq
    gw = GQA_GROUP * HEAD_DIM
    kcb = W_C_Q // HEAD_DIM
    vcb = V_OFF_C // HEAD_DIM
    q_spec = pl.BlockSpec((tq, gw), lambda b, n, i: (b * nqb + i, n))
    v_t = _transposed_columns(v_all, V_OFF_C, W_C_KV)
    return pl.pallas_call(
        functools.partial(_gqa_kernel, seq=seq, tk=tk),
        grid=(batch, GQA_KV_HEADS, nqb),
        in_specs=[_FLAG_SPEC, q_spec,
                  pl.BlockSpec((seq, HEAD_DIM), lambda b, n, i: (b, kcb + n)),
                  pl.BlockSpec((seq, HEAD_DIM), lambda b, n, i: (b, vcb + n)),
                  pl.BlockSpec((HEAD_DIM, seq), lambda b, n, i: (n, b))],
        out_specs=q_spec,
        out_shape=jax.ShapeDtypeStruct((t, W_C_Q), BF16),
        scratch_shapes=[pltpu.VMEM((HEAD_DIM, GQA_GROUP * tq), F32), pltpu.VMEM((8, GQA_GROUP * tq), F32)],
        compiler_params=_params("parallel", "parallel", "parallel"),
        name="gqa_attention",
    )(_plain_flag(plain_ok), qk, qk, v_all, v_t)


def _diff_kernel(plain_ref, q_ref, k_ref, v_ref, lq1_ref, lk1_ref, lq2_ref, lk2_ref, sg_ref, o_ref, *scratch,
                 seq, tk, lam_init):
    tq = q_ref.shape[0]
    dv = 2 * HEAD_DIM
    q1 = q_ref[:, :HEAD_DIM]
    q2 = q_ref[:, HEAD_DIM:]

    def scores(kj):
        return jnp.concatenate([_qk(q1, kj[:, :HEAD_DIM]), _qk(q2, kj[:, HEAD_DIM:])], axis=0)

    def run(online):
        o12 = _attend(scores, k_ref, v_ref, 2 * tq, dv, seq, tk, online, scratch)
        lam = (jnp.exp(jnp.sum(lq1_ref[...] * lk1_ref[...], axis=-1, keepdims=True))
               - jnp.exp(jnp.sum(lq2_ref[...] * lk2_ref[...], axis=-1, keepdims=True)) + lam_init)
        o = o12[:tq] - lam * o12[tq:]
        ms = jnp.mean(o * o, axis=-1, keepdims=True)
        o = o * lax.rsqrt(ms + NORM_EPS) * sg_ref[...] * (1.0 - lam_init)
        o_ref[...] = o.astype(o_ref.dtype)

    _by_softmax_kind(plain_ref, run)


def _diff_attention(plain_ok, qk, v_all, lam_vecs, sub_gain, *, lam_init, batch, seq, tq=512, tk=2048):
    t = batch * seq
    tq, tk = min(tq, seq), min(tk, seq)
    nqb = seq // tq
    pw = 2 * HEAD_DIM
    kcb = W_B_QK // pw
    vcb = V_OFF_B // pw
    vec_spec = pl.BlockSpec((1, HEAD_DIM), lambda b, h, i: (0, 0))
    return pl.pallas_call(
        functools.partial(_diff_kernel, seq=seq, tk=tk, lam_init=lam_init),
        grid=(batch, DIFF_HEADS, nqb),
        in_specs=[
            _FLAG_SPEC,
            pl.BlockSpec((tq, pw), lambda b, h, i: (b * nqb + i, h)),
            pl.BlockSpec((seq, pw), lambda b, h, i: (b, kcb + h)),
            pl.BlockSpec((seq, pw), lambda b, h, i: (b, vcb + h)),
            vec_spec, vec_spec, vec_spec, vec_spec,
            pl.BlockSpec((1, pw), lambda b, h, i: (0, 0)),
        ],
        out_specs=pl.BlockSpec((tq, pw), lambda b, h, i: (b * nqb + i, h)),
        out_shape=jax.ShapeDtypeStruct((t, W_B_V), BF16),
        scratch_shapes=_attend_scratch(2 * tq, pw),
        compiler_params=_params("parallel", "parallel", "parallel"),
        name="diff_attention",
    )(_plain_flag(plain_ok), qk, qk, v_all, *lam_vecs, sub_gain)


NA_Q_ROWS = 4
NA_K_ROWS = NA_Q_ROWS + NA_WIN_R
NA_BLOCK_TYPES = 3
NA_UNROLL = 4


def _na_bias_table(rpb):
    m = jnp.arange(LANES)
    delta = jnp.where(m < GRID_W, m, m - LANES)
    by_delta = rpb.astype(F32)[:, :, jnp.clip(delta, -(NA_WIN_C - 1), NA_WIN_C - 1) + NA_WIN_C - 1] * LOG2E
    heads = rpb.shape[0]
    return pl.pallas_call(
        _na_bias_kernel,
        grid=(heads,),
        in_specs=[pl.BlockSpec((1, 2 * NA_WIN_R - 1, LANES), lambda h: (h, 0, 0))],
        out_specs=pl.BlockSpec((1, NA_BLOCK_TYPES, NA_Q_ROWS * GRID_W, NA_K_ROWS * GRID_W), lambda h: (h, 0, 0, 0)),
        out_shape=jax.ShapeDtypeStruct((heads, NA_BLOCK_TYPES, NA_Q_ROWS * GRID_W, NA_K_ROWS * GRID_W), F32),
        compiler_params=_params("parallel"),
        name="na_bias_table",
    )(by_delta)


def _na_bias_kernel(by_delta_ref, o_ref):
    q = lax.broadcasted_iota(jnp.int32, (GRID_W, GRID_W), 0)
    k = lax.broadcasted_iota(jnp.int32, (GRID_W, GRID_W), 1)
    col_start = jnp.clip(q - NA_WIN_C // 2, 0, GRID_W - NA_WIN_C)
    in_window = (k >= col_start) & (k < col_start + NA_WIN_C)
    masked = jnp.full((GRID_W, GRID_W), NEG_BIG, F32)

    def block(d):
        rows = jnp.broadcast_to(by_delta_ref[0, d:d + 1, :], (GRID_W, LANES))
        toeplitz = pltpu.roll(rows, 0, 1, stride=1, stride_axis=0)
        return jnp.where(in_window, toeplitz[:, :GRID_W], masked)

    by_offset = [block(d) for d in range(2 * NA_WIN_R - 1)]
    for block_type in range(NA_BLOCK_TYPES):
        for rr in range(NA_Q_ROWS):
            q_off, w_off = ((rr, 0), (rr + NA_WIN_R // 2, rr),
                            (rr + NA_K_ROWS - NA_Q_ROWS, NA_K_ROWS - NA_WIN_R))[block_type]
            strip = [by_offset[i - q_off + NA_WIN_R - 1] if w_off <= i < w_off + NA_WIN_R else masked
                     for i in range(NA_K_ROWS)]
            o_ref[0, block_type, rr * GRID_W:(rr + 1) * GRID_W, :] = jnp.concatenate(strip, axis=1)


def _na_kernel(plain_ref, q_ref, k_ref, v_ref, bias_ref, o_ref, *, rows):
    nrb = rows // NA_Q_ROWS
    tq = NA_Q_ROWS * GRID_W

    def run(online):
        def row_block(rb, carry):
            ws = jnp.clip(rb * NA_Q_ROWS - NA_WIN_R // 2, 0, rows - NA_K_ROWS)
            start = pl.multiple_of(ws * GRID_W, GRID_W)
            qstart = pl.multiple_of(rb * tq, tq)
            kw = k_ref[pl.ds(start, NA_K_ROWS * GRID_W), :]
            vw = v_ref[pl.ds(start, NA_K_ROWS * GRID_W), :]
            block_type = jnp.where(rb == 0, 0, jnp.where(rb == nrb - 1, 2, 1))
            s = _qk(q_ref[pl.ds(qstart, tq), :], kw) + bias_ref[0, block_type]
            if online:
                s = s - jnp.max(s, axis=-1, keepdims=True)
            p = jnp.exp2(s)
            l = jnp.sum(p, axis=-1, keepdims=True)
            o = jnp.dot(p.astype(vw.dtype), vw, preferred_element_type=F32) / l
            o_ref[pl.ds(qstart, tq), :] = o.astype(o_ref.dtype)
            return carry

        lax.fori_loop(0, nrb, row_block, 0, unroll=NA_UNROLL)

    _by_softmax_kind(plain_ref, run)


def _na_attention(plain_ok, qk, v_all, bias_tab, *, batch, seq):
    t = batch * seq
    rows = seq // GRID_W
    assert rows >= NA_K_ROWS and rows % (NA_UNROLL * NA_Q_ROWS) == 0
    kcb = W_A // HEAD_DIM
    vcb = V_OFF_A // HEAD_DIM
    return pl.pallas_call(
        functools.partial(_na_kernel, rows=rows),
        grid=(batch, NA_HEADS),
        in_specs=[
            _FLAG_SPEC,
            pl.BlockSpec((seq, HEAD_DIM), lambda b, h: (b, h)),
            pl.BlockSpec((seq, HEAD_DIM), lambda b, h: (b, kcb + h)),
            pl.BlockSpec((seq, HEAD_DIM), lambda b, h: (b, vcb + h)),
            pl.BlockSpec((1, NA_BLOCK_TYPES, NA_Q_ROWS * GRID_W, NA_K_ROWS * GRID_W), lambda b, h: (h, 0, 0, 0)),
        ],
        out_specs=pl.BlockSpec((seq, HEAD_DIM), lambda b, h: (b, h)),
        out_shape=jax.ShapeDtypeStruct((t, W_A), BF16),
        compiler_params=_params("parallel", "parallel"),
        name="na_attention",
    )(_plain_flag(plain_ok), qk, qk, v_all, bias_tab)


def _merge_kernel(oa_ref, ob_ref, oc_ref, wa_ref, wb_ref, wc_ref, ga_ref, gb_ref, gc_ref, o_ref):
    def branch(o_r, w_r, g_r):
        y = jnp.dot(o_r[...], w_r[...], preferred_element_type=F32)
        return jax.nn.sigmoid(g_r[...].astype(F32)) * y

    merged = branch(oa_ref, wa_ref, ga_ref) + branch(ob_ref, wb_ref, gb_ref) + branch(oc_ref, wc_ref, gc_ref)
    o_ref[...] = merged.astype(o_ref.dtype)


def _merge(o_a, o_b, o_c, w_oa, w_ob, w_oc, layer, rest, tm=1024, tn=1024):
    t = o_a.shape[0]
    d = w_oa.shape[2]
    tm, tn = min(tm, t), min(tn, d)
    assert d % tn == 0 and G_OFF % tn == 0
    gcb = G_OFF // tn
    ncb = d // tn

    def act(width):
        return pl.BlockSpec((tm, width), lambda j, i: (i, 0))

    def wgt(width):
        return pl.BlockSpec((None, width, tn), lambda j, i: (layer, 0, j))

    def gate(branch):
        return pl.BlockSpec((tm, tn), lambda j, i: (i, gcb + branch * ncb + j))

    return pl.pallas_call(
        _merge_kernel,
        grid=(ncb, t // tm),
        in_specs=[act(W_A), act(W_B_V), act(W_C_Q), wgt(W_A), wgt(W_B_V), wgt(W_C_Q), gate(0), gate(1), gate(2)],
        out_specs=pl.BlockSpec((tm, tn), lambda j, i: (i, j)),
        out_shape=jax.ShapeDtypeStruct((t, d), BF16),
        compiler_params=_params("parallel", "parallel"),
        name="gated_merge",
    )(o_a, o_b, o_c, w_oa, w_ob, w_oc, rest, rest, rest)


def _row(v):
    return v.reshape(1, -1).astype(F32)


def _input_projection(h, w, l, seq, tables):
    rope1, rope2 = tables
    w_in = w["w_in"]
    tn = 1024
    assert REST_OFF == 2 * W_A and REST_OFF % tn == 0

    def gains(q_gain, k_gain, q_heads, k_heads):
        return jnp.concatenate([jnp.tile(q_gain.astype(F32) * QK_SCALE_LOG2, q_heads),
                                jnp.tile(k_gain.astype(F32), k_heads)]).reshape(1, -1)

    qk_a = _matmul(h, w_in, l, kind="headnorm", n=REST_OFF, tn=tn,
                   head_gain=gains(w["qn_a"][l], w["kn_a"][l], NA_HEADS, NA_HEADS), name="proj_qk_a")
    rest_args = dict(n=w_in.shape[2] - REST_OFF, tn=tn, col_block0=REST_OFF // tn, name="proj_rest")
    if l in w["late_casts"]:
        rest = _matmul(h, w_in, l, **rest_args)
    else:
        rest, w_out, w_up, w_down = _matmul(h, w_in, l, riders=[
            (w["w_out"], l, None), (w["w_up"], l, w["norm_mlp"][l]), (w["w_down"], l, None)], **rest_args)
        w["late_casts"][l] = dict(w_out=w_out[None], w_up=w_up[None], w_down=w_down[None])
    qk_b = _headnorm(rest, OFF_QB - REST_OFF, gains(w["qn_b"][l], w["kn_b"][l], 2 * DIFF_HEADS, 2 * DIFF_HEADS),
                     rope1, seq)
    qk_c = _headnorm(rest, OFF_QC - REST_OFF, gains(w["qn_c"][l], w["kn_c"][l], GQA_Q_HEADS, GQA_KV_HEADS),
                     rope2, seq)
    return qk_a, qk_b, qk_c, rest


def _layer(x, l, batch, seq, w, tables):
    lam_init = 0.8 - 0.6 * math.exp(-0.3 * l)
    dims = dict(batch=batch, seq=seq)
    h = _rmsnorm(x, w["norm_mix"][l])
    qk_a, qk_b, qk_c, rest = _input_projection(h, w, l, seq, tables)

    o_a = _na_attention(_score_bound_ok(w["qn_a"][l], w["kn_a"][l], w["rpb"][l]), qk_a, rest, w["na_bias"][l],
                        **dims)
    lam_vecs = tuple(_row(w[n][l]) for n in ("lam_q1", "lam_k1", "lam_q2", "lam_k2"))
    o_b = _diff_attention(_score_bound_ok(w["qn_b"][l], w["kn_b"][l]), qk_b, rest, lam_vecs, _row(w["subln_b"][l]),
                          lam_init=lam_init, **dims)
    o_c = _gqa_attention(_score_bound_ok(w["qn_c"][l], w["kn_c"][l]), qk_c, rest, **dims)

    merged = _merge(o_a, o_b, o_c, w["w_oa"], w["w_ob"], w["w_oc"], l, rest)
    late = w["late_casts"][l]
    x, x_bf16, x_sumsq = _matmul(merged, late["w_out"], 0, kind="residual_stats", residual=x, out_dtype=F32,
                                 name="proj_out")
    u = _matmul(x_bf16, late["w_up"], 0, kind="rownorm_relu2", row_sumsq=x_sumsq, name="mlp_up")
    return _matmul(u, late["w_down"], 0, kind="residual", residual=x, out_dtype=F32, name="mlp_down")


_PRECAST_WEIGHTS = ("w_in", "w_oa", "w_ob", "w_oc")


def kernel(x_prompt, x_sample, norm_mix, w_in, qn_a, kn_a, rpb, qn_b, kn_b, lam_q1, lam_k1, lam_q2, lam_k2,
           subln_b, qn_c, kn_c, w_oa, w_ob, w_oc, w_out, norm_mlp, w_up, w_down):
    w = dict(norm_mix=norm_mix, w_in=w_in, qn_a=qn_a, kn_a=kn_a, rpb=rpb, qn_b=qn_b, kn_b=kn_b, lam_q1=lam_q1,
             lam_k1=lam_k1, lam_q2=lam_q2, lam_k2=lam_k2, subln_b=subln_b, qn_c=qn_c, kn_c=kn_c, w_oa=w_oa,
             w_ob=w_ob, w_oc=w_oc, w_out=w_out, norm_mlp=norm_mlp, w_up=w_up, w_down=w_down)
    for name in _PRECAST_WEIGHTS:
        w[name] = w[name].astype(BF16)
    w["late_casts"] = {}
    depth = w_in.shape[0]
    w["na_bias"] = [_na_bias_table(rpb[l]) for l in range(depth)]

    def trunk(x):
        batch, seq, d = x.shape
        tables = _rope_tables(seq)
        y = x.reshape(batch * seq, d)
        for l in range(depth):
            y = _layer(y, l, batch, seq, w, tables)
        return y.reshape(batch, seq, d)

    return (trunk(x_prompt), trunk(x_sample))
```

```python
import functools
import math

import jax
import jax.numpy as jnp
from jax import lax
from jax.experimental import pallas as pl
from jax.experimental.pallas import tpu as pltpu

F32 = jnp.float32
BF16 = jnp.bfloat16

LANES = 128
HEAD_DIM = 128
GRID_W = 64
NA_HEADS = 12
NA_WIN_R = 8
NA_WIN_C = 16
DIFF_HEADS = 4
GQA_Q_HEADS = 12
GQA_KV_HEADS = 4
GQA_GROUP = GQA_Q_HEADS // GQA_KV_HEADS
ROPE_THETA = 10000.0
NORM_EPS = 1e-6

W_A = NA_HEADS * HEAD_DIM
W_B_QK = 2 * DIFF_HEADS * HEAD_DIM
W_B_V = DIFF_HEADS * 2 * HEAD_DIM
W_C_Q = GQA_Q_HEADS * HEAD_DIM
W_C_KV = GQA_KV_HEADS * HEAD_DIM

OFF_QA = 0
OFF_KA = OFF_QA + W_A
OFF_VA = OFF_KA + W_A
OFF_QB = OFF_VA + W_A
OFF_KB = OFF_QB + W_B_QK
OFF_VB = OFF_KB + W_B_QK
OFF_QC = OFF_VB + W_B_V
OFF_KC = OFF_QC + W_C_Q
OFF_VC = OFF_KC + W_C_KV
OFF_GA = OFF_VC + W_C_KV
REST_OFF = OFF_VA
V_OFF_A = OFF_VA - REST_OFF
V_OFF_B = OFF_VB - REST_OFF
V_OFF_C = OFF_VC - REST_OFF
G_OFF = OFF_GA - REST_OFF

V7X_VMEM_LIMIT_BYTES = 62 * 1024 * 1024

LOG2E = 1.4426950408889634
QK_SCALE_LOG2 = HEAD_DIM ** -0.5 * LOG2E
NEG_BIG = -1e30
PLAIN_SOFTMAX_LOG2_LIMIT = 64.0


def _params(*semantics):
    return pltpu.CompilerParams(dimension_semantics=semantics, vmem_limit_bytes=V7X_VMEM_LIMIT_BYTES)


def _rmsnorm_kernel(x_ref, g_ref, o_ref):
    x = x_ref[...]
    ms = jnp.mean(x * x, axis=-1, keepdims=True)
    o_ref[...] = (x * lax.rsqrt(ms + NORM_EPS) * g_ref[...]).astype(o_ref.dtype)


def _rmsnorm(x, gain, tm=512):
    t, d = x.shape
    tm = min(tm, t)
    return pl.pallas_call(
        _rmsnorm_kernel,
        grid=(t // tm,),
        in_specs=[pl.BlockSpec((tm, d), lambda i: (i, 0)), pl.BlockSpec((1, d), lambda i: (0, 0))],
        out_specs=pl.BlockSpec((tm, d), lambda i: (i, 0)),
        out_shape=jax.ShapeDtypeStruct((t, d), BF16),
        compiler_params=_params("parallel"),
        name="rmsnorm",
    )(x, gain.reshape(1, d).astype(F32))


def _store_normed_heads(x, g_ref, o_ref, rope_refs=None, rot_half=0, mean_on_mxu=False):
    averager = jnp.full((HEAD_DIM, HEAD_DIM), 1.0 / HEAD_DIM, BF16)
    for h in range(x.shape[1] // HEAD_DIM):
        cols = slice(h * HEAD_DIM, (h + 1) * HEAD_DIM)
        xh = x[:, cols]
        if mean_on_mxu:
            ms = jnp.dot((xh * xh).astype(BF16), averager, preferred_element_type=F32)
        else:
            ms = jnp.mean(xh * xh, axis=-1, keepdims=True)
        y = xh * lax.rsqrt(ms + NORM_EPS) * g_ref[:, cols]
        if rope_refs is not None:
            cos_ref, sin_ref = rope_refs
            y = y * cos_ref[...] + _rotate_pairs(y, rot_half) * sin_ref[...]
        o_ref[:, cols] = y.astype(o_ref.dtype)


def _matmul_kernel(*refs, kind, nk, kdim, n_in, rider_gains):
    ins, outs = refs[:n_in], refs[n_in:]
    for has_gain, o_ref in zip(reversed(rider_gains), reversed(outs)):
        if has_gain:
            src, ins = ins[-2][...] * ins[-1][...], ins[:-2]
        else:
            src, ins = ins[-1][...], ins[:-1]
        o_ref[...] = src.astype(o_ref.dtype)
    refs = ins + outs[:len(outs) - len(rider_gains)]
    a_ref, b_ref = refs[:2]
    prod = jnp.dot(a_ref[...], b_ref[...], preferred_element_type=F32)
    if kind == "residual":
        res_ref, o_ref = refs[2:]
        if nk == 1:
            o_ref[...] = prod + res_ref[...]
        else:
            o_ref[...] = prod + jnp.where(pl.program_id(2) == 0, res_ref[...], o_ref[...])
    elif kind == "residual_stats":
        res_ref, o_ref, ob_ref, ss_ref = refs[2:]
        x = prod + res_ref[...]
        o_ref[...] = x
        ob_ref[...] = x.astype(ob_ref.dtype)
        ss_ref[...] = _lane_fold(x * x)
    elif kind == "rownorm_relu2":
        ss_ref, o_ref = refs[2:]
        ms = jnp.sum(ss_ref[...], axis=-1, keepdims=True) * (1.0 / kdim)
        o_ref[...] = jnp.square(jnp.maximum(prod * lax.rsqrt(ms + NORM_EPS), 0.0)).astype(o_ref.dtype)
    elif kind == "headnorm":
        g_ref, o_ref = refs[2:]
        _store_normed_heads(prod, g_ref, o_ref)
    else:
        (o_ref,) = refs[2:]
        o_ref[...] = prod.astype(o_ref.dtype)


def _rider_rows(total_rows, steps):
    return next(r for r in range(16, total_rows + 1, 16) if total_rows % r == 0 and total_rows // r <= steps)


def _matmul(a, w, layer, *, kind="plain", residual=None, row_sumsq=None, head_gain=None, riders=(),
            n=None, col_block0=0, out_dtype=BF16, tm=1024, tn=1024, tk=4096, name="matmul"):
    m, kdim = a.shape
    n = w.shape[2] if n is None else n
    tm, tn, tk = min(tm, m), min(tn, n), min(tk, kdim)
    nk = kdim // tk
    tile = pl.BlockSpec((tm, tn), lambda i, j, k: (i, j))
    in_specs = [pl.BlockSpec((tm, tk), lambda i, j, k: (i, k)),
                pl.BlockSpec((None, tk, tn), lambda i, j, k: (layer, k, col_block0 + j))]
    args = [a, w]
    out_specs, out_shape = tile, jax.ShapeDtypeStruct((m, n), out_dtype)
    if kind == "headnorm":
        in_specs.append(pl.BlockSpec((1, tn), lambda i, j, k: (0, j)))
        args.append(head_gain)
    if kind in ("residual", "residual_stats"):
        in_specs.append(tile)
        args.append(residual)
    if kind == "residual_stats":
        out_specs = (tile, tile, pl.BlockSpec((tm, LANES), lambda i, j, k: (i, j)))
        out_shape = (out_shape, jax.ShapeDtypeStruct((m, n), BF16),
                     jax.ShapeDtypeStruct((m, n // tn * LANES), F32))
    if kind == "rownorm_relu2":
        in_specs.append(pl.BlockSpec((tm, row_sumsq.shape[1]), lambda i, j, k: (i, 0)))
        args.append(row_sumsq)
    assert nk == 1 or (kind == "residual" and out_dtype == F32)
    if riders:
        assert nk == 1
        nj = n // tn
        out_specs, out_shape = [out_specs], [out_shape]
        for src, src_layer, gain in riders:
            _, rows_total, cols = src.shape
            rows = _rider_rows(rows_total, (m // tm) * nj)
            last = rows_total // rows - 1

            def slab(i, j, k, last=last):
                return jnp.minimum(i * nj + j, last)

            in_specs.append(pl.BlockSpec((None, rows, cols), lambda i, j, k, s=slab, l=src_layer: (l, s(i, j, k), 0)))
            args.append(src)
            if gain is not None:
                in_specs.append(pl.BlockSpec((rows, 1), lambda i, j, k, s=slab: (s(i, j, k), 0)))
                args.append(gain.reshape(rows_total, 1).astype(F32))
            out_specs.append(pl.BlockSpec((rows, cols), lambda i, j, k, s=slab: (s(i, j, k), 0)))
            out_shape.append(jax.ShapeDtypeStruct((rows_total, cols), BF16))
    return pl.pallas_call(
        functools.partial(_matmul_kernel, kind=kind, nk=nk, kdim=kdim, n_in=len(args),
                          rider_gains=tuple(gain is not None for _, _, gain in riders)),
        grid=(m // tm, n // tn, nk),
        in_specs=in_specs,
        out_specs=out_specs,
        out_shape=out_shape,
        compiler_params=_params(*(("arbitrary",) * 3 if riders else ("parallel", "parallel", "arbitrary"))),
        name=name,
    )(*args)


def _rotate_pairs(y, half):
    width = y.shape[-1]
    if 2 * half == width:
        return pltpu.roll(y, half, 1)
    lane = lax.broadcasted_iota(jnp.int32, y.shape, 1)
    first = (lane % (2 * half)) < half
    return jnp.where(first, pltpu.roll(y, width - half, 1), pltpu.roll(y, half, 1))


def _headnorm_kernel(x_ref, g_ref, cos_ref, sin_ref, o_ref, *, rot_half):
    _store_normed_heads(x_ref[...].astype(F32), g_ref, o_ref, (cos_ref, sin_ref), rot_half, mean_on_mxu=True)


def _headnorm(x, col_off, head_gain, rope, seq, tm=1024, bw=512):
    t = x.shape[0]
    width = head_gain.shape[1]
    tm = min(tm, seq)
    assert col_off % bw == 0 and width % bw == 0 and seq % tm == 0
    cb = col_off // bw
    cos, sin, rot_half = rope
    nsb = seq // tm
    tab_spec = pl.BlockSpec((tm, HEAD_DIM), lambda i, j: (i % nsb, 0))
    return pl.pallas_call(
        functools.partial(_headnorm_kernel, rot_half=rot_half),
        grid=(t // tm, width // bw),
        in_specs=[pl.BlockSpec((tm, bw), lambda i, j: (i, cb + j)), pl.BlockSpec((1, bw), lambda i, j: (0, j)),
                  tab_spec, tab_spec],
        out_specs=pl.BlockSpec((tm, bw), lambda i, j: (i, j)),
        out_shape=jax.ShapeDtypeStruct((t, width), BF16),
        compiler_params=_params("parallel", "parallel"),
        name="headnorm",
    )(x, head_gain, cos, sin)


def _rope_tables(seq):
    pos = jnp.arange(seq, dtype=jnp.int32)

    def angles(p, dr):
        inv = ROPE_THETA ** (-jnp.arange(0, dr, 2, dtype=F32) / dr)
        return p.astype(F32)[:, None] * inv[None, :]

    a1 = angles(pos, HEAD_DIM)
    cos1 = jnp.concatenate([jnp.cos(a1), jnp.cos(a1)], axis=-1)
    sin1 = jnp.concatenate([-jnp.sin(a1), jnp.sin(a1)], axis=-1)
    ar = angles(pos // GRID_W, HEAD_DIM // 2)
    ac = angles(pos % GRID_W, HEAD_DIM // 2)
    cos2 = jnp.concatenate([jnp.cos(ar), jnp.cos(ar), jnp.cos(ac), jnp.cos(ac)], axis=-1)
    sin2 = jnp.concatenate([-jnp.sin(ar), jnp.sin(ar), -jnp.sin(ac), jnp.sin(ac)], axis=-1)
    return (cos1, sin1, HEAD_DIM // 2), (cos2, sin2, HEAD_DIM // 4)


def _qk(q, k):
    return lax.dot_general(q, k, (((1,), (1,)), ((), ())), preferred_element_type=F32)


def _score_bound_ok(q_gain, k_gain, bias=None):
    bound = HEAD_DIM * QK_SCALE_LOG2 * jnp.max(jnp.abs(q_gain)) * jnp.max(jnp.abs(k_gain))
    if bias is not None:
        bound = bound + LOG2E * jnp.max(jnp.abs(bias))
    return bound <= PLAIN_SOFTMAX_LOG2_LIMIT


def _lane_fold(p):
    out = p[:, :LANES]
    for c in range(1, p.shape[1] // LANES):
        out = out + p[:, c * LANES:(c + 1) * LANES]
    return out


def _online_step(s, vj, m, l, acc):
    m_new = jnp.maximum(m, jnp.max(s, axis=-1, keepdims=True))
    alpha = jnp.exp2(m - m_new)
    p = jnp.exp2(s - m_new)
    l_new = alpha * l + jnp.sum(p, axis=-1, keepdims=True)
    acc_new = alpha * acc + jnp.dot(p.astype(vj.dtype), vj, preferred_element_type=F32)
    return m_new, l_new, acc_new


def _attend(score_fn, k_ref, v_ref, rows, dv, seq, tk, online, scratch):
    def chunk(j):
        start = pl.multiple_of(j * tk, tk)
        return k_ref[pl.ds(start, tk), :], v_ref[pl.ds(start, tk), :]

    if online:
        def body(j, carry):
            kj, vj = chunk(j)
            return _online_step(score_fn(kj), vj, *carry)

        init = (jnp.full((rows, 1), -jnp.inf, F32), jnp.zeros((rows, 1), F32), jnp.zeros((rows, dv), F32))
        _, l, acc = lax.fori_loop(0, seq // tk, body, init)
        return acc / l

    acc_ref, l_ref = scratch
    acc_ref[...] = jnp.zeros_like(acc_ref)
    l_ref[...] = jnp.zeros_like(l_ref)

    def body(j, carry):
        kj, vj = chunk(j)
        p = jnp.exp2(score_fn(kj))
        l_ref[...] += _lane_fold(p)
        acc_ref[...] = jnp.dot(p.astype(vj.dtype), vj, preferred_element_type=F32) + acc_ref[...]
        return carry

    lax.fori_loop(0, seq // tk, body, 0)
    return acc_ref[...] / jnp.sum(l_ref[...], axis=-1, keepdims=True)


def _attend_scratch(rows, dv):
    return [pltpu.VMEM((rows, dv), F32), pltpu.VMEM((rows, LANES), F32)]


def _by_softmax_kind(plain_ref, run):
    @pl.when(plain_ref[0] != 0)
    def _():
        run(False)

    @pl.when(plain_ref[0] == 0)
    def _():
        run(True)


def _plain_flag(plain_ok):
    return plain_ok.astype(jnp.int32).reshape(1)


_FLAG_SPEC = pl.BlockSpec(memory_space=pltpu.SMEM)


def _transpose_kernel(x_ref, o_ref):
    o_ref[...] = x_ref[...].T


def _transposed_columns(x, col_off, width, tm=1024):
    t = x.shape[0]
    tm = min(tm, t)
    assert col_off % width == 0 and t % tm == 0
    cb = col_off // width
    return pl.pallas_call(
        _transpose_kernel,
        grid=(t // tm,),
        in_specs=[pl.BlockSpec((tm, width), lambda i: (i, cb))],
        out_specs=pl.BlockSpec((width, tm), lambda i: (0, i)),
        out_shape=jax.ShapeDtypeStruct((width, t), x.dtype),
        compiler_params=_params("parallel"),
        name="transpose_cols",
    )(x)


def _gqa_kernel(plain_ref, q_ref, k_ref, v_ref, vt_ref, o_ref, acc_ref, l_ref, *, seq, tk):
    tq = q_ref.shape[0]
    m = GQA_GROUP * tq
    q = jnp.concatenate([q_ref[:, g * HEAD_DIM:(g + 1) * HEAD_DIM] for g in range(GQA_GROUP)], axis=0)

    def store(o):
        for g in range(GQA_GROUP):
            o_ref[:, g * HEAD_DIM:(g + 1) * HEAD_DIM] = o[g * tq:(g + 1) * tq].astype(o_ref.dtype)

    def run(online):
        if online:
            store(_attend(lambda kj: _qk(q, kj), k_ref, v_ref, m, HEAD_DIM, seq, tk, True, None))
            return
        acc_ref[...] = jnp.zeros_like(acc_ref)
        l_ref[...] = jnp.zeros_like(l_ref)

        def body(j, carry):
            start = pl.multiple_of(j * tk, tk)
            pt = jnp.exp2(_qk(k_ref[pl.ds(start, tk), :], q))
            l_ref[...] += pt.reshape(tk // 8, 8, m).sum(axis=0)
            acc_ref[...] = (jnp.dot(vt_ref[:, pl.ds(start, tk)], pt.astype(BF16), preferred_element_type=F32)
                            + acc_ref[...])
            return carry

        lax.fori_loop(0, seq // tk, body, 0)
        store((acc_ref[...] / jnp.sum(l_ref[...], axis=0, keepdims=True)).T)

    _by_softmax_kind(plain_ref, run)


def _gqa_attention(plain_ok, qk, v_all, *, batch, seq, tq=512, tk=2048):
    t = batch * seq
    tq, tk = min(tq, seq), min(tk, seq)
    nqb = seq // tq
    gw = GQA_GROUP * HEAD_DIM
    kcb = W_C_Q // HEAD_DIM
    vcb = V_OFF_C // HEAD_DIM
    q_spec = pl.BlockSpec((tq, gw), lambda b, n, i: (b * nqb + i, n))
    v_t = _transposed_columns(v_all, V_OFF_C, W_C_KV)
    return pl.pallas_call(
        functools.partial(_gqa_kernel, seq=seq, tk=tk),
        grid=(batch, GQA_KV_HEADS, nqb),
        in_specs=[_FLAG_SPEC, q_spec,
                  pl.BlockSpec((seq, HEAD_DIM), lambda b, n, i: (b, kcb + n)),
                  pl.BlockSpec((seq, HEAD_DIM), lambda b, n, i: (b, vcb + n)),
                  pl.BlockSpec((HEAD_DIM, seq), lambda b, n, i: (n, b))],
        out_specs=q_spec,
        out_shape=jax.ShapeDtypeStruct((t, W_C_Q), BF16),
        scratch_shapes=[pltpu.VMEM((HEAD_DIM, GQA_GROUP * tq), F32), pltpu.VMEM((8, GQA_GROUP * tq), F32)],
        compiler_params=_params("parallel", "parallel", "parallel"),
        name="gqa_attention",
    )(_plain_flag(plain_ok), qk, qk, v_all, v_t)


def _diff_kernel(plain_ref, q_ref, k_ref, v_ref, lq1_ref, lk1_ref, lq2_ref, lk2_ref, sg_ref, o_ref, *scratch,
                 seq, tk, lam_init):
    tq = q_ref.shape[0]
    dv = 2 * HEAD_DIM
    q1 = q_ref[:, :HEAD_DIM]
    q2 = q_ref[:, HEAD_DIM:]

    def scores(kj):
        return jnp.concatenate([_qk(q1, kj[:, :HEAD_DIM]), _qk(q2, kj[:, HEAD_DIM:])], axis=0)

    def run(online):
        o12 = _attend(scores, k_ref, v_ref, 2 * tq, dv, seq, tk, online, scratch)
        lam = (jnp.exp(jnp.sum(lq1_ref[...] * lk1_ref[...], axis=-1, keepdims=True))
               - jnp.exp(jnp.sum(lq2_ref[...] * lk2_ref[...], axis=-1, keepdims=True)) + lam_init)
        o = o12[:tq] - lam * o12[tq:]
        ms = jnp.mean(o * o, axis=-1, keepdims=True)
        o = o * lax.rsqrt(ms + NORM_EPS) * sg_ref[...] * (1.0 - lam_init)
        o_ref[...] = o.astype(o_ref.dtype)

    _by_softmax_kind(plain_ref, run)


def _diff_attention(plain_ok, qk, v_all, lam_vecs, sub_gain, *, lam_init, batch, seq, tq=512, tk=2048):
    t = batch * seq
    tq, tk = min(tq, seq), min(tk, seq)
    nqb = seq // tq
    pw = 2 * HEAD_DIM
    kcb = W_B_QK // pw
    vcb = V_OFF_B // pw
    vec_spec = pl.BlockSpec((1, HEAD_DIM), lambda b, h, i: (0, 0))
    return pl.pallas_call(
        functools.partial(_diff_kernel, seq=seq, tk=tk, lam_init=lam_init),
        grid=(batch, DIFF_HEADS, nqb),
        in_specs=[
            _FLAG_SPEC,
            pl.BlockSpec((tq, pw), lambda b, h, i: (b * nqb + i, h)),
            pl.BlockSpec((seq, pw), lambda b, h, i: (b, kcb + h)),
            pl.BlockSpec((seq, pw), lambda b, h, i: (b, vcb + h)),
            vec_spec, vec_spec, vec_spec, vec_spec,
            pl.BlockSpec((1, pw), lambda b, h, i: (0, 0)),
        ],
        out_specs=pl.BlockSpec((tq, pw), lambda b, h, i: (b * nqb + i, h)),
        out_shape=jax.ShapeDtypeStruct((t, W_B_V), BF16),
        scratch_shapes=_attend_scratch(2 * tq, pw),
        compiler_params=_params("parallel", "parallel", "parallel"),
        name="diff_attention",
    )(_plain_flag(plain_ok), qk, qk, v_all, *lam_vecs, sub_gain)


NA_Q_ROWS = 4
NA_K_ROWS = NA_Q_ROWS + NA_WIN_R
NA_BLOCK_TYPES = 3
NA_UNROLL = 4


def _na_bias_table(rpb):
    m = jnp.arange(LANES)
    delta = jnp.where(m < GRID_W, m, m - LANES)
    by_delta = rpb.astype(F32)[:, :, jnp.clip(delta, -(NA_WIN_C - 1), NA_WIN_C - 1) + NA_WIN_C - 1] * LOG2E
    heads = rpb.shape[0]
    return pl.pallas_call(
        _na_bias_kernel,
        grid=(heads,),
        in_specs=[pl.BlockSpec((1, 2 * NA_WIN_R - 1, LANES), lambda h: (h, 0, 0))],
        out_specs=pl.BlockSpec((1, NA_BLOCK_TYPES, NA_Q_ROWS * GRID_W, NA_K_ROWS * GRID_W), lambda h: (h, 0, 0, 0)),
        out_shape=jax.ShapeDtypeStruct((heads, NA_BLOCK_TYPES, NA_Q_ROWS * GRID_W, NA_K_ROWS * GRID_W), F32),
        compiler_params=_params("parallel"),
        name="na_bias_table",
    )(by_delta)


def _na_bias_kernel(by_delta_ref, o_ref):
    q = lax.broadcasted_iota(jnp.int32, (GRID_W, GRID_W), 0)
    k = lax.broadcasted_iota(jnp.int32, (GRID_W, GRID_W), 1)
    col_start = jnp.clip(q - NA_WIN_C // 2, 0, GRID_W - NA_WIN_C)
    in_window = (k >= col_start) & (k < col_start + NA_WIN_C)
    masked = jnp.full((GRID_W, GRID_W), NEG_BIG, F32)

    def block(d):
        rows = jnp.broadcast_to(by_delta_ref[0, d:d + 1, :], (GRID_W, LANES))
        toeplitz = pltpu.roll(rows, 0, 1, stride=1, stride_axis=0)
        return jnp.where(in_window, toeplitz[:, :GRID_W], masked)

    by_offset = [block(d) for d in range(2 * NA_WIN_R - 1)]
    for block_type in range(NA_BLOCK_TYPES):
        for rr in range(NA_Q_ROWS):
            q_off, w_off = ((rr, 0), (rr + NA_WIN_R // 2, rr),
                            (rr + NA_K_ROWS - NA_Q_ROWS, NA_K_ROWS - NA_WIN_R))[block_type]
            strip = [by_offset[i - q_off + NA_WIN_R - 1] if w_off <= i < w_off + NA_WIN_R else masked
                     for i in range(NA_K_ROWS)]
            o_ref[0, block_type, rr * GRID_W:(rr + 1) * GRID_W, :] = jnp.concatenate(strip, axis=1)


def _na_kernel(plain_ref, q_ref, k_ref, v_ref, bias_ref, o_ref, *, rows):
    nrb = rows // NA_Q_ROWS
    tq = NA_Q_ROWS * GRID_W

    def run(online):
        def row_block(rb, carry):
            ws = jnp.clip(rb * NA_Q_ROWS - NA_WIN_R // 2, 0, rows - NA_K_ROWS)
            start = pl.multiple_of(ws * GRID_W, GRID_W)
            qstart = pl.multiple_of(rb * tq, tq)
            kw = k_ref[pl.ds(start, NA_K_ROWS * GRID_W), :]
            vw = v_ref[pl.ds(start, NA_K_ROWS * GRID_W), :]
            block_type = jnp.where(rb == 0, 0, jnp.where(rb == nrb - 1, 2, 1))
            s = _qk(q_ref[pl.ds(qstart, tq), :], kw) + bias_ref[0, block_type]
            if online:
                s = s - jnp.max(s, axis=-1, keepdims=True)
            p = jnp.exp2(s)
            l = jnp.sum(p, axis=-1, keepdims=True)
            o = jnp.dot(p.astype(vw.dtype), vw, preferred_element_type=F32) / l
            o_ref[pl.ds(qstart, tq), :] = o.astype(o_ref.dtype)
            return carry

        lax.fori_loop(0, nrb, row_block, 0, unroll=NA_UNROLL)

    _by_softmax_kind(plain_ref, run)


def _na_attention(plain_ok, qk, v_all, bias_tab, *, batch, seq):
    t = batch * seq
    rows = seq // GRID_W
    assert rows >= NA_K_ROWS and rows % (NA_UNROLL * NA_Q_ROWS) == 0
    kcb = W_A // HEAD_DIM
    vcb = V_OFF_A // HEAD_DIM
    return pl.pallas_call(
        functools.partial(_na_kernel, rows=rows),
        grid=(batch, NA_HEADS),
        in_specs=[
            _FLAG_SPEC,
            pl.BlockSpec((seq, HEAD_DIM), lambda b, h: (b, h)),
            pl.BlockSpec((seq, HEAD_DIM), lambda b, h: (b, kcb + h)),
            pl.BlockSpec((seq, HEAD_DIM), lambda b, h: (b, vcb + h)),
            pl.BlockSpec((1, NA_BLOCK_TYPES, NA_Q_ROWS * GRID_W, NA_K_ROWS * GRID_W), lambda b, h: (h, 0, 0, 0)),
        ],
        out_specs=pl.BlockSpec((seq, HEAD_DIM), lambda b, h: (b, h)),
        out_shape=jax.ShapeDtypeStruct((t, W_A), BF16),
        compiler_params=_params("parallel", "parallel"),
        name="na_attention",
    )(_plain_flag(plain_ok), qk, qk, v_all, bias_tab)


def _merge_kernel(oa_ref, ob_ref, oc_ref, wa_ref, wb_ref, wc_ref, ga_ref, gb_ref, gc_ref, o_ref):
    def branch(o_r, w_r, g_r):
        y = jnp.dot(o_r[...], w_r[...], preferred_element_type=F32)
        return jax.nn.sigmoid(g_r[...].astype(F32)) * y

    merged = branch(oa_ref, wa_ref, ga_ref) + branch(ob_ref, wb_ref, gb_ref) + branch(oc_ref, wc_ref, gc_ref)
    o_ref[...] = merged.astype(o_ref.dtype)


def _merge(o_a, o_b, o_c, w_oa, w_ob, w_oc, layer, rest, tm=1024, tn=1024):
    t = o_a.shape[0]
    d = w_oa.shape[2]
    tm, tn = min(tm, t), min(tn, d)
    assert d % tn == 0 and G_OFF % tn == 0
    gcb = G_OFF // tn
    ncb = d // tn

    def act(width):
        return pl.BlockSpec((tm, width), lambda j, i: (i, 0))

    def wgt(width):
        return pl.BlockSpec((None, width, tn), lambda j, i: (layer, 0, j))

    def gate(branch):
        return pl.BlockSpec((tm, tn), lambda j, i: (i, gcb + branch * ncb + j))

    return pl.pallas_call(
        _merge_kernel,
        grid=(ncb, t // tm),
        in_specs=[act(W_A), act(W_B_V), act(W_C_Q), wgt(W_A), wgt(W_B_V), wgt(W_C_Q), gate(0), gate(1), gate(2)],
        out_specs=pl.BlockSpec((tm, tn), lambda j, i: (i, j)),
        out_shape=jax.ShapeDtypeStruct((t, d), BF16),
        compiler_params=_params("parallel", "parallel"),
        name="gated_merge",
    )(o_a, o_b, o_c, w_oa, w_ob, w_oc, rest, rest, rest)


def _row(v):
    return v.reshape(1, -1).astype(F32)


def _input_projection(h, w, l, seq, tables):
    rope1, rope2 = tables
    w_in = w["w_in_bf16"][l]
    tn = 1024
    assert REST_OFF == 2 * W_A and REST_OFF % tn == 0

    def gains(q_gain, k_gain, q_heads, k_heads):
        return jnp.concatenate([jnp.tile(q_gain.astype(F32) * QK_SCALE_LOG2, q_heads),
                                jnp.tile(k_gain.astype(F32), k_heads)]).reshape(1, -1)

    qk_a = _matmul(h, w_in, 0, kind="headnorm", n=REST_OFF, tn=tn,
                   head_gain=gains(w["qn_a"][l], w["kn_a"][l], NA_HEADS, NA_HEADS), name="proj_qk_a")
    rest_args = dict(n=w_in.shape[2] - REST_OFF, tn=tn, col_block0=REST_OFF // tn, name="proj_rest")
    if l in w["late_casts"]:
        rest = _matmul(h, w_in, 0, **rest_args)
    else:
        names = ("w_oa", "w_ob", "w_oc", "w_out", "w_up", "w_down")
        rest, *casts = _matmul(h, w_in, 0, riders=[
            (w[name], l, w["norm_mlp"][l] if name == "w_up" else None) for name in names], **rest_args)
        w["late_casts"][l] = {name: cast[None] for name, cast in zip(names, casts)}
    qk_b = _headnorm(rest, OFF_QB - REST_OFF, gains(w["qn_b"][l], w["kn_b"][l], 2 * DIFF_HEADS, 2 * DIFF_HEADS),
                     rope1, seq)
    qk_c = _headnorm(rest, OFF_QC - REST_OFF, gains(w["qn_c"][l], w["kn_c"][l], GQA_Q_HEADS, GQA_KV_HEADS),
                     rope2, seq)
    return qk_a, qk_b, qk_c, rest


def _layer(x, l, batch, seq, w, tables):
    lam_init = 0.8 - 0.6 * math.exp(-0.3 * l)
    dims = dict(batch=batch, seq=seq)
    h = _rmsnorm(x, w["norm_mix"][l])
    qk_a, qk_b, qk_c, rest = _input_projection(h, w, l, seq, tables)

    o_a = _na_attention(_score_bound_ok(w["qn_a"][l], w["kn_a"][l], w["rpb"][l]), qk_a, rest, w["na_bias"][l],
                        **dims)
    lam_vecs = tuple(_row(w[n][l]) for n in ("lam_q1", "lam_k1", "lam_q2", "lam_k2"))
    o_b = _diff_attention(_score_bound_ok(w["qn_b"][l], w["kn_b"][l]), qk_b, rest, lam_vecs, _row(w["subln_b"][l]),
                          lam_init=lam_init, **dims)
    o_c = _gqa_attention(_score_bound_ok(w["qn_c"][l], w["kn_c"][l]), qk_c, rest, **dims)

    late = w["late_casts"][l]
    merged = _merge(o_a, o_b, o_c, late["w_oa"], late["w_ob"], late["w_oc"], 0, rest)
    x, x_bf16, x_sumsq = _matmul(merged, late["w_out"], 0, kind="residual_stats", residual=x, out_dtype=F32,
                                 name="proj_out")
    up_args = dict(kind="rownorm_relu2", row_sumsq=x_sumsq, name="mlp_up")
    if l + 1 < len(w["w_in_bf16"]) and w["w_in_bf16"][l + 1] is None:
        u, w_in_next = _matmul(x_bf16, late["w_up"], 0, riders=[(w["w_in"], l + 1, None)], **up_args)
        w["w_in_bf16"][l + 1] = w_in_next[None]
    else:
        u = _matmul(x_bf16, late["w_up"], 0, **up_args)
    return _matmul(u, late["w_down"], 0, kind="residual", residual=x, out_dtype=F32, name="mlp_down")


def kernel(x_prompt, x_sample, norm_mix, w_in, qn_a, kn_a, rpb, qn_b, kn_b, lam_q1, lam_k1, lam_q2, lam_k2,
           subln_b, qn_c, kn_c, w_oa, w_ob, w_oc, w_out, norm_mlp, w_up, w_down):
    w = dict(norm_mix=norm_mix, w_in=w_in, qn_a=qn_a, kn_a=kn_a, rpb=rpb, qn_b=qn_b, kn_b=kn_b, lam_q1=lam_q1,
             lam_k1=lam_k1, lam_q2=lam_q2, lam_k2=lam_k2, subln_b=subln_b, qn_c=qn_c, kn_c=kn_c, w_oa=w_oa,
             w_ob=w_ob, w_oc=w_oc, w_out=w_out, norm_mlp=norm_mlp, w_up=w_up, w_down=w_down)
    depth = w_in.shape[0]
    w["w_in_bf16"] = [w_in[:1].astype(BF16)] + [None] * (depth - 1)
    w["late_casts"] = {}
    w["na_bias"] = [_na_bias_table(rpb[l]) for l in range(depth)]

    def trunk(x):
        batch, seq, d = x.shape
        tables = _rope_tables(seq)
        y = x.reshape(batch * seq, d)
        for l in range(depth):
            y = _layer(y, l, batch, seq, w, tables)
        return y.reshape(batch, seq, d)

    return (trunk(x_prompt), trunk(x_sample))
```

```python
import functools
import math

import jax
import jax.numpy as jnp
from jax import lax
from jax.experimental import pallas as pl
from jax.experimental.pallas import tpu as pltpu

F32 = jnp.float32
BF16 = jnp.bfloat16

LANES = 128
HEAD_DIM = 128
GRID_W = 64
NA_HEADS = 12
NA_WIN_R = 8
NA_WIN_C = 16
DIFF_HEADS = 4
GQA_Q_HEADS = 12
GQA_KV_HEADS = 4
GQA_GROUP = GQA_Q_HEADS // GQA_KV_HEADS
ROPE_THETA = 10000.0
NORM_EPS = 1e-6

W_A = NA_HEADS * HEAD_DIM
W_B_QK = 2 * DIFF_HEADS * HEAD_DIM
W_B_V = DIFF_HEADS * 2 * HEAD_DIM
W_C_Q = GQA_Q_HEADS * HEAD_DIM
W_C_KV = GQA_KV_HEADS * HEAD_DIM

OFF_QA = 0
OFF_KA = OFF_QA + W_A
OFF_VA = OFF_KA + W_A
OFF_QB = OFF_VA + W_A
OFF_KB = OFF_QB + W_B_QK
OFF_VB = OFF_KB + W_B_QK
OFF_QC = OFF_VB + W_B_V
OFF_KC = OFF_QC + W_C_Q
OFF_VC = OFF_KC + W_C_KV
OFF_GA = OFF_VC + W_C_KV
REST_OFF = OFF_VA
V_OFF_A = OFF_VA - REST_OFF
V_OFF_B = OFF_VB - REST_OFF
V_OFF_C = OFF_VC - REST_OFF
G_OFF = OFF_GA - REST_OFF

V7X_VMEM_LIMIT_BYTES = 62 * 1024 * 1024

LOG2E = 1.4426950408889634
QK_SCALE_LOG2 = HEAD_DIM ** -0.5 * LOG2E
NEG_BIG = -1e30
PLAIN_SOFTMAX_LOG2_LIMIT = 64.0


def _params(*semantics):
    return pltpu.CompilerParams(dimension_semantics=semantics, vmem_limit_bytes=V7X_VMEM_LIMIT_BYTES)


def _rmsnorm_kernel(x_ref, g_ref, o_ref):
    x = x_ref[...]
    ms = jnp.mean(x * x, axis=-1, keepdims=True)
    o_ref[...] = (x * lax.rsqrt(ms + NORM_EPS) * g_ref[...]).astype(o_ref.dtype)


def _rmsnorm(x, gain, tm=512):
    t, d = x.shape
    tm = min(tm, t)
    return pl.pallas_call(
        _rmsnorm_kernel,
        grid=(t // tm,),
        in_specs=[pl.BlockSpec((tm, d), lambda i: (i, 0)), pl.BlockSpec((1, d), lambda i: (0, 0))],
        out_specs=pl.BlockSpec((tm, d), lambda i: (i, 0)),
        out_shape=jax.ShapeDtypeStruct((t, d), BF16),
        compiler_params=_params("parallel"),
        name="rmsnorm",
    )(x, gain.reshape(1, d).astype(F32))


def _store_normed_heads(x, g_ref, o_ref, rope_refs=None, rot_half=0, mean_on_mxu=False):
    averager = jnp.full((HEAD_DIM, HEAD_DIM), 1.0 / HEAD_DIM, BF16)
    for h in range(x.shape[1] // HEAD_DIM):
        cols = slice(h * HEAD_DIM, (h + 1) * HEAD_DIM)
        xh = x[:, cols]
        if mean_on_mxu:
            ms = jnp.dot((xh * xh).astype(BF16), averager, preferred_element_type=F32)
        else:
            ms = jnp.mean(xh * xh, axis=-1, keepdims=True)
        y = xh * lax.rsqrt(ms + NORM_EPS) * g_ref[:, cols]
        if rope_refs is not None:
            cos_ref, sin_ref = rope_refs
            y = y * cos_ref[...] + _rotate_pairs(y, rot_half) * sin_ref[...]
        o_ref[:, cols] = y.astype(o_ref.dtype)


def _matmul_kernel(*refs, kind, nk, kdim, n_in, rider_gains):
    ins, outs = refs[:n_in], refs[n_in:]
    for mode, o_ref in zip(reversed(rider_gains), reversed(outs)):
        if mode:
            src, gain, ins = ins[-2][...], ins[-1][...], ins[:-2]
            if mode == "rmsnorm":
                src = src * lax.rsqrt(jnp.mean(src * src, axis=-1, keepdims=True) + NORM_EPS)
            src = src * gain
        else:
            src, ins = ins[-1][...], ins[:-1]
        o_ref[...] = src.astype(o_ref.dtype)
    refs = ins + outs[:len(outs) - len(rider_gains)]
    a_ref, b_ref = refs[:2]
    prod = jnp.dot(a_ref[...], b_ref[...], preferred_element_type=F32)
    if kind == "residual":
        res_ref, o_ref = refs[2:]
        if nk == 1:
            o_ref[...] = prod + res_ref[...]
        else:
            o_ref[...] = prod + jnp.where(pl.program_id(2) == 0, res_ref[...], o_ref[...])
    elif kind == "residual_stats":
        res_ref, o_ref, ob_ref, ss_ref = refs[2:]
        x = prod + res_ref[...]
        o_ref[...] = x
        ob_ref[...] = x.astype(ob_ref.dtype)
        ss_ref[...] = _lane_fold(x * x)
    elif kind == "rownorm_relu2":
        ss_ref, o_ref = refs[2:]
        ms = jnp.sum(ss_ref[...], axis=-1, keepdims=True) * (1.0 / kdim)
        o_ref[...] = jnp.square(jnp.maximum(prod * lax.rsqrt(ms + NORM_EPS), 0.0)).astype(o_ref.dtype)
    elif kind == "headnorm":
        g_ref, o_ref = refs[2:]
        _store_normed_heads(prod, g_ref, o_ref)
    else:
        (o_ref,) = refs[2:]
        o_ref[...] = prod.astype(o_ref.dtype)


def _rider_rows(total_rows, steps):
    return next(r for r in range(16, total_rows + 1, 16) if total_rows % r == 0 and total_rows // r <= steps)


def _matmul(a, w, layer, *, kind="plain", residual=None, row_sumsq=None, head_gain=None, riders=(),
            n=None, col_block0=0, out_dtype=BF16, tm=1024, tn=1024, tk=4096, name="matmul"):
    m, kdim = a.shape
    n = w.shape[2] if n is None else n
    tm, tn, tk = min(tm, m), min(tn, n), min(tk, kdim)
    nk = kdim // tk
    tile = pl.BlockSpec((tm, tn), lambda i, j, k: (i, j))
    in_specs = [pl.BlockSpec((tm, tk), lambda i, j, k: (i, k)),
                pl.BlockSpec((None, tk, tn), lambda i, j, k: (layer, k, col_block0 + j))]
    args = [a, w]
    out_specs, out_shape = tile, jax.ShapeDtypeStruct((m, n), out_dtype)
    if kind == "headnorm":
        in_specs.append(pl.BlockSpec((1, tn), lambda i, j, k: (0, j)))
        args.append(head_gain)
    if kind in ("residual", "residual_stats"):
        in_specs.append(tile)
        args.append(residual)
    if kind == "residual_stats":
        out_specs = (tile, tile, pl.BlockSpec((tm, LANES), lambda i, j, k: (i, j)))
        out_shape = (out_shape, jax.ShapeDtypeStruct((m, n), BF16),
                     jax.ShapeDtypeStruct((m, n // tn * LANES), F32))
    if kind == "rownorm_relu2":
        in_specs.append(pl.BlockSpec((tm, row_sumsq.shape[1]), lambda i, j, k: (i, 0)))
        args.append(row_sumsq)
    assert nk == 1 or (kind == "residual" and out_dtype == F32)
    if riders:
        assert nk == 1
        nj = n // tn
        out_specs, out_shape = [out_specs], [out_shape]
        for src, src_layer, gain in riders:
            _, rows_total, cols = src.shape
            rows = _rider_rows(rows_total, (m // tm) * nj)
            last = rows_total // rows - 1

            def slab(i, j, k, last=last):
                return jnp.minimum(i * nj + j, last)

            in_specs.append(pl.BlockSpec((None, rows, cols), lambda i, j, k, s=slab, l=src_layer: (l, s(i, j, k), 0)))
            args.append(src)
            if gain is not None and gain.ndim == 2:
                in_specs.append(pl.BlockSpec((1, cols), lambda i, j, k: (0, 0)))
                args.append(gain.astype(F32))
            elif gain is not None:
                in_specs.append(pl.BlockSpec((rows, 1), lambda i, j, k, s=slab: (s(i, j, k), 0)))
                args.append(gain.reshape(rows_total, 1).astype(F32))
            out_specs.append(pl.BlockSpec((rows, cols), lambda i, j, k, s=slab: (s(i, j, k), 0)))
            out_shape.append(jax.ShapeDtypeStruct((rows_total, cols), BF16))
    return pl.pallas_call(
        functools.partial(_matmul_kernel, kind=kind, nk=nk, kdim=kdim, n_in=len(args),
                          rider_gains=tuple("" if gain is None else "rmsnorm" if gain.ndim == 2 else "scale"
                                            for _, _, gain in riders)),
        grid=(m // tm, n // tn, nk),
        in_specs=in_specs,
        out_specs=out_specs,
        out_shape=out_shape,
        compiler_params=_params(*(("arbitrary",) * 3 if riders else ("parallel", "parallel", "arbitrary"))),
        name=name,
    )(*args)


def _rotate_pairs(y, half):
    width = y.shape[-1]
    if 2 * half == width:
        return pltpu.roll(y, half, 1)
    lane = lax.broadcasted_iota(jnp.int32, y.shape, 1)
    first = (lane % (2 * half)) < half
    return jnp.where(first, pltpu.roll(y, width - half, 1), pltpu.roll(y, half, 1))


def _headnorm_kernel(x_ref, g_ref, cos_ref, sin_ref, o_ref, *, rot_half):
    _store_normed_heads(x_ref[...].astype(F32), g_ref, o_ref, (cos_ref, sin_ref), rot_half, mean_on_mxu=True)


def _headnorm(x, col_off, head_gain, rope, seq, tm=1024, bw=512):
    t = x.shape[0]
    width = head_gain.shape[1]
    tm = min(tm, seq)
    assert col_off % bw == 0 and width % bw == 0 and seq % tm == 0
    cb = col_off // bw
    cos, sin, rot_half = rope
    nsb = seq // tm
    tab_spec = pl.BlockSpec((tm, HEAD_DIM), lambda i, j: (i % nsb, 0))
    return pl.pallas_call(
        functools.partial(_headnorm_kernel, rot_half=rot_half),
        grid=(t // tm, width // bw),
        in_specs=[pl.BlockSpec((tm, bw), lambda i, j: (i, cb + j)), pl.BlockSpec((1, bw), lambda i, j: (0, j)),
                  tab_spec, tab_spec],
        out_specs=pl.BlockSpec((tm, bw), lambda i, j: (i, j)),
        out_shape=jax.ShapeDtypeStruct((t, width), BF16),
        compiler_params=_params("parallel", "parallel"),
        name="headnorm",
    )(x, head_gain, cos, sin)


def _rope_tables(seq):
    pos = jnp.arange(seq, dtype=jnp.int32)

    def angles(p, dr):
        inv = ROPE_THETA ** (-jnp.arange(0, dr, 2, dtype=F32) / dr)
        return p.astype(F32)[:, None] * inv[None, :]

    a1 = angles(pos, HEAD_DIM)
    cos1 = jnp.concatenate([jnp.cos(a1), jnp.cos(a1)], axis=-1)
    sin1 = jnp.concatenate([-jnp.sin(a1), jnp.sin(a1)], axis=-1)
    ar = angles(pos // GRID_W, HEAD_DIM // 2)
    ac = angles(pos % GRID_W, HEAD_DIM // 2)
    cos2 = jnp.concatenate([jnp.cos(ar), jnp.cos(ar), jnp.cos(ac), jnp.cos(ac)], axis=-1)
    sin2 = jnp.concatenate([-jnp.sin(ar), jnp.sin(ar), -jnp.sin(ac), jnp.sin(ac)], axis=-1)
    return (cos1, sin1, HEAD_DIM // 2), (cos2, sin2, HEAD_DIM // 4)


def _qk(q, k):
    return lax.dot_general(q, k, (((1,), (1,)), ((), ())), preferred_element_type=F32)


def _score_bound_ok(q_gain, k_gain, bias=None):
    bound = HEAD_DIM * QK_SCALE_LOG2 * jnp.max(jnp.abs(q_gain)) * jnp.max(jnp.abs(k_gain))
    if bias is not None:
        bound = bound + LOG2E * jnp.max(jnp.abs(bias))
    return bound <= PLAIN_SOFTMAX_LOG2_LIMIT


def _lane_fold(p):
    out = p[:, :LANES]
    for c in range(1, p.shape[1] // LANES):
        out = out + p[:, c * LANES:(c + 1) * LANES]
    return out


def _online_step(s, vj, m, l, acc):
    m_new = jnp.maximum(m, jnp.max(s, axis=-1, keepdims=True))
    alpha = jnp.exp2(m - m_new)
    p = jnp.exp2(s - m_new)
    l_new = alpha * l + jnp.sum(p, axis=-1, keepdims=True)
    acc_new = alpha * acc + jnp.dot(p.astype(vj.dtype), vj, preferred_element_type=F32)
    return m_new, l_new, acc_new


def _attend(score_fn, k_ref, v_ref, rows, dv, seq, tk, online, scratch):
    def chunk(j):
        start = pl.multiple_of(j * tk, tk)
        return k_ref[pl.ds(start, tk), :], v_ref[pl.ds(start, tk), :]

    if online:
        def body(j, carry):
            kj, vj = chunk(j)
            return _online_step(score_fn(kj), vj, *carry)

        init = (jnp.full((rows, 1), -jnp.inf, F32), jnp.zeros((rows, 1), F32), jnp.zeros((rows, dv), F32))
        _, l, acc = lax.fori_loop(0, seq // tk, body, init)
        return acc / l

    acc_ref, l_ref = scratch
    acc_ref[...] = jnp.zeros_like(acc_ref)
    l_ref[...] = jnp.zeros_like(l_ref)

    def body(j, carry):
        kj, vj = chunk(j)
        p = jnp.exp2(score_fn(kj))
        l_ref[...] += _lane_fold(p)
        acc_ref[...] = jnp.dot(p.astype(vj.dtype), vj, preferred_element_type=F32) + acc_ref[...]
        return carry

    lax.fori_loop(0, seq // tk, body, 0)
    return acc_ref[...] / jnp.sum(l_ref[...], axis=-1, keepdims=True)


def _attend_scratch(rows, dv):
    return [pltpu.VMEM((rows, dv), F32), pltpu.VMEM((rows, LANES), F32)]


def _by_softmax_kind(plain_ref, run):
    @pl.when(plain_ref[0] != 0)
    def _():
        run(False)

    @pl.when(plain_ref[0] == 0)
    def _():
        run(True)


def _plain_flag(plain_ok):
    return plain_ok.astype(jnp.int32).reshape(1)


_FLAG_SPEC = pl.BlockSpec(memory_space=pltpu.SMEM)


def _transpose_kernel(x_ref, o_ref):
    o_ref[...] = x_ref[...].T


def _transposed_columns(x, col_off, width, tm=1024):
    t = x.shape[0]
    tm = min(tm, t)
    assert col_off % width == 0 and t % tm == 0
    cb = col_off // width
    return pl.pallas_call(
        _transpose_kernel,
        grid=(t // tm,),
        in_specs=[pl.BlockSpec((tm, width), lambda i: (i, cb))],
        out_specs=pl.BlockSpec((width, tm), lambda i: (0, i)),
        out_shape=jax.ShapeDtypeStruct((width, t), x.dtype),
        compiler_params=_params("parallel"),
        name="transpose_cols",
    )(x)


def _gqa_kernel(plain_ref, q_ref, k_ref, v_ref, vt_ref, o_ref, acc_ref, l_ref, *, seq, tk):
    tq = q_ref.shape[0]
    m = GQA_GROUP * tq
    q = jnp.concatenate([q_ref[:, g * HEAD_DIM:(g + 1) * HEAD_DIM] for g in range(GQA_GROUP)], axis=0)

    def store(o):
        for g in range(GQA_GROUP):
            o_ref[:, g * HEAD_DIM:(g + 1) * HEAD_DIM] = o[g * tq:(g + 1) * tq].astype(o_ref.dtype)

    def run(online):
        if online:
            store(_attend(lambda kj: _qk(q, kj), k_ref, v_ref, m, HEAD_DIM, seq, tk, True, None))
            return
        acc_ref[...] = jnp.zeros_like(acc_ref)
        l_ref[...] = jnp.zeros_like(l_ref)

        def body(j, carry):
            start = pl.multiple_of(j * tk, tk)
            pt = jnp.exp2(_qk(k_ref[pl.ds(start, tk), :], q))
            l_ref[...] += pt.reshape(tk // 8, 8, m).sum(axis=0)
            acc_ref[...] = (jnp.dot(vt_ref[:, pl.ds(start, tk)], pt.astype(BF16), preferred_element_type=F32)
                            + acc_ref[...])
            return carry

        lax.fori_loop(0, seq // tk, body, 0)
        store((acc_ref[...] / jnp.sum(l_ref[...], axis=0, keepdims=True)).T)

    _by_softmax_kind(plain_ref, run)


def _gqa_attention(plain_ok, qk, v_all, *, batch, seq, tq=512, tk=2048):
    t = batch * seq
    tq, tk = min(tq, seq), min(tk, seq)
    nqb = seq // tq
    gw = GQA_GROUP * HEAD_DIM
    kcb = W_C_Q // HEAD_DIM
    vcb = V_OFF_C // HEAD_DIM
    q_spec = pl.BlockSpec((tq, gw), lambda b, n, i: (b * nqb + i, n))
    v_t = _transposed_columns(v_all, V_OFF_C, W_C_KV)
    return pl.pallas_call(
        functools.partial(_gqa_kernel, seq=seq, tk=tk),
        grid=(batch, GQA_KV_HEADS, nqb),
        in_specs=[_FLAG_SPEC, q_spec,
                  pl.BlockSpec((seq, HEAD_DIM), lambda b, n, i: (b, kcb + n)),
                  pl.BlockSpec((seq, HEAD_DIM), lambda b, n, i: (b, vcb + n)),
                  pl.BlockSpec((HEAD_DIM, seq), lambda b, n, i: (n, b))],
        out_specs=q_spec,
        out_shape=jax.ShapeDtypeStruct((t, W_C_Q), BF16),
        scratch_shapes=[pltpu.VMEM((HEAD_DIM, GQA_GROUP * tq), F32), pltpu.VMEM((8, GQA_GROUP * tq), F32)],
        compiler_params=_params("parallel", "parallel", "parallel"),
        name="gqa_attention",
    )(_plain_flag(plain_ok), qk, qk, v_all, v_t)


def _diff_kernel(plain_ref, q_ref, k_ref, v_ref, lq1_ref, lk1_ref, lq2_ref, lk2_ref, sg_ref, o_ref, *scratch,
                 seq, tk, lam_init):
    tq = q_ref.shape[0]
    dv = 2 * HEAD_DIM
    q1 = q_ref[:, :HEAD_DIM]
    q2 = q_ref[:, HEAD_DIM:]

    def scores(kj):
        return jnp.concatenate([_qk(q1, kj[:, :HEAD_DIM]), _qk(q2, kj[:, HEAD_DIM:])], axis=0)

    def run(online):
        o12 = _attend(scores, k_ref, v_ref, 2 * tq, dv, seq, tk, online, scratch)
        lam = (jnp.exp(jnp.sum(lq1_ref[...] * lk1_ref[...], axis=-1, keepdims=True))
               - jnp.exp(jnp.sum(lq2_ref[...] * lk2_ref[...], axis=-1, keepdims=True)) + lam_init)
        o = o12[:tq] - lam * o12[tq:]
        ms = jnp.mean(o * o, axis=-1, keepdims=True)
        o = o * lax.rsqrt(ms + NORM_EPS) * sg_ref[...] * (1.0 - lam_init)
        o_ref[...] = o.astype(o_ref.dtype)

    _by_softmax_kind(plain_ref, run)


def _diff_attention(plain_ok, qk, v_all, lam_vecs, sub_gain, *, lam_init, batch, seq, tq=512, tk=2048):
    t = batch * seq
    tq, tk = min(tq, seq), min(tk, seq)
    nqb = seq // tq
    pw = 2 * HEAD_DIM
    kcb = W_B_QK // pw
    vcb = V_OFF_B // pw
    vec_spec = pl.BlockSpec((1, HEAD_DIM), lambda b, h, i: (0, 0))
    return pl.pallas_call(
        functools.partial(_diff_kernel, seq=seq, tk=tk, lam_init=lam_init),
        grid=(batch, DIFF_HEADS, nqb),
        in_specs=[
            _FLAG_SPEC,
            pl.BlockSpec((tq, pw), lambda b, h, i: (b * nqb + i, h)),
            pl.BlockSpec((seq, pw), lambda b, h, i: (b, kcb + h)),
            pl.BlockSpec((seq, pw), lambda b, h, i: (b, vcb + h)),
            vec_spec, vec_spec, vec_spec, vec_spec,
            pl.BlockSpec((1, pw), lambda b, h, i: (0, 0)),
        ],
        out_specs=pl.BlockSpec((tq, pw), lambda b, h, i: (b * nqb + i, h)),
        out_shape=jax.ShapeDtypeStruct((t, W_B_V), BF16),
        scratch_shapes=_attend_scratch(2 * tq, pw),
        compiler_params=_params("parallel", "parallel", "parallel"),
        name="diff_attention",
    )(_plain_flag(plain_ok), qk, qk, v_all, *lam_vecs, sub_gain)


NA_Q_ROWS = 4
NA_K_ROWS = NA_Q_ROWS + NA_WIN_R
NA_BLOCK_TYPES = 3
NA_UNROLL = 4


def _na_bias_table(rpb):
    m = jnp.arange(LANES)
    delta = jnp.where(m < GRID_W, m, m - LANES)
    by_delta = rpb.astype(F32)[:, :, jnp.clip(delta, -(NA_WIN_C - 1), NA_WIN_C - 1) + NA_WIN_C - 1] * LOG2E
    heads = rpb.shape[0]
    return pl.pallas_call(
        _na_bias_kernel,
        grid=(heads,),
        in_specs=[pl.BlockSpec((1, 2 * NA_WIN_R - 1, LANES), lambda h: (h, 0, 0))],
        out_specs=pl.BlockSpec((1, NA_BLOCK_TYPES, NA_Q_ROWS * GRID_W, NA_K_ROWS * GRID_W), lambda h: (h, 0, 0, 0)),
        out_shape=jax.ShapeDtypeStruct((heads, NA_BLOCK_TYPES, NA_Q_ROWS * GRID_W, NA_K_ROWS * GRID_W), F32),
        compiler_params=_params("parallel"),
        name="na_bias_table",
    )(by_delta)


def _na_bias_kernel(by_delta_ref, o_ref):
    q = lax.broadcasted_iota(jnp.int32, (GRID_W, GRID_W), 0)
    k = lax.broadcasted_iota(jnp.int32, (GRID_W, GRID_W), 1)
    col_start = jnp.clip(q - NA_WIN_C // 2, 0, GRID_W - NA_WIN_C)
    in_window = (k >= col_start) & (k < col_start + NA_WIN_C)
    masked = jnp.full((GRID_W, GRID_W), NEG_BIG, F32)

    def block(d):
        rows = jnp.broadcast_to(by_delta_ref[0, d:d + 1, :], (GRID_W, LANES))
        toeplitz = pltpu.roll(rows, 0, 1, stride=1, stride_axis=0)
        return jnp.where(in_window, toeplitz[:, :GRID_W], masked)

    by_offset = [block(d) for d in range(2 * NA_WIN_R - 1)]
    for block_type in range(NA_BLOCK_TYPES):
        for rr in range(NA_Q_ROWS):
            q_off, w_off = ((rr, 0), (rr + NA_WIN_R // 2, rr),
                            (rr + NA_K_ROWS - NA_Q_ROWS, NA_K_ROWS - NA_WIN_R))[block_type]
            strip = [by_offset[i - q_off + NA_WIN_R - 1] if w_off <= i < w_off + NA_WIN_R else masked
                     for i in range(NA_K_ROWS)]
            o_ref[0, block_type, rr * GRID_W:(rr + 1) * GRID_W, :] = jnp.concatenate(strip, axis=1)


def _na_kernel(plain_ref, q_ref, k_ref, v_ref, bias_ref, o_ref, *, rows):
    nrb = rows // NA_Q_ROWS
    tq = NA_Q_ROWS * GRID_W

    def run(online):
        def row_block(rb, carry):
            ws = jnp.clip(rb * NA_Q_ROWS - NA_WIN_R // 2, 0, rows - NA_K_ROWS)
            start = pl.multiple_of(ws * GRID_W, GRID_W)
            qstart = pl.multiple_of(rb * tq, tq)
            kw = k_ref[pl.ds(start, NA_K_ROWS * GRID_W), :]
            vw = v_ref[pl.ds(start, NA_K_ROWS * GRID_W), :]
            block_type = jnp.where(rb == 0, 0, jnp.where(rb == nrb - 1, 2, 1))
            s = _qk(q_ref[pl.ds(qstart, tq), :], kw) + bias_ref[0, block_type]
            if online:
                s = s - jnp.max(s, axis=-1, keepdims=True)
            p = jnp.exp2(s)
            l = jnp.sum(p, axis=-1, keepdims=True)
            o = jnp.dot(p.astype(vw.dtype), vw, preferred_element_type=F32) / l
            o_ref[pl.ds(qstart, tq), :] = o.astype(o_ref.dtype)
            return carry

        lax.fori_loop(0, nrb, row_block, 0, unroll=NA_UNROLL)

    _by_softmax_kind(plain_ref, run)


def _na_attention(plain_ok, qk, v_all, bias_tab, *, batch, seq):
    t = batch * seq
    rows = seq // GRID_W
    assert rows >= NA_K_ROWS and rows % (NA_UNROLL * NA_Q_ROWS) == 0
    kcb = W_A // HEAD_DIM
    vcb = V_OFF_A // HEAD_DIM
    return pl.pallas_call(
        functools.partial(_na_kernel, rows=rows),
        grid=(batch, NA_HEADS),
        in_specs=[
            _FLAG_SPEC,
            pl.BlockSpec((seq, HEAD_DIM), lambda b, h: (b, h)),
            pl.BlockSpec((seq, HEAD_DIM), lambda b, h: (b, kcb + h)),
            pl.BlockSpec((seq, HEAD_DIM), lambda b, h: (b, vcb + h)),
            pl.BlockSpec((1, NA_BLOCK_TYPES, NA_Q_ROWS * GRID_W, NA_K_ROWS * GRID_W), lambda b, h: (h, 0, 0, 0)),
        ],
        out_specs=pl.BlockSpec((seq, HEAD_DIM), lambda b, h: (b, h)),
        out_shape=jax.ShapeDtypeStruct((t, W_A), BF16),
        compiler_params=_params("parallel", "parallel"),
        name="na_attention",
    )(_plain_flag(plain_ok), qk, qk, v_all, bias_tab)


def _merge_kernel(oa_ref, ob_ref, oc_ref, wa_ref, wb_ref, wc_ref, ga_ref, gb_ref, gc_ref, o_ref):
    def branch(o_r, w_r, g_r):
        y = jnp.dot(o_r[...], w_r[...], preferred_element_type=F32)
        return jax.nn.sigmoid(g_r[...].astype(F32)) * y

    merged = branch(oa_ref, wa_ref, ga_ref) + branch(ob_ref, wb_ref, gb_ref) + branch(oc_ref, wc_ref, gc_ref)
    o_ref[...] = merged.astype(o_ref.dtype)


def _merge(o_a, o_b, o_c, w_oa, w_ob, w_oc, layer, rest, tm=1024, tn=1024):
    t = o_a.shape[0]
    d = w_oa.shape[2]
    tm, tn = min(tm, t), min(tn, d)
    assert d % tn == 0 and G_OFF % tn == 0
    gcb = G_OFF // tn
    ncb = d // tn

    def act(width):
        return pl.BlockSpec((tm, width), lambda j, i: (i, 0))

    def wgt(width):
        return pl.BlockSpec((None, width, tn), lambda j, i: (layer, 0, j))

    def gate(branch):
        return pl.BlockSpec((tm, tn), lambda j, i: (i, gcb + branch * ncb + j))

    return pl.pallas_call(
        _merge_kernel,
        grid=(ncb, t // tm),
        in_specs=[act(W_A), act(W_B_V), act(W_C_Q), wgt(W_A), wgt(W_B_V), wgt(W_C_Q), gate(0), gate(1), gate(2)],
        out_specs=pl.BlockSpec((tm, tn), lambda j, i: (i, j)),
        out_shape=jax.ShapeDtypeStruct((t, d), BF16),
        compiler_params=_params("parallel", "parallel"),
        name="gated_merge",
    )(o_a, o_b, o_c, w_oa, w_ob, w_oc, rest, rest, rest)


def _row(v):
    return v.reshape(1, -1).astype(F32)


def _input_projection(h, w, l, seq, tables):
    rope1, rope2 = tables
    w_in = w["w_in_bf16"][l]
    tn = 1024
    assert REST_OFF == 2 * W_A and REST_OFF % tn == 0

    def gains(q_gain, k_gain, q_heads, k_heads):
        return jnp.concatenate([jnp.tile(q_gain.astype(F32) * QK_SCALE_LOG2, q_heads),
                                jnp.tile(k_gain.astype(F32), k_heads)]).reshape(1, -1)

    qk_a = _matmul(h, w_in, 0, kind="headnorm", n=REST_OFF, tn=tn,
                   head_gain=gains(w["qn_a"][l], w["kn_a"][l], NA_HEADS, NA_HEADS), name="proj_qk_a")
    rest_args = dict(n=w_in.shape[2] - REST_OFF, tn=tn, col_block0=REST_OFF // tn, name="proj_rest")
    if l in w["late_casts"]:
        rest = _matmul(h, w_in, 0, **rest_args)
    else:
        names = ("w_oa", "w_ob", "w_oc", "w_out", "w_up", "w_down")
        rest, *casts = _matmul(h, w_in, 0, riders=[
            (w[name], l, w["norm_mlp"][l] if name == "w_up" else None) for name in names], **rest_args)
        w["late_casts"][l] = {name: cast[None] for name, cast in zip(names, casts)}
    qk_b = _headnorm(rest, OFF_QB - REST_OFF, gains(w["qn_b"][l], w["kn_b"][l], 2 * DIFF_HEADS, 2 * DIFF_HEADS),
                     rope1, seq)
    qk_c = _headnorm(rest, OFF_QC - REST_OFF, gains(w["qn_c"][l], w["kn_c"][l], GQA_Q_HEADS, GQA_KV_HEADS),
                     rope2, seq)
    return qk_a, qk_b, qk_c, rest


def _layer(x, l, batch, seq, w, tables):
    lam_init = 0.8 - 0.6 * math.exp(-0.3 * l)
    dims = dict(batch=batch, seq=seq)
    if l == 0 and w["next_trunk_h"] is not None:
        h, w["next_trunk_h"] = w["next_trunk_h"], None
    else:
        h = _rmsnorm(x, w["norm_mix"][l])
    qk_a, qk_b, qk_c, rest = _input_projection(h, w, l, seq, tables)

    o_a = _na_attention(_score_bound_ok(w["qn_a"][l], w["kn_a"][l], w["rpb"][l]), qk_a, rest, w["na_bias"][l],
                        **dims)
    lam_vecs = tuple(_row(w[n][l]) for n in ("lam_q1", "lam_k1", "lam_q2", "lam_k2"))
    o_b = _diff_attention(_score_bound_ok(w["qn_b"][l], w["kn_b"][l]), qk_b, rest, lam_vecs, _row(w["subln_b"][l]),
                          lam_init=lam_init, **dims)
    o_c = _gqa_attention(_score_bound_ok(w["qn_c"][l], w["kn_c"][l]), qk_c, rest, **dims)

    late = w["late_casts"][l]
    merged = _merge(o_a, o_b, o_c, late["w_oa"], late["w_ob"], late["w_oc"], 0, rest)
    x, x_bf16, x_sumsq = _matmul(merged, late["w_out"], 0, kind="residual_stats", residual=x, out_dtype=F32,
                                 name="proj_out")
    up_args = dict(kind="rownorm_relu2", row_sumsq=x_sumsq, name="mlp_up")
    if l + 1 < len(w["w_in_bf16"]) and w["w_in_bf16"][l + 1] is None:
        u, w_in_next = _matmul(x_bf16, late["w_up"], 0, riders=[(w["w_in"], l + 1, None)], **up_args)
        w["w_in_bf16"][l + 1] = w_in_next[None]
    elif l + 1 == len(w["w_in_bf16"]) and w["next_trunk_x"] is not None:
        u, w["next_trunk_h"] = _matmul(x_bf16, late["w_up"], 0, riders=[
            (w["next_trunk_x"], 0, _row(w["norm_mix"][0]))], **up_args)
        w["next_trunk_x"] = None
    else:
        u = _matmul(x_bf16, late["w_up"], 0, **up_args)
    return _matmul(u, late["w_down"], 0, kind="residual", residual=x, out_dtype=F32, name="mlp_down")


def kernel(x_prompt, x_sample, norm_mix, w_in, qn_a, kn_a, rpb, qn_b, kn_b, lam_q1, lam_k1, lam_q2, lam_k2,
           subln_b, qn_c, kn_c, w_oa, w_ob, w_oc, w_out, norm_mlp, w_up, w_down):
    w = dict(norm_mix=norm_mix, w_in=w_in, qn_a=qn_a, kn_a=kn_a, rpb=rpb, qn_b=qn_b, kn_b=kn_b, lam_q1=lam_q1,
             lam_k1=lam_k1, lam_q2=lam_q2, lam_k2=lam_k2, subln_b=subln_b, qn_c=qn_c, kn_c=kn_c, w_oa=w_oa,
             w_ob=w_ob, w_oc=w_oc, w_out=w_out, norm_mlp=norm_mlp, w_up=w_up, w_down=w_down)
    depth = w_in.shape[0]
    w["w_in_bf16"] = [w_in[:1].astype(BF16)] + [None] * (depth - 1)
    w["late_casts"] = {}
    w["next_trunk_x"] = x_sample.reshape(1, -1, x_sample.shape[-1])
    w["next_trunk_h"] = None
    w["na_bias"] = [_na_bias_table(rpb[l]) for l in range(depth)]

    def trunk(x):
        batch, seq, d = x.shape
        tables = _rope_tables(seq)
        y = x.reshape(batch * seq, d)
        for l in range(depth):
            y = _layer(y, l, batch, seq, w, tables)
        return y.reshape(batch, seq, d)

    return (trunk(x_prompt), trunk(x_sample))
```
